```python
import math
import jax
import jax.numpy as jnp
from jax import lax
import numpy as np

D_MODEL = 1024
BATCH = 8
SEQ = 2048
DEPTH = 2
DEC_BATCH = 32
DEC_SEQ = 1
PAST_LEN = 8192
PAGE_SIZE = 128

H_A = 4
DH_A = 64
DV_A = 2 * DH_A
D_A = H_A * DV_A
H_B = 8
P_B = 64
D_B = H_B * P_B
G_B = 2
N_B = 128
CONV_B = D_B + 2 * G_B * N_B
H_C = 4
DK_C = 128
DV_C = 128
D_C = H_C * DV_C
CONV_C = 2 * H_C * DK_C + D_C
CONV_W = 4
D_MIX = D_A + D_B + D_C
IN_SPLITS = (2 * H_A * DH_A, 2 * H_A * DH_A, D_A, CONV_B, H_B, CONV_C, H_C, H_C, D_MIX)
N_IN = sum(IN_SPLITS)
REL_BUCKETS = 32
REL_MAX_DIST = 128
Q_BLOCK = 128
CHUNK = 64
EPS = 1e-6
POOL_NUM = 5
POOL_DEN = 4

kernel_name = 'hybrid_diffattn_ssd_gdn_decoder_step'


def rms_norm(x):
    xf = x.astype(jnp.float32)
    return (xf * lax.rsqrt(jnp.mean(xf * xf, axis=-1, keepdims=True) + EPS)).astype(x.dtype)


def l2_norm(x):
    return x * lax.rsqrt(jnp.sum(x * x, axis=-1, keepdims=True) + EPS)


def causal_conv(x, buf, w):
    L = x.shape[1]
    xp = jnp.concatenate([buf.astype(x.dtype), x], axis=1)
    y = xp[:, 0:L] * w[0]
    for i in range(1, CONV_W):
        y = y + xp[:, i:i + L] * w[i]
    return y, xp[:, L:]


def rel_pos_bias(q_pos, k_pos, table):
    n = jnp.maximum(q_pos[:, None] - k_pos[None, :], 0)
    exact = REL_BUCKETS // 2
    large = exact + (jnp.log(jnp.maximum(n, 1).astype(jnp.float32) / exact)
                     / math.log(REL_MAX_DIST / exact) * (REL_BUCKETS - exact)).astype(jnp.int32)
    bucket = jnp.where(n < exact, n, jnp.minimum(large, REL_BUCKETS - 1))
    return jnp.moveaxis(table[bucket].astype(jnp.float32), -1, 0)


def diff_attn_block(q, k, v, q_pos, k_pos, table, lam):
    logits = jnp.einsum('bqhmd,bkhmd->bmhqk', q, k).astype(jnp.float32) * (DH_A ** -0.5)
    logits = logits + rel_pos_bias(q_pos, k_pos, table)
    causal = k_pos[None, :] <= q_pos[:, None]
    logits = jnp.where(causal, logits, jnp.finfo(jnp.float32).min)
    p = jax.nn.softmax(logits, axis=-1)
    wts = p[:, 0] - lam * p[:, 1]
    return jnp.einsum('bhqk,bkhd->bqhd', wts.astype(v.dtype), v)


def diff_attention(q, k, v, q_pos, k_pos, table, lam):
    b, lq = q.shape[:2]
    qb = Q_BLOCK if lq % Q_BLOCK == 0 else lq
    nb = lq // qb
    q_blocks = jnp.moveaxis(q.reshape(b, nb, qb, H_A, 2, DH_A), 1, 0)
    pos_blocks = q_pos.reshape(nb, qb)
    o = lax.map(lambda a: diff_attn_block(a[0], k, v, a[1], k_pos, table, lam), (q_blocks, pos_blocks))
    return jnp.moveaxis(o, 0, 1).reshape(b, lq, H_A, DV_A)


def ssd_chunked(x, dt, A, Bm, Cm, h0):
    b, L = x.shape[:2]
    q = CHUNK if L % CHUNK == 0 else L
    nc = L // q
    a_cum = jnp.cumsum((dt * A).reshape(b, nc, q, H_B), axis=2)
    xc = (x * dt[..., None]).reshape(b, nc, q, H_B, P_B)
    Bc = Bm.reshape(b, nc, q, H_B, N_B)
    Cc = Cm.reshape(b, nc, q, H_B, N_B)
    causal = jnp.tril(jnp.ones((q, q), bool))[None, None, :, :, None]
    seg = a_cum[:, :, :, None, :] - a_cum[:, :, None, :, :]
    decay = jnp.exp(jnp.where(causal, seg, -jnp.inf))
    scores = jnp.einsum('bclhn,bcshn->bclsh', Cc, Bc) * decay
    y_diag = jnp.einsum('bclsh,bcshp->bclhp', scores, xc)
    chunk_states = jnp.einsum('bclhn,bclh,bclhp->bchpn', Bc, jnp.exp(a_cum[:, :, -1:] - a_cum), xc)
    chunk_decay = jnp.exp(a_cum[:, :, -1])

    def step(h, inp):
        s, d = inp
        return h * d[:, :, None, None] + s, h

    h1, h_in = lax.scan(step, h0, (jnp.moveaxis(chunk_states, 1, 0), jnp.moveaxis(chunk_decay, 1, 0)))
    y_off = jnp.einsum('bclhn,cbhpn,bclh->bclhp', Cc, h_in, jnp.exp(a_cum))
    return (y_diag + y_off).reshape(b, L, H_B, P_B), h1


def gated_delta_chunked(q, k, v, g, beta, S0):
    b, L = q.shape[:2]
    cs = CHUNK if L % CHUNK == 0 else L
    nc = L // cs

    def blk(t):
        t = jnp.moveaxis(t, 2, 1)
        return t.reshape(t.shape[:2] + (nc, cs) + t.shape[3:])

    q, k, v, g, beta = [blk(t) for t in (q, k, v, g, beta)]
    g_cum = jnp.cumsum(g, axis=-1)
    incl = jnp.tril(jnp.ones((cs, cs), bool))
    strict = jnp.tril(jnp.ones((cs, cs), bool), -1)
    decay = jnp.exp(jnp.where(incl, g_cum[..., :, None] - g_cum[..., None, :], -jnp.inf))
    kk = jnp.einsum('bhcld,bhcsd->bhcls', k, k)
    lower = jnp.where(strict, beta[..., :, None] * kk * decay, 0.0)
    eye = jnp.eye(cs, dtype=q.dtype)
    rhs = jnp.concatenate([v * beta[..., None], k * (beta * jnp.exp(g_cum))[..., None]], axis=-1)
    sol = lax.linalg.triangular_solve(lower + eye, rhs, left_side=True, lower=True, unit_diagonal=True)
    u, w = sol[..., :DV_C], sol[..., DV_C:]
    attn = jnp.einsum('bhcld,bhcsd->bhcls', q, k) * decay
    qg = q * jnp.exp(g_cum)[..., None]
    kd = k * jnp.exp(g_cum[..., -1:] - g_cum)[..., None]
    g_last = jnp.exp(g_cum[..., -1])

    def step(S, inp):
        qg_c, kd_c, u_c, w_c, attn_c, gl_c = inp
        v_new = u_c - jnp.einsum('bhlk,bhkv->bhlv', w_c, S)
        o = jnp.einsum('bhlk,bhkv->bhlv', qg_c, S) + jnp.einsum('bhls,bhsv->bhlv', attn_c, v_new)
        S = S * gl_c[..., None, None] + jnp.einsum('bhlk,bhlv->bhkv', kd_c, v_new)
        return S, o

    seq_in = tuple(jnp.moveaxis(t, 2, 0) for t in (qg, kd, u, w, attn, g_last))
    S1, o = lax.scan(step, S0, seq_in)
    o = jnp.transpose(o, (1, 0, 3, 2, 4)).reshape(b, L, H_C, DV_C)
    return o, S1


def mamba2_branch(xbc, dt_raw, z, conv0, h0, conv_w, conv_b, dt_bias, A_log, D_skip, norm_g):
    b, L = xbc.shape[:2]
    f32 = jnp.float32
    xbc, conv1 = causal_conv(xbc, conv0, conv_w)
    xbc = jax.nn.silu(xbc + conv_b)
    xs, Bm, Cm = jnp.split(xbc, [D_B, D_B + G_B * N_B], axis=-1)
    xs = xs.reshape(b, L, H_B, P_B).astype(f32)
    Bm = jnp.repeat(Bm.reshape(b, L, G_B, N_B), H_B // G_B, axis=2).astype(f32)
    Cm = jnp.repeat(Cm.reshape(b, L, G_B, N_B), H_B // G_B, axis=2).astype(f32)
    dt = jax.nn.softplus(dt_raw.astype(f32) + dt_bias.astype(f32))
    A = -jnp.exp(A_log.astype(f32))
    y, h1 = ssd_chunked(xs, dt, A, Bm, Cm, h0.astype(f32))
    y = y + xs * D_skip.astype(f32)[:, None]
    y = y.reshape(b, L, D_B).astype(z.dtype) * jax.nn.silu(z)
    y = rms_norm(y.reshape(b, L, G_B, D_B // G_B)).reshape(b, L, D_B) * norm_g
    return y, conv1, h1


def gdn_branch(qkv, beta_raw, a_raw, z, conv0, S0, conv_w, dt_bias, A_log, norm_g):
    b, L = qkv.shape[:2]
    f32 = jnp.float32
    qkv, conv1 = causal_conv(qkv, conv0, conv_w)
    qkv = jax.nn.silu(qkv).astype(f32)
    q, k, v = jnp.split(qkv, [H_C * DK_C, 2 * H_C * DK_C], axis=-1)
    q = l2_norm(q.reshape(b, L, H_C, DK_C)) * (DK_C ** -0.5)
    k = l2_norm(k.reshape(b, L, H_C, DK_C))
    v = v.reshape(b, L, H_C, DV_C)
    beta = jax.nn.sigmoid(beta_raw.astype(f32))
    g = -jnp.exp(A_log.astype(f32)) * jax.nn.softplus(a_raw.astype(f32) + dt_bias.astype(f32))
    o, S1 = gated_delta_chunked(q, k, v, g, beta, S0.astype(f32))
    o = (rms_norm(o) * norm_g).reshape(b, L, D_C).astype(z.dtype) * jax.nn.silu(z)
    return o, conv1, S1


def mixer_layer(x, c, k_past, v_past, ssm_conv0, ssm_h0, gdn_conv0, gdn_S0, w, table, lam_init):
    b, L, _ = x.shape
    P = k_past.shape[1]
    mod = jnp.dot(jax.nn.silu(c), w['ada']) + w['ada_b']
    shift, scale, gate = jnp.split(mod, 3, axis=-1)
    h = rms_norm(x) * (1 + scale[:, None]) + shift[:, None]
    proj = jnp.dot(h, w['in'])
    idx = [int(s) for s in np.cumsum(IN_SPLITS)[:-1]]
    q_a, k_a, v_a, xbc, dt_raw, qkv_c, beta_raw, a_raw, z = jnp.split(proj, idx, axis=-1)
    z_a, z_b, z_c = jnp.split(z, [D_A, D_A + D_B], axis=-1)

    k_new = k_a.reshape(b, L, H_A, 2 * DH_A)
    v_new = v_a.reshape(b, L, H_A, DV_A)
    k_all = jnp.concatenate([k_past.astype(k_new.dtype), k_new], axis=1).reshape(b, P + L, H_A, 2, DH_A)
    v_all = jnp.concatenate([v_past.astype(v_new.dtype), v_new], axis=1)
    q_pos = P + jnp.arange(L)
    k_pos = jnp.arange(P + L)
    lv = w['lam'].astype(jnp.float32)
    lam = jnp.exp(jnp.sum(lv[0] * lv[1])) - jnp.exp(jnp.sum(lv[2] * lv[3])) + lam_init
    o_a = diff_attention(q_a.reshape(b, L, H_A, 2, DH_A), k_all, v_all, q_pos, k_pos, table, lam)
    o_a = (rms_norm(o_a) * w['subln'] * (1 - lam_init)).reshape(b, L, D_A) * jax.nn.silu(z_a)

    o_b, ssm_conv1, ssm_h1 = mamba2_branch(xbc, dt_raw, z_b, ssm_conv0, ssm_h0, w['ssm_conv_w'], w['ssm_conv_b'],
                                           w['ssm_dt_bias'], w['ssm_A_log'], w['ssm_D'], w['ssm_norm_g'])
    o_c, gdn_conv1, gdn_S1 = gdn_branch(qkv_c, beta_raw, a_raw, z_c, gdn_conv0, gdn_S0, w['gdn_conv_w'],
                                        w['gdn_dt_bias'], w['gdn_A_log'], w['gdn_norm_g'])

    out = jnp.dot(jnp.concatenate([o_a, o_b, o_c], axis=-1), w['out'])
    x = x + gate[:, None] * out
    return x, (k_new, v_new, ssm_h1, ssm_conv1, gdn_S1, gdn_conv1)


def setup_inputs(seed: int = 0) -> dict:
    key = jax.random.key(seed)
    ks = jax.random.split(key, 32)
    f32 = jnp.float32
    n_pages = PAST_LEN // PAGE_SIZE
    n_pool = (DEC_BATCH * n_pages * POOL_NUM) // POOL_DEN

    def nrm(k, shape, s):
        return jax.random.normal(k, shape, f32) * s

    def inv_softplus_dt(k, shape):
        dt = jnp.exp(jax.random.uniform(k, shape, f32, math.log(1e-3), math.log(1e-1)))
        return jnp.log(jnp.expm1(dt))

    page_table = jax.random.permutation(ks[10], n_pool)[:DEC_BATCH * n_pages]
    page_table = page_table.reshape(DEC_BATCH, n_pages).astype(jnp.int32)
    return {
        'x_prompt': nrm(ks[0], (BATCH, SEQ, D_MODEL), 1.0),
        'x_sample': nrm(ks[1], (DEC_BATCH, DEC_SEQ, D_MODEL), 1.0),
        'c_prompt': nrm(ks[2], (BATCH, D_MODEL), 1.0),
        'c_sample': nrm(ks[3], (DEC_BATCH, D_MODEL), 1.0),
        'cache_k': nrm(ks[4], (DEPTH, n_pool, PAGE_SIZE, H_A, 2 * DH_A), 1.0),
        'cache_v': nrm(ks[5], (DEPTH, n_pool, PAGE_SIZE, H_A, DV_A), 1.0),
        'state_ssm': nrm(ks[6], (DEPTH, DEC_BATCH, H_B, P_B, N_B), 0.1),
        'state_ssm_conv': nrm(ks[7], (DEPTH, DEC_BATCH, CONV_W - 1, CONV_B), 1.0),
        'state_gdn': nrm(ks[8], (DEPTH, DEC_BATCH, H_C, DK_C, DV_C), 0.1),
        'state_gdn_conv': nrm(ks[9], (DEPTH, DEC_BATCH, CONV_W - 1, CONV_C), 1.0),
        'page_table': page_table,
        'w_ada': nrm(ks[11], (DEPTH, D_MODEL, 3 * D_MODEL), 0.5 * D_MODEL ** -0.5),
        'b_ada': nrm(ks[12], (DEPTH, 3 * D_MODEL), 0.02),
        'w_in': nrm(ks[13], (DEPTH, D_MODEL, N_IN), D_MODEL ** -0.5),
        'w_out': nrm(ks[14], (DEPTH, D_MIX, D_MODEL), D_MIX ** -0.5),
        'rel_bias': nrm(ks[15], (REL_BUCKETS, H_A), 0.5),
        'attn_lambda': nrm(ks[16], (DEPTH, 4, DH_A), 0.1),
        'attn_subln_g': 1.0 + nrm(ks[17], (DEPTH, DV_A), 0.02),
        'ssm_conv_w': nrm(ks[18], (DEPTH, CONV_W, CONV_B), CONV_W ** -0.5),
        'ssm_conv_b': nrm(ks[19], (DEPTH, CONV_B), 0.02),
        'ssm_dt_bias': inv_softplus_dt(ks[20], (DEPTH, H_B)),
        'ssm_A_log': jnp.log(jax.random.uniform(ks[21], (DEPTH, H_B), f32, 1.0, 16.0)),
        'ssm_D': 1.0 + nrm(ks[22], (DEPTH, H_B), 0.1),
        'ssm_norm_g': 1.0 + nrm(ks[23], (DEPTH, D_B), 0.02),
        'gdn_conv_w': nrm(ks[24], (DEPTH, CONV_W, CONV_C), CONV_W ** -0.5),
        'gdn_dt_bias': inv_softplus_dt(ks[25], (DEPTH, H_C)),
        'gdn_A_log': jnp.log(jax.random.uniform(ks[26], (DEPTH, H_C), f32, 1.0, 16.0)),
        'gdn_norm_g': 1.0 + nrm(ks[27], (DEPTH, DV_C), 0.02),
        'final_norm_g': 1.0 + nrm(ks[28], (D_MODEL,), 0.02),
    }


def reference(x_prompt, x_sample, c_prompt, c_sample, cache_k, cache_v, state_ssm, state_ssm_conv,
              state_gdn, state_gdn_conv, page_table, w_ada, b_ada, w_in, w_out, rel_bias, attn_lambda,
              attn_subln_g, ssm_conv_w, ssm_conv_b, ssm_dt_bias, ssm_A_log, ssm_D, ssm_norm_g,
              gdn_conv_w, gdn_dt_bias, gdn_A_log, gdn_norm_g, final_norm_g):
    bp = x_prompt.shape[0]
    bs = x_sample.shape[0]
    past = page_table.shape[1] * PAGE_SIZE
    dtp = x_prompt.dtype
    xp, xs = x_prompt, x_sample
    new_p, new_s = [], []
    for l in range(DEPTH):
        w = {'ada': w_ada[l], 'ada_b': b_ada[l], 'in': w_in[l], 'out': w_out[l], 'lam': attn_lambda[l],
             'subln': attn_subln_g[l], 'ssm_conv_w': ssm_conv_w[l], 'ssm_conv_b': ssm_conv_b[l],
             'ssm_dt_bias': ssm_dt_bias[l], 'ssm_A_log': ssm_A_log[l], 'ssm_D': ssm_D[l],
             'ssm_norm_g': ssm_norm_g[l], 'gdn_conv_w': gdn_conv_w[l], 'gdn_dt_bias': gdn_dt_bias[l],
             'gdn_A_log': gdn_A_log[l], 'gdn_norm_g': gdn_norm_g[l]}
        lam_init = 0.8 - 0.6 * math.exp(-0.3 * l)
        xp, st_p = mixer_layer(xp, c_prompt,
                               jnp.zeros((bp, 0, H_A, 2 * DH_A), dtp), jnp.zeros((bp, 0, H_A, DV_A), dtp),
                               jnp.zeros((bp, CONV_W - 1, CONV_B), dtp), jnp.zeros((bp, H_B, P_B, N_B), jnp.float32),
                               jnp.zeros((bp, CONV_W - 1, CONV_C), dtp), jnp.zeros((bp, H_C, DK_C, DV_C), jnp.float32),
                               w, rel_bias, lam_init)
        new_p.append(st_p)
        k_past = cache_k[l, page_table].reshape(bs, past, H_A, 2 * DH_A)
        v_past = cache_v[l, page_table].reshape(bs, past, H_A, DV_A)
        xs, st_s = mixer_layer(xs, c_sample, k_past, v_past, state_ssm_conv[l], state_ssm[l],
                               state_gdn_conv[l], state_gdn[l], w, rel_bias, lam_init)
        new_s.append(st_s)
    y_prompt = rms_norm(xp) * final_norm_g
    y_sample = rms_norm(xs) * final_norm_g
    k_p, v_p, ssm_p, ssm_conv_p, gdn_p, gdn_conv_p = [jnp.stack(a) for a in zip(*new_p)]
    k_s, v_s, ssm_s, ssm_conv_s, gdn_s, gdn_conv_s = [jnp.stack(a) for a in zip(*new_s)]
    return (y_prompt, y_sample, k_p, v_p, ssm_p, ssm_conv_p, gdn_p, gdn_conv_p,
            k_s, v_s, ssm_s, ssm_conv_s, gdn_s, gdn_conv_s)
```

```python
import functools
import math

import jax
import jax.numpy as jnp
from jax import lax
from jax.experimental import pallas as pl
from jax.experimental.pallas import tpu as pltpu

F32 = jnp.float32
BF16 = jnp.bfloat16
HIGHEST = lax.Precision.HIGHEST

D_MODEL = 1024
H_A, DH_A, DV_A = 4, 64, 128
D_A = H_A * DV_A
H_B, P_B, G_B, N_B = 8, 64, 2, 128
D_B = H_B * P_B
CONV_B = D_B + 2 * G_B * N_B
H_C, DK_C, DV_C = 4, 128, 128
D_C = H_C * DV_C
CONV_C = 2 * H_C * DK_C + D_C
CONV_W = 4
D_MIX = D_A + D_B + D_C
REL_BUCKETS, REL_MAX_DIST = 32, 128
PAGE_SIZE = 128
EPS = 1e-6
GDN_CHUNK = 64

LANES = 128
SUBLANES = 8
SMALL_W = LANES
DT_LANE, BETA_LANE, AG_LANE = 0, H_B, H_B + H_C
NEG_BIG = -1e30
VMEM_LIMIT = 56 * 1024 * 1024

SEGMENTS = (("q", 2 * H_A * DH_A), ("k", 2 * H_A * DH_A), ("v", D_A), ("xbc", CONV_B),
            ("qkv", CONV_C), ("z", D_MIX), ("small", SMALL_W))


def _cparams(*sem):
    return pltpu.CompilerParams(dimension_semantics=sem, vmem_limit_bytes=VMEM_LIMIT)


def _silu(x):
    return x * jax.nn.sigmoid(x)


def _softplus(x):
    return jnp.maximum(x, 0.0) + jnp.log1p(jnp.exp(-jnp.abs(x)))


def _rms(x):
    return x * lax.rsqrt(jnp.mean(x * x, axis=-1, keepdims=True) + EPS)


def _dot_nt(a, b):
    return lax.dot_general(a, b, (((1,), (1,)), ((), ())), preferred_element_type=F32)


def _dot_tn(a, b):
    return lax.dot_general(a, b, (((0,), (0,)), ((), ())), preferred_element_type=F32)


def _dot_tn_hi(a, b):
    return lax.dot_general(a, b, (((0,), (0,)), ((), ())), preferred_element_type=F32, precision=HIGHEST)


def _dot(a, b):
    return jnp.dot(a, b, preferred_element_type=F32)


def _dot_hi(a, b):
    return jnp.dot(a, b, preferred_element_type=F32, precision=HIGHEST)


def _iota(shape, dim):
    return lax.broadcasted_iota(jnp.int32, shape, dim)


def _mod_kernel(c_ref, w_ref, b_ref, o_ref):
    sc = _silu(c_ref[...]).astype(BF16)
    o_ref[...] = _dot(sc, w_ref[...].astype(BF16)) + b_ref[...]


def _modulation(c_all, w_ada, b_ada):
    depth = w_ada.shape[0]
    n = c_all.shape[0]
    return pl.pallas_call(
        _mod_kernel,
        grid=(depth, 3),
        in_specs=[pl.BlockSpec((n, D_MODEL), lambda l, j: (0, 0)),
                  pl.BlockSpec((None, D_MODEL, D_MODEL), lambda l, j: (l, 0, j)),
                  pl.BlockSpec((None, 1, D_MODEL), lambda l, j: (l, 0, j))],
        out_specs=pl.BlockSpec((None, n, D_MODEL), lambda l, j: (l, 0, j)),
        out_shape=jax.ShapeDtypeStruct((depth, n, 3 * D_MODEL), F32),
        compiler_params=_cparams("arbitrary", "arbitrary"),
    )(c_all, w_ada, b_ada.reshape(depth, 1, 3 * D_MODEL))


def _inproj_kernel(x_ref, scale_ref, shift_ref, w_ref, *out_refs):
    h = _rms(x_ref[...]) * (1.0 + scale_ref[...]) + shift_ref[...]
    hb = h.astype(BF16)
    off = 0
    for ref in out_refs:
        n = ref.shape[-1]
        ref[...] = _dot(hb, w_ref[:, off:off + n])
        off += n


def _in_projection(x, scale, shift, w_perm, tm):
    b, L, _ = x.shape
    ms = scale.shape[1]
    mod_rows = tm if ms == L else 1
    mod_map = (lambda i, j: (i, j, 0)) if ms == L else (lambda i, j: (i, 0, 0))
    n_tot = w_perm.shape[1]
    return pl.pallas_call(
        _inproj_kernel,
        grid=(b, L // tm),
        in_specs=[pl.BlockSpec((None, tm, D_MODEL), lambda i, j: (i, j, 0)),
                  pl.BlockSpec((None, mod_rows, D_MODEL), mod_map),
                  pl.BlockSpec((None, mod_rows, D_MODEL), mod_map),
                  pl.BlockSpec((D_MODEL, n_tot), lambda i, j: (0, 0))],
        out_specs=[pl.BlockSpec((None, tm, w), lambda i, j: (i, j, 0)) for _, w in SEGMENTS],
        out_shape=[jax.ShapeDtypeStruct((b, L, w), F32) for _, w in SEGMENTS],
        compiler_params=_cparams("arbitrary", "arbitrary"),
    )(x, scale, shift, w_perm)


def _outproj_kernel(x_ref, gate_ref, oa_ref, ob_ref, oc_ref, w_ref, *rest, final):
    acc = _dot(oa_ref[...], w_ref[0:D_A, :])
    acc += _dot(ob_ref[...], w_ref[D_A:D_A + D_B, :])
    acc += _dot(oc_ref[...], w_ref[D_A + D_B:D_MIX, :])
    y = x_ref[...] + gate_ref[...] * acc
    if final:
        g_ref, o_ref = rest
        o_ref[...] = _rms(y) * g_ref[...]
    else:
        (o_ref,) = rest
        o_ref[...] = y


def _out_projection(x, gate, oa, ob, oc, w_out, final_g, tm):
    b, L, _ = x.shape
    ms = gate.shape[1]
    mod_rows = tm if ms == L else 1
    mod_map = (lambda i, j: (i, j, 0)) if ms == L else (lambda i, j: (i, 0, 0))
    row = lambda w: pl.BlockSpec((None, tm, w), lambda i, j: (i, j, 0))
    in_specs = [row(D_MODEL), pl.BlockSpec((None, mod_rows, D_MODEL), mod_map),
                row(D_A), row(D_B), row(D_C),
                pl.BlockSpec((D_MIX, D_MODEL), lambda i, j: (0, 0))]
    args = [x, gate, oa, ob, oc, w_out]
    final = final_g is not None
    if final:
        in_specs.append(pl.BlockSpec((1, D_MODEL), lambda i, j: (0, 0)))
        args.append(final_g.reshape(1, D_MODEL))
    return pl.pallas_call(
        functools.partial(_outproj_kernel, final=final),
        grid=(b, L // tm),
        in_specs=in_specs,
        out_specs=row(D_MODEL),
        out_shape=jax.ShapeDtypeStruct((b, L, D_MODEL), F32),
        compiler_params=_cparams("arbitrary", "arbitrary"),
    )(*args)


def _online_update(s, vb, m, l, acc):
    m_new = jnp.maximum(m, jnp.max(s, axis=-1, keepdims=True))
    alpha = jnp.exp(m - m_new)
    p = jnp.exp(s - m_new)
    l = alpha * l + jnp.sum(p, axis=-1, keepdims=True)
    acc = alpha * acc + _dot(p.astype(BF16), vb)
    return m_new, l, acc


def _lambda_value(lp, lam_init):
    s01 = jnp.sum(lp[0:1, :] * lp[1:2, :], axis=-1, keepdims=True)
    s23 = jnp.sum(lp[2:3, :] * lp[3:4, :], axis=-1, keepdims=True)
    return jnp.exp(s01) - jnp.exp(s23) + lam_init


def _rel_bias_values(table, n):
    exact = REL_BUCKETS // 2
    large = exact + (jnp.log(jnp.maximum(n, 1).astype(F32) / exact)
                     / math.log(REL_MAX_DIST / exact) * (REL_BUCKETS - exact)).astype(jnp.int32)
    bucket = jnp.where(n < exact, n, jnp.minimum(large, REL_BUCKETS - 1))
    return table[bucket].astype(F32)


def _attn_kernel(q_ref, k_ref, v_ref, z_ref, bias_ref, lamp_ref, subln_ref, o_ref, kb_scr, vb_scr,
                 *, T, lam_init):
    qi = pl.program_id(2)

    @pl.when(qi == 0)
    def _():
        kb_scr[...] = k_ref[...].astype(BF16)
        vb_scr[...] = v_ref[...].astype(BF16)

    q = q_ref[...] * (DH_A ** -0.5)
    lane = _iota(q.shape, 1)
    qs = (jnp.where(lane < DH_A, q, 0.0).astype(BF16), jnp.where(lane >= DH_A, q, 0.0).astype(BF16))

    def init():
        return (jnp.full((T, 1), NEG_BIG, F32), jnp.zeros((T, 1), F32), jnp.zeros((T, DV_A), F32))

    def far_body(j, carry):
        start = pl.multiple_of(j * T, T)
        kt = kb_scr[pl.ds(start, T), :]
        vt = vb_scr[pl.ds(start, T), :]
        return tuple(_online_update(_dot_nt(qs[i], kt), vt, *carry[i]) for i in range(2))

    carry = lax.fori_loop(0, jnp.maximum(qi - 1, 0), far_body, (init(), init()))

    start = pl.multiple_of(jnp.maximum(qi - 1, 0) * T, T)
    kt = kb_scr[pl.ds(start, 2 * T), :]
    vt = vb_scr[pl.ds(start, 2 * T), :]
    bias = bias_ref[...]
    outs = []
    for i in range(2):
        m, l, acc = _online_update(_dot_nt(qs[i], kt) + bias, vt, *carry[i])
        outs.append(acc / l)
    lam = _lambda_value(lamp_ref[...], lam_init)
    o = outs[0] - lam * outs[1]
    o = _rms(o) * subln_ref[...] * (1.0 - lam_init) * _silu(z_ref[...])
    o_ref[...] = o.astype(o_ref.dtype)


def _prompt_attention(q, k, v, z, bias_near, lam_p, subln, lam_init, T):
    b, L, _ = q.shape
    nq = L // T
    return pl.pallas_call(
        functools.partial(_attn_kernel, T=T, lam_init=lam_init),
        grid=(b, H_A, nq),
        in_specs=[pl.BlockSpec((None, T, DV_A), lambda i, h, j: (i, j, h)),
                  pl.BlockSpec((None, L, DV_A), lambda i, h, j: (i, 0, h)),
                  pl.BlockSpec((None, L, DV_A), lambda i, h, j: (i, 0, h)),
                  pl.BlockSpec((None, T, DV_A), lambda i, h, j: (i, j, h)),
                  pl.BlockSpec((None, None, T, 2 * T), lambda i, h, j: (h, jnp.minimum(j, 1), 0, 0)),
                  pl.BlockSpec((4, DH_A), lambda i, h, j: (0, 0)),
                  pl.BlockSpec((1, DV_A), lambda i, h, j: (0, 0))],
        out_specs=pl.BlockSpec((None, T, DV_A), lambda i, h, j: (i, j, h)),
        out_shape=jax.ShapeDtypeStruct((b, L, D_A), BF16),
        scratch_shapes=[pltpu.VMEM((L, DV_A), BF16), pltpu.VMEM((L, DV_A), BF16)],
        compiler_params=_cparams("arbitrary", "arbitrary", "arbitrary"),
    )(q, k, v, z, bias_near, lam_p, subln.reshape(1, DV_A))


def _near_bias_tiles(table, T):
    i = jnp.arange(T)[:, None]
    j = jnp.arange(2 * T)[None, :]
    tiles = []
    far = table[REL_BUCKETS - 1].astype(F32)
    for off in (0, T):
        n = off + i - j
        vals = _rel_bias_values(table, jnp.maximum(n, 0)) - far
        vals = jnp.where((n >= 0)[..., None], vals, NEG_BIG)
        tiles.append(jnp.moveaxis(vals, -1, 0))
    return jnp.stack(tiles, axis=1)


def _decode_kernel(pt_ref, qpat_ref, knew_ref, vnew_ref, bias_ref, biasnew_ref, z_ref, lamp_ref,
                   subln_ref, *rest, G, lam_init):
    del pt_ref
    k_refs, v_refs = rest[:G], rest[G:2 * G]
    o_ref, m_scr, l_scr, acc_scr = rest[2 * G:]
    j = pl.program_id(1)

    @pl.when(j == 0)
    def _():
        m_scr[...] = jnp.full(m_scr.shape, NEG_BIG, F32)
        l_scr[...] = jnp.zeros(l_scr.shape, F32)
        acc_scr[...] = jnp.zeros(acc_scr.shape, F32)

    qf = qpat_ref[...]
    qb = qf.astype(BF16)
    state = (m_scr[...], l_scr[...], acc_scr[...])
    for g in range(G):
        s = _dot_nt(qb, k_refs[g][...].astype(BF16)) + bias_ref[g]
        state = _online_update(s, v_refs[g][...].astype(BF16), *state)
    m, l, acc = state
    m_scr[...] = m
    l_scr[...] = l
    acc_scr[...] = acc

    @pl.when(j == pl.num_programs(1) - 1)
    def _():
        s_new = jnp.sum(qf * knew_ref[...], axis=-1, keepdims=True) + biasnew_ref[:, 0:1]
        m_new = jnp.maximum(m, s_new)
        alpha = jnp.exp(m - m_new)
        p_new = jnp.exp(s_new - m_new)
        l_fin = alpha * l + p_new
        out = (alpha * acc + p_new * vnew_ref[...]) / l_fin
        lam = _lambda_value(lamp_ref[...], lam_init)
        o = out[0:H_A, :] - lam * out[H_A:2 * H_A, :]
        o_ref[...] = _rms(o) * subln_ref[...] * (1.0 - lam_init) * _silu(z_ref[...])


def _decode_attention(layer, page_table, cache_k, cache_v, qpat, knew, vnew, bias_pages, bias_new, z3,
                      lam_p, subln, lam_init, G):
    bs, n_pages = page_table.shape
    rows = PAGE_SIZE * H_A

    def page_spec(g):
        return pl.BlockSpec((None, None, rows, DV_A),
                            lambda i, j, pt: (layer, pt[i, j * G + g], 0, 0))

    per_b = lambda r: pl.BlockSpec((None, r, DV_A), lambda i, j, pt: (i, 0, 0))
    grid_spec = pltpu.PrefetchScalarGridSpec(
        num_scalar_prefetch=1,
        grid=(bs, n_pages // G),
        in_specs=[per_b(2 * H_A), per_b(2 * H_A), per_b(2 * H_A),
                  pl.BlockSpec((G, 2 * H_A, rows), lambda i, j, pt: (j, 0, 0)),
                  pl.BlockSpec((2 * H_A, DV_A), lambda i, j, pt: (0, 0)),
                  per_b(H_A),
                  pl.BlockSpec((4, DH_A), lambda i, j, pt: (0, 0)),
                  pl.BlockSpec((1, DV_A), lambda i, j, pt: (0, 0))]
                 + [page_spec(g) for g in range(G)] + [page_spec(g) for g in range(G)],
        out_specs=per_b(H_A),
        scratch_shapes=[pltpu.VMEM((2 * H_A, 1), F32), pltpu.VMEM((2 * H_A, 1), F32),
                        pltpu.VMEM((2 * H_A, DV_A), F32)],
    )
    return pl.pallas_call(
        functools.partial(_decode_kernel, G=G, lam_init=lam_init),
        grid_spec=grid_spec,
        out_shape=jax.ShapeDtypeStruct((bs, H_A, DV_A), F32),
        compiler_params=_cparams("arbitrary", "arbitrary"),
    )(page_table, qpat, knew, vnew, bias_pages, bias_new, z3, lam_p, subln.reshape(1, DV_A),
      *([cache_k] * G), *([cache_v] * G))


def _decode_bias(table, past):
    n = past - jnp.arange(past)
    vals = _rel_bias_values(table, n).reshape(past // PAGE_SIZE, PAGE_SIZE, H_A)
    vals = jnp.moveaxis(vals, -1, 1)[..., None]
    same = (jnp.arange(H_A)[:, None, None] == jnp.arange(H_A)[None, None, :])
    full = jnp.where(same[None], vals, NEG_BIG).reshape(past // PAGE_SIZE, H_A, PAGE_SIZE * H_A)
    return jnp.concatenate([full, full], axis=1)


def _conv_taps(xp_ref, x, w_ref, rows):
    xp_ref[SUBLANES:SUBLANES + rows, :] = x
    y = w_ref[CONV_W - 1:CONV_W, :] * x
    for i in range(CONV_W - 1):
        y = y + w_ref[i:i + 1, :] * xp_ref[pl.ds(SUBLANES - (CONV_W - 1) + i, rows), :]
    xp_ref[0:SUBLANES, :] = x[rows - SUBLANES:rows, :]
    return y


def _ssd_kernel(xbc_ref, sm_ref, z_ref, cw_ref, cb_ref, dtb_s_ref, alog_s_ref, dtb_e_ref, alog_e_ref,
                dskip_ref, ng_ref, exp_ref, y_ref, hout_ref, xp_scr, h_scr, *, Q):
    c = pl.program_id(1)

    @pl.when(c == 0)
    def _():
        xp_scr[0:SUBLANES, :] = jnp.zeros((SUBLANES, CONV_B), F32)
        h_scr[...] = jnp.zeros(h_scr.shape, F32)

    xa = _silu(_conv_taps(xp_scr, xbc_ref[...], cw_ref, Q) + cb_ref[...])
    xs = xa[:, 0:D_B]
    sm = sm_ref[...]
    row = _iota((Q, Q), 0)
    col = _iota((Q, Q), 1)
    tril = row >= col
    tril_f = tril.astype(F32)

    a_s = _softplus(sm + dtb_s_ref[...]) * (-jnp.exp(alog_s_ref[...]))
    cum_s = _dot_hi(tril_f, a_s)
    cum_t = cum_s.T
    dt_e = _softplus(_dot_hi(sm, exp_ref[...]) + dtb_e_ref[...])
    cum_e = _dot_hi(cum_s, exp_ref[...])
    xc = xs * dt_e
    dec_end = jnp.exp(cum_e[Q - 1:Q, :] - cum_e)
    dec_in = jnp.exp(cum_e)
    xc_b = xc.astype(BF16)
    xe_b = (xc * dec_end).astype(BF16)
    lane = _iota((Q, LANES), 1)
    prow = _iota((LANES, LANES), 0)
    heads_per_group = H_B // G_B
    pairs = []
    for g in range(G_B):
        bm = xa[:, D_B + g * N_B:D_B + (g + 1) * N_B].astype(BF16)
        cm = xa[:, D_B + G_B * N_B + g * N_B:D_B + G_B * N_B + (g + 1) * N_B].astype(BF16)
        cb = _dot_nt(cm, bm)
        for pr in range(g * heads_per_group // 2, (g + 1) * heads_per_group // 2):
            lo, hi = pr * LANES, (pr + 1) * LANES
            yd = []
            for hh in (2 * pr, 2 * pr + 1):
                seg = cum_s[:, hh:hh + 1] - cum_t[hh:hh + 1, :]
                mat = cb * jnp.exp(jnp.where(tril, seg, -jnp.inf))
                yd.append(_dot(mat.astype(BF16), xc_b[:, lo:hi]))
            y_diag = jnp.where(lane < P_B, yd[0], yd[1])
            hp = h_scr[lo:hi, :]
            y_off = _dot_nt(cm, hp.astype(BF16)) * dec_in[:, lo:hi]
            st = _dot_tn(xe_b[:, lo:hi], bm)
            a_last = jnp.where(prow < P_B, cum_t[2 * pr:2 * pr + 1, Q - 1:Q],
                               cum_t[2 * pr + 1:2 * pr + 2, Q - 1:Q])
            h_scr[lo:hi, :] = hp * jnp.exp(a_last) + st
            pairs.append(y_diag + y_off + xs[:, lo:hi] * dskip_ref[:, lo:hi])
    y = jnp.concatenate(pairs, axis=-1) * _silu(z_ref[...])
    gw = D_B // G_B
    y = jnp.concatenate([_rms(y[:, g * gw:(g + 1) * gw]) for g in range(G_B)], axis=-1) * ng_ref[...]
    y_ref[...] = y.astype(y_ref.dtype)

    @pl.when(c == pl.num_programs(1) - 1)
    def _():
        hout_ref[...] = h_scr[...]


def _prompt_ssd(xbc, sm, z, p, Q):
    b, L, _ = xbc.shape
    full = lambda a: pl.BlockSpec(a.shape, lambda i, j: (0,) * a.ndim)
    params = [p["ssm_conv_w"], p["ssm_conv_b"], p["dtb_small"], p["alog_small"], p["ssm_dtb_e"],
              p["ssm_alog_e"], p["ssm_d_e"], p["ssm_norm_g"], p["expand"]]
    return pl.pallas_call(
        functools.partial(_ssd_kernel, Q=Q),
        grid=(b, L // Q),
        in_specs=[pl.BlockSpec((None, Q, CONV_B), lambda i, j: (i, j, 0)),
                  pl.BlockSpec((None, Q, SMALL_W), lambda i, j: (i, j, 0)),
                  pl.BlockSpec((None, Q, D_B), lambda i, j: (i, j, D_A // D_B))]
                 + [full(a) for a in params],
        out_specs=[pl.BlockSpec((None, Q, D_B), lambda i, j: (i, j, 0)),
                   pl.BlockSpec((None, D_B, N_B), lambda i, j: (i, 0, 0))],
        out_shape=[jax.ShapeDtypeStruct((b, L, D_B), BF16),
                   jax.ShapeDtypeStruct((b, D_B, N_B), F32)],
        scratch_shapes=[pltpu.VMEM((Q + SUBLANES, CONV_B), F32), pltpu.VMEM((D_B, N_B), F32)],
        compiler_params=_cparams("arbitrary", "arbitrary"),
    )(xbc, sm, z, *params)


def _unit_lower_solve(lower, rhs):
    n = lower.shape[0]
    p = -lower
    x = rhs + _dot_hi(p, rhs)
    k = 2
    while k < n:
        p = _dot_hi(p, p)
        x = x + _dot_hi(p, x)
        k *= 2
    return x


def _gdn_chunk(qkv, sm, z, dtb_s, alog_s, ng, s_scr):
    cs = GDN_CHUNK
    row = _iota((cs, cs), 0)
    col = _iota((cs, cs), 1)
    incl = row >= col
    strict = row > col
    g_s = _softplus(sm + dtb_s) * (-jnp.exp(alog_s))
    gcum_s = _dot_hi(incl.astype(F32), g_s)
    gcum_t = gcum_s.T
    beta_s = jax.nn.sigmoid(sm)
    outs = []
    for h in range(H_C):
        q = qkv[:, h * DK_C:(h + 1) * DK_C]
        k = qkv[:, D_C + h * DK_C:D_C + (h + 1) * DK_C]
        v = qkv[:, 2 * D_C + h * DV_C:2 * D_C + (h + 1) * DV_C]
        q = q * lax.rsqrt(jnp.sum(q * q, axis=-1, keepdims=True) + EPS) * (DK_C ** -0.5)
        k = k * lax.rsqrt(jnp.sum(k * k, axis=-1, keepdims=True) + EPS)
        beta = beta_s[:, BETA_LANE + h:BETA_LANE + h + 1]
        gc = gcum_s[:, AG_LANE + h:AG_LANE + h + 1]
        gr = gcum_t[AG_LANE + h:AG_LANE + h + 1, :]
        g_last = gcum_s[cs - 1:cs, AG_LANE + h:AG_LANE + h + 1]
        decay = jnp.exp(jnp.where(incl, gc - gr, -jnp.inf))
        kb = k.astype(BF16)
        lower = jnp.where(strict, beta * _dot_nt(kb, kb) * decay, 0.0)
        rhs = jnp.concatenate([v * beta, k * (beta * jnp.exp(gc))], axis=-1)
        sol = _unit_lower_solve(lower, rhs)
        u, w = sol[:, 0:DV_C], sol[:, DV_C:2 * DV_C]
        attn = _dot_nt(q.astype(BF16), kb) * decay
        s_old = s_scr[h * DK_C:(h + 1) * DK_C, :]
        s_b = s_old.astype(BF16)
        v_new = u - _dot(w.astype(BF16), s_b)
        v_new_b = v_new.astype(BF16)
        o = _dot((q * jnp.exp(gc)).astype(BF16), s_b) + _dot(attn.astype(BF16), v_new_b)
        kd = (k * jnp.exp(g_last - gc)).astype(BF16)
        s_scr[h * DK_C:(h + 1) * DK_C, :] = s_old * jnp.exp(g_last) + _dot_tn(kd, v_new_b)
        outs.append(_rms(o) * ng)
    return jnp.concatenate(outs, axis=-1) * _silu(z)


def _gdn_kernel(qkv_ref, sm_ref, z_ref, cw_ref, dtb_s_ref, alog_s_ref, ng_ref, o_ref, sout_ref,
                xp_scr, s_scr, act_scr, *, R):
    c = pl.program_id(1)

    @pl.when(c == 0)
    def _():
        xp_scr[0:SUBLANES, :] = jnp.zeros((SUBLANES, CONV_C), F32)
        s_scr[...] = jnp.zeros(s_scr.shape, F32)

    act_scr[...] = _silu(_conv_taps(xp_scr, qkv_ref[...], cw_ref, R))

    def body(i, carry):
        r0 = pl.multiple_of(i * GDN_CHUNK, GDN_CHUNK)
        rows = pl.ds(r0, GDN_CHUNK)
        o = _gdn_chunk(act_scr[rows, :], sm_ref[rows, :], z_ref[rows, :], dtb_s_ref[...],
                       alog_s_ref[...], ng_ref[...], s_scr)
        o_ref[rows, :] = o.astype(o_ref.dtype)
        return carry

    lax.fori_loop(0, R // GDN_CHUNK, body, 0)

    @pl.when(c == pl.num_programs(1) - 1)
    def _():
        sout_ref[...] = s_scr[...]


def _prompt_gdn(qkv, sm, z, p, R):
    b, L, _ = qkv.shape
    full = lambda a: pl.BlockSpec(a.shape, lambda i, j: (0,) * a.ndim)
    params = [p["gdn_conv_w"], p["dtb_small"], p["alog_small"], p["gdn_norm_g"]]
    return pl.pallas_call(
        functools.partial(_gdn_kernel, R=R),
        grid=(b, L // R),
        in_specs=[pl.BlockSpec((None, R, CONV_C), lambda i, j: (i, j, 0)),
                  pl.BlockSpec((None, R, SMALL_W), lambda i, j: (i, j, 0)),
                  pl.BlockSpec((None, R, D_C), lambda i, j: (i, j, (D_A + D_B) // D_C))]
                 + [full(a) for a in params],
        out_specs=[pl.BlockSpec((None, R, D_C), lambda i, j: (i, j, 0)),
                   pl.BlockSpec((None, H_C * DK_C, DV_C), lambda i, j: (i, 0, 0))],
        out_shape=[jax.ShapeDtypeStruct((b, L, D_C), BF16),
                   jax.ShapeDtypeStruct((b, H_C * DK_C, DV_C), F32)],
        scratch_shapes=[pltpu.VMEM((R + SUBLANES, CONV_C), F32),
                        pltpu.VMEM((H_C * DK_C, DV_C), F32),
                        pltpu.VMEM((R, CONV_C), F32)],
        compiler_params=_cparams("arbitrary", "arbitrary"),
    )(qkv, sm, z, *params)


def _expand_heads(vals, n_heads, width):
    lane = _iota((1, n_heads * width), 1)
    out = jnp.zeros((1, n_heads * width), F32)
    for h in range(n_heads):
        out = jnp.where((lane >= h * width) & (lane < (h + 1) * width), vals[:, h:h + 1], out)
    return out


def _pad_rows(rows):
    n = rows[0].shape[-1]
    r = _iota((SUBLANES, n), 0)
    out = jnp.zeros((SUBLANES, n), F32)
    for i, v in enumerate(rows):
        out = jnp.where(r == i, v, out)
    return out


def _sample_step_kernel(xbc_ref, qkv_ref, sm_ref, z_ref, cst_b_ref, cst_c_ref, h0_ref, s0_ref,
                        cwb_ref, cbb_ref, cwc_ref, dtb_s_ref, alog_s_ref, dskip_ref, ngb_ref, ngc_ref,
                        ob_ref, oc_ref, cnew_b_ref, cnew_c_ref, h1_ref, s1_ref):
    def conv(x, st_ref, w_ref, new_ref):
        y = w_ref[CONV_W - 1:CONV_W, :] * x
        for i in range(CONV_W - 1):
            y = y + w_ref[i:i + 1, :] * st_ref[i:i + 1, :]
        for i in range(CONV_W - 2):
            new_ref[i:i + 1, :] = st_ref[i + 1:i + 2, :]
        new_ref[CONV_W - 2:CONV_W - 1, :] = x
        return y

    sm = sm_ref[...]
    dt_or_sp = _softplus(sm + dtb_s_ref[...])
    a_s = dt_or_sp * (-jnp.exp(alog_s_ref[...]))
    z = z_ref[...]

    xa = _silu(conv(xbc_ref[...], cst_b_ref, cwb_ref, cnew_b_ref) + cbb_ref[...])
    xs = xa[:, 0:D_B]
    dt_e = _expand_heads(dt_or_sp[:, DT_LANE:DT_LANE + H_B], H_B, P_B)
    a_e = _expand_heads(a_s[:, DT_LANE:DT_LANE + H_B], H_B, P_B)
    xc = xs * dt_e
    half = D_B // G_B
    lane_b = _iota((1, D_B), 1)
    bm = [xa[:, D_B + g * N_B:D_B + (g + 1) * N_B] for g in range(G_B)]
    cm = [xa[:, D_B + G_B * N_B + g * N_B:D_B + G_B * N_B + (g + 1) * N_B] for g in range(G_B)]
    h0 = h0_ref[...]
    coff = _dot_nt(_pad_rows(cm).astype(BF16), h0.astype(BF16))
    y_off = jnp.where(lane_b < half, coff[0:1, :], coff[1:2, :]) * jnp.exp(a_e)
    cb = [jnp.sum(cm[g] * bm[g], axis=-1, keepdims=True) for g in range(G_B)]
    y_diag = jnp.where(lane_b < half, cb[0], cb[1]) * xc
    y = (y_diag + y_off + xs * dskip_ref[...]) * _silu(z[:, D_A:D_A + D_B])
    y = jnp.concatenate([_rms(y[:, g * half:(g + 1) * half]) for g in range(G_B)], axis=-1)
    ob_ref[...] = y * ngb_ref[...]
    xc_rows = _pad_rows([jnp.where(lane_b < half, xc, 0.0), jnp.where(lane_b >= half, xc, 0.0)])
    outer = _dot_tn(xc_rows.astype(BF16), _pad_rows(bm).astype(BF16))
    dec_col = _dot_tn_hi(_pad_rows([jnp.exp(a_e)]), jnp.ones((SUBLANES, N_B), F32))
    h1_ref[...] = h0 * dec_col + outer

    qkv = _silu(conv(qkv_ref[...], cst_c_ref, cwc_ref, cnew_c_ref))
    beta_s = jax.nn.sigmoid(sm)
    outs = []
    for h in range(H_C):
        q = qkv[:, h * DK_C:(h + 1) * DK_C]
        k = qkv[:, D_C + h * DK_C:D_C + (h + 1) * DK_C]
        v = qkv[:, 2 * D_C + h * DV_C:2 * D_C + (h + 1) * DV_C]
        q = q * lax.rsqrt(jnp.sum(q * q, axis=-1, keepdims=True) + EPS) * (DK_C ** -0.5)
        k = k * lax.rsqrt(jnp.sum(k * k, axis=-1, keepdims=True) + EPS)
        beta = beta_s[:, BETA_LANE + h:BETA_LANE + h + 1]
        eg = jnp.exp(a_s[:, AG_LANE + h:AG_LANE + h + 1])
        s_old = s0_ref[h * DK_C:(h + 1) * DK_C, :]
        ws = _dot(_pad_rows([k * (beta * eg), q * eg]).astype(BF16), s_old.astype(BF16))
        v_new = v * beta - ws[0:1, :]
        o = ws[1:2, :] + jnp.sum(q * k, axis=-1, keepdims=True) * v_new
        upd = _dot_tn(_pad_rows([k]).astype(BF16), _pad_rows([v_new]).astype(BF16))
        eg_col = _dot_tn_hi(_pad_rows([jnp.broadcast_to(eg, (1, DK_C))]), jnp.ones((SUBLANES, DV_C), F32))
        s1_ref[h * DK_C:(h + 1) * DK_C, :] = s_old * eg_col + upd
        outs.append(_rms(o) * ngc_ref[...])
    oc_ref[...] = jnp.concatenate(outs, axis=-1) * _silu(z[:, D_A + D_B:D_MIX])


def _sample_step(xbc, qkv, sm, z, cst_b, cst_c, h0, s0, p):
    bs = xbc.shape[0]
    per_b = lambda a: pl.BlockSpec((None,) + tuple(a.shape[1:]), lambda i: (i,) + (0,) * (len(a.shape) - 1))
    full = lambda a: pl.BlockSpec(a.shape, lambda i: (0,) * a.ndim)
    data = [xbc, qkv, sm, z, cst_b, cst_c, h0, s0]
    params = [p["ssm_conv_w"], p["ssm_conv_b"], p["gdn_conv_w"], p["dtb_small"], p["alog_small"],
              p["ssm_d_e"], p["ssm_norm_g"], p["gdn_norm_g"]]
    outs = [jax.ShapeDtypeStruct((bs, 1, D_B), F32), jax.ShapeDtypeStruct((bs, 1, D_C), F32),
            jax.ShapeDtypeStruct(cst_b.shape, F32), jax.ShapeDtypeStruct(cst_c.shape, F32),
            jax.ShapeDtypeStruct(h0.shape, F32), jax.ShapeDtypeStruct(s0.shape, F32)]
    return pl.pallas_call(
        _sample_step_kernel,
        grid=(bs,),
        in_specs=[per_b(a) for a in data] + [full(a) for a in params],
        out_specs=[per_b(a) for a in outs],
        out_shape=outs,
        compiler_params=_cparams("arbitrary"),
    )(*data, *params)


def _permute_w_in(w):
    sizes = (2 * H_A * DH_A, 2 * H_A * DH_A, D_A, CONV_B, H_B, CONV_C, H_C, H_C, D_MIX)
    offs = [0]
    for s in sizes:
        offs.append(offs[-1] + s)
    col = lambda i: w[:, offs[i]:offs[i + 1]]
    small = jnp.concatenate([col(4), col(6), col(7),
                             jnp.zeros((w.shape[0], SMALL_W - H_B - 2 * H_C), w.dtype)], axis=1)
    return jnp.concatenate([col(0), col(1), col(2), col(3), col(5), col(8), small], axis=1).astype(BF16)


def _layer_params(l, w):
    pad_small = lambda a_ssm, a_gdn: jnp.concatenate(
        [a_ssm, jnp.zeros((H_C,), F32), a_gdn, jnp.zeros((SMALL_W - H_B - 2 * H_C,), F32)]).reshape(1, SMALL_W)
    rep = lambda a: jnp.repeat(a, P_B).reshape(1, D_B)
    expand = (jnp.arange(SMALL_W)[:, None] == (jnp.arange(D_B)[None, :] // P_B)).astype(F32)
    return {
        "ssm_conv_w": w["ssm_conv_w"][l], "ssm_conv_b": w["ssm_conv_b"][l].reshape(1, CONV_B),
        "gdn_conv_w": w["gdn_conv_w"][l],
        "dtb_small": pad_small(w["ssm_dt_bias"][l], w["gdn_dt_bias"][l]),
        "alog_small": pad_small(w["ssm_A_log"][l], w["gdn_A_log"][l]),
        "ssm_dtb_e": rep(w["ssm_dt_bias"][l]), "ssm_alog_e": rep(w["ssm_A_log"][l]),
        "ssm_d_e": rep(w["ssm_D"][l]),
        "ssm_norm_g": w["ssm_norm_g"][l].reshape(1, D_B), "gdn_norm_g": w["gdn_norm_g"][l].reshape(1, DV_C),
        "expand": expand,
    }


def _pick(n, prefs):
    for t in prefs:
        if n % t == 0:
            return t
    return n


def kernel(x_prompt, x_sample, c_prompt, c_sample, cache_k, cache_v, state_ssm, state_ssm_conv, state_gdn, state_gdn_conv, page_table, w_ada, b_ada, w_in, w_out, rel_bias, attn_lambda, attn_subln_g, ssm_conv_w, ssm_conv_b, ssm_dt_bias, ssm_A_log, ssm_D, ssm_norm_g, gdn_conv_w, gdn_dt_bias, gdn_A_log, gdn_norm_g, final_norm_g):
    depth = w_in.shape[0]
    bp, L, _ = x_prompt.shape
    bs = x_sample.shape[0]
    n_pages = page_table.shape[1]
    past = n_pages * PAGE_SIZE
    n_pool = cache_k.shape[1]
    wts = dict(ssm_conv_w=ssm_conv_w, ssm_conv_b=ssm_conv_b, ssm_dt_bias=ssm_dt_bias, ssm_A_log=ssm_A_log,
               ssm_D=ssm_D, ssm_norm_g=ssm_norm_g, gdn_conv_w=gdn_conv_w, gdn_dt_bias=gdn_dt_bias,
               gdn_A_log=gdn_A_log, gdn_norm_g=gdn_norm_g)

    tm = _pick(L, (256, 128, 64))
    T = _pick(L, (256, 128))
    Q = _pick(L, (128, 64))
    R = _pick(L, (256, 128, 64))
    G = _pick(n_pages, (8, 4, 2, 1))

    mod = _modulation(jnp.concatenate([c_prompt, c_sample], axis=0), w_ada, b_ada)
    bias_near = _near_bias_tiles(rel_bias, T)
    bias_pages = _decode_bias(rel_bias, past)
    bias_new = jnp.broadcast_to(jnp.tile(rel_bias[0].astype(F32), 2)[:, None], (2 * H_A, DV_A))
    ck = cache_k.reshape(depth, n_pool, PAGE_SIZE * H_A, 2 * DH_A)
    cv = cache_v.reshape(depth, n_pool, PAGE_SIZE * H_A, DV_A)
    lane_half = jnp.arange(DV_A) < DH_A

    xp = x_prompt
    xs = x_sample.reshape(1, bs, D_MODEL)
    new_p, new_s = [], []
    for l in range(depth):
        lam_init = 0.8 - 0.6 * math.exp(-0.3 * l)
        p = _layer_params(l, wts)
        w_perm = _permute_w_in(w_in[l])
        w_o = w_out[l].astype(BF16)
        shift, scale, gate = jnp.split(mod[l], 3, axis=-1)
        mp = lambda a: a[:bp].reshape(bp, 1, D_MODEL)
        ms = lambda a: a[bp:].reshape(1, bs, D_MODEL)

        q, k, v, xbc, qkv, z, sm = _in_projection(xp, mp(scale), mp(shift), w_perm, tm)
        o_a = _prompt_attention(q, k, v, z, bias_near, attn_lambda[l], attn_subln_g[l], lam_init, T)
        o_b, ssm_h = _prompt_ssd(xbc, sm, z, p, Q)
        o_c, gdn_s = _prompt_gdn(qkv, sm, z, p, R)
        final = final_norm_g if l == depth - 1 else None
        xp = _out_projection(xp, mp(gate), o_a, o_b, o_c, w_o, final, tm)
        new_p.append((k.reshape(bp, L, H_A, 2 * DH_A), v.reshape(bp, L, H_A, DV_A),
                      ssm_h.reshape(bp, H_B, P_B, N_B), xbc[:, L - (CONV_W - 1):, :],
                      gdn_s.reshape(bp, H_C, DK_C, DV_C), qkv[:, L - (CONV_W - 1):, :]))

        q, k, v, xbc, qkv, z, sm = _in_projection(xs, ms(scale), ms(shift), w_perm, bs)
        q4 = q.reshape(bs, H_A, DV_A) * (DH_A ** -0.5)
        qpat = jnp.concatenate([jnp.where(lane_half, q4, 0.0), jnp.where(lane_half, 0.0, q4)], axis=1)
        k4 = k.reshape(bs, H_A, DV_A)
        v4 = v.reshape(bs, H_A, DV_A)
        o_a = _decode_attention(l, page_table, ck, cv, qpat, jnp.concatenate([k4, k4], axis=1),
                                jnp.concatenate([v4, v4], axis=1), bias_pages, bias_new,
                                z[..., :D_A].reshape(bs, H_A, DV_A), attn_lambda[l], attn_subln_g[l],
                                lam_init, G)
        row = lambda a: a.reshape(bs, 1, a.shape[-1])
        o_b, o_c, conv_b1, conv_c1, ssm_h, gdn_s = _sample_step(
            row(xbc), row(qkv), row(sm), row(z), state_ssm_conv[l], state_gdn_conv[l],
            state_ssm[l].reshape(bs, D_B, N_B), state_gdn[l].reshape(bs, H_C * DK_C, DV_C), p)
        xs = _out_projection(xs, ms(gate), o_a.reshape(1, bs, D_A).astype(BF16),
                             o_b.reshape(1, bs, D_B).astype(BF16), o_c.reshape(1, bs, D_C).astype(BF16),
                             w_o, final, bs)
        new_s.append((k.reshape(bs, 1, H_A, 2 * DH_A), v.reshape(bs, 1, H_A, DV_A),
                      ssm_h.reshape(bs, H_B, P_B, N_B), conv_b1,
                      gdn_s.reshape(bs, H_C, DK_C, DV_C), conv_c1))

    k_p, v_p, ssm_p, ssm_conv_p, gdn_p, gdn_conv_p = [jnp.stack(a) for a in zip(*new_p)]
    k_s, v_s, ssm_s, ssm_conv_s, gdn_s, gdn_conv_s = [jnp.stack(a) for a in zip(*new_s)]
    return (xp, xs.reshape(bs, 1, D_MODEL), k_p, v_p, ssm_p, ssm_conv_p, gdn_p, gdn_conv_p,
            k_s, v_s, ssm_s, ssm_conv_s, gdn_s, gdn_conv_s)
```

```python
import functools
import math

import jax
import jax.numpy as jnp
from jax import lax
from jax.experimental import pallas as pl
from jax.experimental.pallas import tpu as pltpu

F32 = jnp.float32
BF16 = jnp.bfloat16
HIGHEST = lax.Precision.HIGHEST

D_MODEL = 1024
H_A, DH_A, DV_A = 4, 64, 128
D_A = H_A * DV_A
H_B, P_B, G_B, N_B = 8, 64, 2, 128
D_B = H_B * P_B
CONV_B = D_B + 2 * G_B * N_B
H_C, DK_C, DV_C = 4, 128, 128
D_C = H_C * DV_C
CONV_C = 2 * H_C * DK_C + D_C
CONV_W = 4
D_MIX = D_A + D_B + D_C
REL_BUCKETS, REL_MAX_DIST = 32, 128
PAGE_SIZE = 128
EPS = 1e-6
GDN_CHUNK = 64

LANES = 128
SUBLANES = 8
SMALL_W = LANES
DT_LANE, BETA_LANE, AG_LANE = 0, H_B, H_B + H_C
NEG_BIG = -1e30
VMEM_LIMIT = 56 * 1024 * 1024

SEGMENTS = (("q", 2 * H_A * DH_A), ("k", 2 * H_A * DH_A), ("v", D_A), ("xbc", CONV_B),
            ("qkv", CONV_C), ("z", D_MIX), ("small", SMALL_W))


def _cparams(*sem):
    return pltpu.CompilerParams(dimension_semantics=sem, vmem_limit_bytes=VMEM_LIMIT)


def _silu(x):
    return x * jax.nn.sigmoid(x)


def _softplus(x):
    return jnp.maximum(x, 0.0) + jnp.log1p(jnp.exp(-jnp.abs(x)))


def _rms(x):
    return x * lax.rsqrt(jnp.mean(x * x, axis=-1, keepdims=True) + EPS)


def _dot_nt(a, b):
    return lax.dot_general(a, b, (((1,), (1,)), ((), ())), preferred_element_type=F32)


def _dot_tn(a, b):
    return lax.dot_general(a, b, (((0,), (0,)), ((), ())), preferred_element_type=F32)


def _dot_tn_hi(a, b):
    return lax.dot_general(a, b, (((0,), (0,)), ((), ())), preferred_element_type=F32, precision=HIGHEST)


def _dot(a, b):
    return jnp.dot(a, b, preferred_element_type=F32)


def _iota(shape, dim):
    return lax.broadcasted_iota(jnp.int32, shape, dim)


def _mod_kernel(c_ref, w_ref, b_ref, o_ref):
    sc = _silu(c_ref[...]).astype(BF16)
    o_ref[...] = _dot(sc, w_ref[...].astype(BF16)) + b_ref[...]


def _modulation(c_all, w_ada, b_ada):
    depth = w_ada.shape[0]
    n = c_all.shape[0]
    return pl.pallas_call(
        _mod_kernel,
        grid=(depth, 3),
        in_specs=[pl.BlockSpec((n, D_MODEL), lambda l, j: (0, 0)),
                  pl.BlockSpec((None, D_MODEL, D_MODEL), lambda l, j: (l, 0, j)),
                  pl.BlockSpec((None, 1, D_MODEL), lambda l, j: (l, 0, j))],
        out_specs=pl.BlockSpec((None, n, D_MODEL), lambda l, j: (l, 0, j)),
        out_shape=jax.ShapeDtypeStruct((depth, n, 3 * D_MODEL), F32),
        compiler_params=_cparams("arbitrary", "arbitrary"),
    )(c_all, w_ada, b_ada.reshape(depth, 1, 3 * D_MODEL))


def _inproj_kernel(x_ref, scale_ref, shift_ref, w_ref, *out_refs):
    h = _rms(x_ref[...]) * (1.0 + scale_ref[...]) + shift_ref[...]
    hb = h.astype(BF16)
    off = 0
    for ref in out_refs:
        n = ref.shape[-1]
        ref[...] = _dot(hb, w_ref[:, off:off + n])
        off += n


def _in_projection(x, scale, shift, w_perm, tm):
    b, L, _ = x.shape
    ms = scale.shape[1]
    mod_rows = tm if ms == L else 1
    mod_map = (lambda i, j: (i, j, 0)) if ms == L else (lambda i, j: (i, 0, 0))
    n_tot = w_perm.shape[1]
    return pl.pallas_call(
        _inproj_kernel,
        grid=(b, L // tm),
        in_specs=[pl.BlockSpec((None, tm, D_MODEL), lambda i, j: (i, j, 0)),
                  pl.BlockSpec((None, mod_rows, D_MODEL), mod_map),
                  pl.BlockSpec((None, mod_rows, D_MODEL), mod_map),
                  pl.BlockSpec((D_MODEL, n_tot), lambda i, j: (0, 0))],
        out_specs=[pl.BlockSpec((None, tm, w), lambda i, j: (i, j, 0)) for _, w in SEGMENTS],
        out_shape=[jax.ShapeDtypeStruct((b, L, w), F32) for _, w in SEGMENTS],
        compiler_params=_cparams("arbitrary", "arbitrary"),
    )(x, scale, shift, w_perm)


def _outproj_kernel(x_ref, gate_ref, oa_ref, ob_ref, oc_ref, w_ref, *rest, final):
    acc = _dot(oa_ref[...], w_ref[0:D_A, :])
    acc += _dot(ob_ref[...], w_ref[D_A:D_A + D_B, :])
    acc += _dot(oc_ref[...], w_ref[D_A + D_B:D_MIX, :])
    y = x_ref[...] + gate_ref[...] * acc
    if final:
        g_ref, o_ref = rest
        o_ref[...] = _rms(y) * g_ref[...]
    else:
        (o_ref,) = rest
        o_ref[...] = y


def _out_projection(x, gate, oa, ob, oc, w_out, final_g, tm):
    b, L, _ = x.shape
    ms = gate.shape[1]
    mod_rows = tm if ms == L else 1
    mod_map = (lambda i, j: (i, j, 0)) if ms == L else (lambda i, j: (i, 0, 0))
    row = lambda w: pl.BlockSpec((None, tm, w), lambda i, j: (i, j, 0))
    in_specs = [row(D_MODEL), pl.BlockSpec((None, mod_rows, D_MODEL), mod_map),
                row(D_A), row(D_B), row(D_C),
                pl.BlockSpec((D_MIX, D_MODEL), lambda i, j: (0, 0))]
    args = [x, gate, oa, ob, oc, w_out]
    final = final_g is not None
    if final:
        in_specs.append(pl.BlockSpec((1, D_MODEL), lambda i, j: (0, 0)))
        args.append(final_g.reshape(1, D_MODEL))
    return pl.pallas_call(
        functools.partial(_outproj_kernel, final=final),
        grid=(b, L // tm),
        in_specs=in_specs,
        out_specs=row(D_MODEL),
        out_shape=jax.ShapeDtypeStruct((b, L, D_MODEL), F32),
        compiler_params=_cparams("arbitrary", "arbitrary"),
    )(*args)


def _lambda_value(lp, lam_init):
    s01 = jnp.sum(lp[0:1, :] * lp[1:2, :], axis=-1, keepdims=True)
    s23 = jnp.sum(lp[2:3, :] * lp[3:4, :], axis=-1, keepdims=True)
    return jnp.exp(s01) - jnp.exp(s23) + lam_init


def _rel_bias_values(table, n):
    exact = REL_BUCKETS // 2
    large = exact + (jnp.log(jnp.maximum(n, 1).astype(F32) / exact)
                     / math.log(REL_MAX_DIST / exact) * (REL_BUCKETS - exact)).astype(jnp.int32)
    bucket = jnp.where(n < exact, n, jnp.minimum(large, REL_BUCKETS - 1))
    onehot = bucket[..., None, None] == jnp.arange(REL_BUCKETS)[:, None]
    return jnp.sum(jnp.where(onehot, table.astype(F32), 0.0), axis=-2)


def _attn_kernel(q_ref, k_ref, v_ref, z_ref, bias_ref, lamp_ref, subln_ref, o_ref, kb_scr, vb_scr,
                 s_scr, w_scr, *, T, lam_init):
    L = q_ref.shape[0]
    kb_scr[...] = k_ref[...].astype(BF16)
    vb_scr[...] = v_ref[...].astype(BF16)
    lam = _lambda_value(lamp_ref[...], lam_init)
    lane = _iota((T, DV_A), 1)
    half = LANES
    for i in range(L // T):
        rows = slice(i * T, (i + 1) * T)
        q = q_ref[rows, :] * (DH_A ** -0.5)
        scale = []
        for sub in range(2):
            qm = jnp.where((lane < DH_A) if sub == 0 else (lane >= DH_A), q, 0.0).astype(BF16)
            mx = None
            for j in range(i + 1):
                s = _dot_nt(qm, kb_scr[j * T:(j + 1) * T, :])
                if j == i:
                    s = s + bias_ref[:, T:2 * T]
                elif j == i - 1:
                    s = s + bias_ref[:, 0:T]
                s_scr[sub, j] = s
                for c0 in range(0, T, half):
                    t = s[:, c0:c0 + half]
                    mx = t if mx is None else jnp.maximum(mx, t)
            m_row = jnp.max(mx, axis=-1, keepdims=True)
            ls = None
            for j in range(i + 1):
                p = jnp.exp(s_scr[sub, j] - m_row)
                s_scr[sub, j] = p
                for c0 in range(0, T, half):
                    t = p[:, c0:c0 + half]
                    ls = t if ls is None else ls + t
            scale.append(1.0 / jnp.sum(ls, axis=-1, keepdims=True))
        r0, r1 = scale[0], lam * scale[1]
        for j in range(i + 1):
            w_scr[:, j * T:(j + 1) * T] = (s_scr[0, j] * r0 - s_scr[1, j] * r1).astype(BF16)
        o = _dot(w_scr[:, 0:(i + 1) * T], vb_scr[0:(i + 1) * T, :])
        o = _rms(o) * subln_ref[...] * (1.0 - lam_init) * _silu(z_ref[rows, :])
        o_ref[rows, :] = o.astype(o_ref.dtype)


def _prompt_attention(q, k, v, z, bias_near, lam_p, subln, lam_init, T):
    b, L, _ = q.shape
    seq = lambda: pl.BlockSpec((None, L, DV_A), lambda i, h: (i, 0, h))
    return pl.pallas_call(
        functools.partial(_attn_kernel, T=T, lam_init=lam_init),
        grid=(b, H_A),
        in_specs=[seq(), seq(), seq(), seq(),
                  pl.BlockSpec((None, T, 2 * T), lambda i, h: (h, 0, 0)),
                  pl.BlockSpec((4, DH_A), lambda i, h: (0, 0)),
                  pl.BlockSpec((1, DV_A), lambda i, h: (0, 0))],
        out_specs=seq(),
        out_shape=jax.ShapeDtypeStruct((b, L, D_A), BF16),
        scratch_shapes=[pltpu.VMEM((L, DV_A), BF16), pltpu.VMEM((L, DV_A), BF16),
                        pltpu.VMEM((2, L // T, T, T), F32), pltpu.VMEM((T, L), BF16)],
        compiler_params=_cparams("arbitrary", "arbitrary"),
    )(q, k, v, z, bias_near, lam_p, subln.reshape(1, DV_A))


def _near_bias_tiles(table, T):
    n = T + jnp.arange(T)[:, None] - jnp.arange(2 * T)[None, :]
    vals = _rel_bias_values(table, jnp.maximum(n, 0)) - table[REL_BUCKETS - 1].astype(F32)
    vals = jnp.where((n >= 0)[..., None], vals, NEG_BIG)
    return jnp.moveaxis(vals, -1, 0)


def _decode_kernel(pt_ref, qpat_ref, knew_ref, vnew_ref, bias_ref, biasnew_ref, z_ref, lamp_ref,
                   subln_ref, *rest, G, lam_init):
    del pt_ref
    k_refs, v_refs = rest[:G], rest[G:2 * G]
    o_ref, m_scr, l_scr, acc_scr = rest[2 * G:]
    j = pl.program_id(1)

    @pl.when(j == 0)
    def _():
        m_scr[...] = jnp.full(m_scr.shape, NEG_BIG, F32)
        l_scr[...] = jnp.zeros(l_scr.shape, F32)
        acc_scr[...] = jnp.zeros(acc_scr.shape, F32)

    qf = qpat_ref[...]
    qb = qf.astype(BF16)
    s = [_dot_nt(qb, k_refs[g][...].astype(BF16)) + bias_ref[g] for g in range(G)]
    m_old = m_scr[...]
    m_tile = s[0]
    for g in range(1, G):
        m_tile = jnp.maximum(m_tile, s[g])
    m = jnp.maximum(m_old, jnp.max(m_tile, axis=-1, keepdims=True))
    alpha = jnp.exp(m_old - m)
    p = [jnp.exp(s[g] - m) for g in range(G)]
    p_sum = p[0]
    for g in range(1, G):
        p_sum = p_sum + p[g]
    l = alpha * l_scr[...] + jnp.sum(p_sum, axis=-1, keepdims=True)
    pv = _dot(p[0].astype(BF16), v_refs[0][...].astype(BF16))
    for g in range(1, G):
        pv = pv + _dot(p[g].astype(BF16), v_refs[g][...].astype(BF16))
    acc = alpha * acc_scr[...] + pv
    m_scr[...] = m
    l_scr[...] = l
    acc_scr[...] = acc

    @pl.when(j == pl.num_programs(1) - 1)
    def _():
        s_new = jnp.sum(qf * knew_ref[...], axis=-1, keepdims=True) + biasnew_ref[:, 0:1]
        m_new = jnp.maximum(m, s_new)
        alpha = jnp.exp(m - m_new)
        p_new = jnp.exp(s_new - m_new)
        l_fin = alpha * l + p_new
        out = (alpha * acc + p_new * vnew_ref[...]) / l_fin
        lam = _lambda_value(lamp_ref[...], lam_init)
        o = out[0:H_A, :] - lam * out[H_A:2 * H_A, :]
        o_ref[...] = _rms(o) * subln_ref[...] * (1.0 - lam_init) * _silu(z_ref[...])


def _decode_attention(layer, page_table, cache_k, cache_v, qpat, knew, vnew, bias_pages, bias_new, z3,
                      lam_p, subln, lam_init, G):
    bs, n_pages = page_table.shape
    rows = PAGE_SIZE * H_A

    def page_spec(g):
        return pl.BlockSpec((None, None, rows, DV_A),
                            lambda i, j, pt: (layer, pt[i, j * G + g], 0, 0))

    per_b = lambda r: pl.BlockSpec((None, r, DV_A), lambda i, j, pt: (i, 0, 0))
    grid_spec = pltpu.PrefetchScalarGridSpec(
        num_scalar_prefetch=1,
        grid=(bs, n_pages // G),
        in_specs=[per_b(2 * H_A), per_b(2 * H_A), per_b(2 * H_A),
                  pl.BlockSpec((G, 2 * H_A, rows), lambda i, j, pt: (j, 0, 0)),
                  pl.BlockSpec((2 * H_A, DV_A), lambda i, j, pt: (0, 0)),
                  per_b(H_A),
                  pl.BlockSpec((4, DH_A), lambda i, j, pt: (0, 0)),
                  pl.BlockSpec((1, DV_A), lambda i, j, pt: (0, 0))]
                 + [page_spec(g) for g in range(G)] + [page_spec(g) for g in range(G)],
        out_specs=per_b(H_A),
        scratch_shapes=[pltpu.VMEM((2 * H_A, 1), F32), pltpu.VMEM((2 * H_A, 1), F32),
                        pltpu.VMEM((2 * H_A, DV_A), F32)],
    )
    return pl.pallas_call(
        functools.partial(_decode_kernel, G=G, lam_init=lam_init),
        grid_spec=grid_spec,
        out_shape=jax.ShapeDtypeStruct((bs, H_A, DV_A), F32),
        compiler_params=_cparams("arbitrary", "arbitrary"),
    )(page_table, qpat, knew, vnew, bias_pages, bias_new, z3, lam_p, subln.reshape(1, DV_A),
      *([cache_k] * G), *([cache_v] * G))


def _decode_bias(table, past):
    n = past - jnp.arange(past)
    vals = _rel_bias_values(table, n).reshape(past // PAGE_SIZE, PAGE_SIZE, H_A)
    vals = jnp.moveaxis(vals, -1, 1)[..., None]
    same = (jnp.arange(H_A)[:, None, None] == jnp.arange(H_A)[None, None, :])
    full = jnp.where(same[None], vals, NEG_BIG).reshape(past // PAGE_SIZE, H_A, PAGE_SIZE * H_A)
    return jnp.concatenate([full, full], axis=1)


def _conv_taps(xp_ref, x, w_ref, rows):
    xp_ref[SUBLANES:SUBLANES + rows, :] = x
    y = w_ref[CONV_W - 1:CONV_W, :] * x
    for i in range(CONV_W - 1):
        y = y + w_ref[i:i + 1, :] * xp_ref[pl.ds(SUBLANES - (CONV_W - 1) + i, rows), :]
    xp_ref[0:SUBLANES, :] = x[rows - SUBLANES:rows, :]
    return y


def _ssd_kernel(xbc_ref, sm_ref, z_ref, cw_ref, cb_ref, dtb_s_ref, alog_s_ref, dtb_e_ref, alog_e_ref,
                dskip_ref, ng_ref, exp_ref, y_ref, hout_ref, xp_scr, h_scr, *, Q):
    c = pl.program_id(1)

    @pl.when(c == 0)
    def _():
        xp_scr[0:SUBLANES, :] = jnp.zeros((SUBLANES, CONV_B), F32)
        h_scr[...] = jnp.zeros(h_scr.shape, F32)

    xa = _silu(_conv_taps(xp_scr, xbc_ref[...], cw_ref, Q) + cb_ref[...])
    xs = xa[:, 0:D_B]
    sm = sm_ref[...]
    row = _iota((Q, Q), 0)
    col = _iota((Q, Q), 1)
    tril = row >= col
    expand = exp_ref[...]

    a_s = _softplus(sm + dtb_s_ref[...]) * (-jnp.exp(alog_s_ref[...]))
    cum_s = _dot_exact_rhs(tril.astype(BF16), a_s)
    cum_t = cum_s.T
    dt_e = _softplus(_dot_exact_lhs(sm, expand) + dtb_e_ref[...])
    cum_e = _dot_exact_lhs(cum_s, expand)
    xc = xs * dt_e
    dec_end = jnp.exp(cum_e[Q - 1:Q, :] - cum_e)
    dec_in = jnp.exp(cum_e)
    xc_b = xc.astype(BF16)
    xe_b = (xc * dec_end).astype(BF16)
    lane = _iota((Q, LANES), 1)
    prow = _iota((LANES, LANES), 0)
    heads_per_group = H_B // G_B
    pairs = []
    for g in range(G_B):
        bm = xa[:, D_B + g * N_B:D_B + (g + 1) * N_B].astype(BF16)
        cm = xa[:, D_B + G_B * N_B + g * N_B:D_B + G_B * N_B + (g + 1) * N_B].astype(BF16)
        cb = _dot_nt(cm, bm)
        for pr in range(g * heads_per_group // 2, (g + 1) * heads_per_group // 2):
            lo, hi = pr * LANES, (pr + 1) * LANES
            yd = []
            for hh in (2 * pr, 2 * pr + 1):
                seg = cum_s[:, hh:hh + 1] - cum_t[hh:hh + 1, :]
                mat = cb * jnp.exp(jnp.where(tril, seg, -jnp.inf))
                yd.append(_dot(mat.astype(BF16), xc_b[:, lo:hi]))
            y_diag = jnp.where(lane < P_B, yd[0], yd[1])
            hp = h_scr[lo:hi, :]
            y_off = _dot_nt(cm, hp.astype(BF16)) * dec_in[:, lo:hi]
            st = _dot_tn(xe_b[:, lo:hi], bm)
            a_last = jnp.where(prow < P_B, cum_t[2 * pr:2 * pr + 1, Q - 1:Q],
                               cum_t[2 * pr + 1:2 * pr + 2, Q - 1:Q])
            h_scr[lo:hi, :] = hp * jnp.exp(a_last) + st
            pairs.append(y_diag + y_off + xs[:, lo:hi] * dskip_ref[:, lo:hi])
    y = jnp.concatenate(pairs, axis=-1) * _silu(z_ref[...])
    gw = D_B // G_B
    y = jnp.concatenate([_rms(y[:, g * gw:(g + 1) * gw]) for g in range(G_B)], axis=-1) * ng_ref[...]
    y_ref[...] = y.astype(y_ref.dtype)

    @pl.when(c == pl.num_programs(1) - 1)
    def _():
        hout_ref[...] = h_scr[...]


def _prompt_ssd(xbc, sm, z, p, Q):
    b, L, _ = xbc.shape
    full = lambda a: pl.BlockSpec(a.shape, lambda i, j: (0,) * a.ndim)
    params = [p["ssm_conv_w"], p["ssm_conv_b"], p["dtb_small"], p["alog_small"], p["ssm_dtb_e"],
              p["ssm_alog_e"], p["ssm_d_e"], p["ssm_norm_g"], p["expand"]]
    return pl.pallas_call(
        functools.partial(_ssd_kernel, Q=Q),
        grid=(b, L // Q),
        in_specs=[pl.BlockSpec((None, Q, CONV_B), lambda i, j: (i, j, 0)),
                  pl.BlockSpec((None, Q, SMALL_W), lambda i, j: (i, j, 0)),
                  pl.BlockSpec((None, Q, D_B), lambda i, j: (i, j, D_A // D_B))]
                 + [full(a) for a in params],
        out_specs=[pl.BlockSpec((None, Q, D_B), lambda i, j: (i, j, 0)),
                   pl.BlockSpec((None, D_B, N_B), lambda i, j: (i, 0, 0))],
        out_shape=[jax.ShapeDtypeStruct((b, L, D_B), BF16),
                   jax.ShapeDtypeStruct((b, D_B, N_B), F32)],
        scratch_shapes=[pltpu.VMEM((Q + SUBLANES, CONV_B), F32), pltpu.VMEM((D_B, N_B), F32)],
        compiler_params=_cparams("arbitrary", "arbitrary"),
    )(xbc, sm, z, *params)


def _split2(a):
    hi = a.astype(BF16)
    return hi, (a - hi.astype(F32)).astype(BF16)


def _dot_split(a, b):
    return _dot(a[0], b[0]) + (_dot(a[0], b[1]) + _dot(a[1], b[0]))


def _split3(x):
    x1 = x.astype(BF16)
    r1 = x - x1.astype(F32)
    x2 = r1.astype(BF16)
    return x1, x2, (r1 - x2.astype(F32)).astype(BF16)


def _dot_exact_rhs(a_bf16, x):
    x1, x2, x3 = _split3(x)
    return _dot(a_bf16, x1) + (_dot(a_bf16, x2) + _dot(a_bf16, x3))


def _dot_exact_lhs(x, a_bf16):
    x1, x2, x3 = _split3(x)
    return _dot(x1, a_bf16) + (_dot(x2, a_bf16) + _dot(x3, a_bf16))


def _gdn_chunk_prepare(act, gcum, gcum_t, beta_s, r0):
    cs, n = GDN_CHUNK, H_C * GDN_CHUNK
    rows = slice(r0, r0 + cs)
    qs, ks, vs = [], [], []
    for h in range(H_C):
        q = act[rows, h * DK_C:(h + 1) * DK_C]
        k = act[rows, D_C + h * DK_C:D_C + (h + 1) * DK_C]
        qs.append(q * lax.rsqrt(jnp.sum(q * q, axis=-1, keepdims=True) + EPS) * (DK_C ** -0.5))
        ks.append(k * lax.rsqrt(jnp.sum(k * k, axis=-1, keepdims=True) + EPS))
        vs.append(act[rows, 2 * D_C + h * DV_C:2 * D_C + (h + 1) * DV_C])
    q_all, k_all, v_all = (jnp.concatenate(t, axis=0) for t in (qs, ks, vs))
    col_of = lambda a, lane0: jnp.concatenate([a[rows, lane0 + h:lane0 + h + 1] for h in range(H_C)], axis=0)
    beta = col_of(beta_s, BETA_LANE)
    gc = col_of(gcum, AG_LANE)
    gr = jnp.concatenate([gcum_t[AG_LANE + h:AG_LANE + h + 1, rows] for h in range(H_C)], axis=1)
    g_last = jnp.concatenate([jnp.broadcast_to(gcum[r0 + cs - 1:r0 + cs, AG_LANE + h:AG_LANE + h + 1], (cs, 1))
                              for h in range(H_C)], axis=0)
    ri = _iota((n, n), 0)
    ci = _iota((n, n), 1)
    same_head = lax.shift_right_logical(ri, 6) == lax.shift_right_logical(ci, 6)
    decay = jnp.exp(jnp.where(same_head & (ri >= ci), gc - gr, -jnp.inf))
    kb = k_all.astype(BF16)
    neg_lower = jnp.where(same_head & (ri > ci), -(beta * _dot_nt(kb, kb) * decay), 0.0)
    rhs = jnp.concatenate([v_all * beta, k_all * (beta * jnp.exp(gc))], axis=-1)
    attn = (_dot_nt(q_all.astype(BF16), kb) * decay).astype(BF16)
    qg = (q_all * jnp.exp(gc)).astype(BF16)
    kd = (k_all * jnp.exp(g_last - gc)).astype(BF16)
    return neg_lower, rhs, attn, qg, kd, jnp.exp(g_last)


def _neumann_solve(ps, xs):
    n_levels = GDN_CHUNK.bit_length() - 1
    for lvl in range(n_levels):
        p_split = [_split2(p) for p in ps]
        xs = [x + _dot_split(pp, _split2(x)) for pp, x in zip(p_split, xs)]
        if lvl + 1 < n_levels:
            ps = [_dot_split(pp, pp) for pp in p_split]
    return xs


def _gdn_kernel(qkv_ref, sm_ref, z_ref, cw_ref, dtb_s_ref, alog_s_ref, ng_ref, o_ref, sout_ref,
                xp_scr, s_scr, act_scr, *, R):
    c = pl.program_id(1)

    @pl.when(c == 0)
    def _():
        xp_scr[0:SUBLANES, :] = jnp.zeros((SUBLANES, CONV_C), F32)
        s_scr[...] = jnp.zeros(s_scr.shape, F32)

    cs = GDN_CHUNK
    act_scr[...] = _silu(_conv_taps(xp_scr, qkv_ref[...], cw_ref, R))
    sm = sm_ref[...]
    g_s = _softplus(sm + dtb_s_ref[...]) * (-jnp.exp(alog_s_ref[...]))
    beta_s = jax.nn.sigmoid(sm)
    ri = _iota((R, R), 0)
    ci = _iota((R, R), 1)
    chunk_tril = (lax.shift_right_logical(ri, 6) == lax.shift_right_logical(ci, 6)) & (ri >= ci)
    gcum = _dot_exact_rhs(chunk_tril.astype(BF16), g_s)
    gcum_t = gcum.T

    n_chunks = R // cs
    prep = [_gdn_chunk_prepare(act_scr, gcum, gcum_t, beta_s, ch * cs) for ch in range(n_chunks)]
    sols = _neumann_solve([pr[0] for pr in prep], [pr[1] for pr in prep])
    for ch in range(n_chunks):
        _, _, attn, qg, kd, e_last = prep[ch]
        u, w = sols[ch][:, 0:DV_C], sols[ch][:, DV_C:2 * DV_C]
        w_b = w.astype(BF16)
        hs = lambda a, h: a[h * cs:(h + 1) * cs, :]
        s_old = [s_scr[h * DK_C:(h + 1) * DK_C, :] for h in range(H_C)]
        s_b = [s.astype(BF16) for s in s_old]
        v_new = u - jnp.concatenate([_dot(hs(w_b, h), s_b[h]) for h in range(H_C)], axis=0)
        v_new_b = v_new.astype(BF16)
        o = jnp.concatenate([_dot(hs(qg, h), s_b[h]) for h in range(H_C)], axis=0) + _dot(attn, v_new_b)
        for h in range(H_C):
            s_scr[h * DK_C:(h + 1) * DK_C, :] = (s_old[h] * e_last[h * cs:h * cs + 1, :]
                                                  + _dot_tn(hs(kd, h), hs(v_new_b, h)))
        o = jnp.concatenate([_rms(hs(o, h)) * ng_ref[...] for h in range(H_C)], axis=-1)
        rows = slice(ch * cs, (ch + 1) * cs)
        o_ref[rows, :] = (o * _silu(z_ref[rows, :])).astype(o_ref.dtype)

    @pl.when(c == pl.num_programs(1) - 1)
    def _():
        sout_ref[...] = s_scr[...]


def _prompt_gdn(qkv, sm, z, p, R):
    b, L, _ = qkv.shape
    full = lambda a: pl.BlockSpec(a.shape, lambda i, j: (0,) * a.ndim)
    params = [p["gdn_conv_w"], p["dtb_small"], p["alog_small"], p["gdn_norm_g"]]
    return pl.pallas_call(
        functools.partial(_gdn_kernel, R=R),
        grid=(b, L // R),
        in_specs=[pl.BlockSpec((None, R, CONV_C), lambda i, j: (i, j, 0)),
                  pl.BlockSpec((None, R, SMALL_W), lambda i, j: (i, j, 0)),
                  pl.BlockSpec((None, R, D_C), lambda i, j: (i, j, (D_A + D_B) // D_C))]
                 + [full(a) for a in params],
        out_specs=[pl.BlockSpec((None, R, D_C), lambda i, j: (i, j, 0)),
                   pl.BlockSpec((None, H_C * DK_C, DV_C), lambda i, j: (i, 0, 0))],
        out_shape=[jax.ShapeDtypeStruct((b, L, D_C), BF16),
                   jax.ShapeDtypeStruct((b, H_C * DK_C, DV_C), F32)],
        scratch_shapes=[pltpu.VMEM((R + SUBLANES, CONV_C), F32),
                        pltpu.VMEM((H_C * DK_C, DV_C), F32),
                        pltpu.VMEM((R, CONV_C), F32)],
        compiler_params=_cparams("arbitrary", "arbitrary"),
    )(qkv, sm, z, *params)


def _expand_heads(vals, n_heads, width):
    lane = _iota((1, n_heads * width), 1)
    out = jnp.zeros((1, n_heads * width), F32)
    for h in range(n_heads):
        out = jnp.where((lane >= h * width) & (lane < (h + 1) * width), vals[:, h:h + 1], out)
    return out


def _pad_rows(rows):
    n = rows[0].shape[-1]
    r = _iota((SUBLANES, n), 0)
    out = jnp.zeros((SUBLANES, n), F32)
    for i, v in enumerate(rows):
        out = jnp.where(r == i, v, out)
    return out


def _sample_step_kernel(xbc_ref, qkv_ref, sm_ref, z_ref, cst_b_ref, cst_c_ref, h0_ref, s0_ref,
                        cwb_ref, cbb_ref, cwc_ref, dtb_s_ref, alog_s_ref, dskip_ref, ngb_ref, ngc_ref,
                        ob_ref, oc_ref, cnew_b_ref, cnew_c_ref, h1_ref, s1_ref):
    def conv(x, st_ref, w_ref, new_ref):
        y = w_ref[CONV_W - 1:CONV_W, :] * x
        for i in range(CONV_W - 1):
            y = y + w_ref[i:i + 1, :] * st_ref[i:i + 1, :]
        for i in range(CONV_W - 2):
            new_ref[i:i + 1, :] = st_ref[i + 1:i + 2, :]
        new_ref[CONV_W - 2:CONV_W - 1, :] = x
        return y

    sm = sm_ref[...]
    dt_or_sp = _softplus(sm + dtb_s_ref[...])
    a_s = dt_or_sp * (-jnp.exp(alog_s_ref[...]))
    z = z_ref[...]

    xa = _silu(conv(xbc_ref[...], cst_b_ref, cwb_ref, cnew_b_ref) + cbb_ref[...])
    xs = xa[:, 0:D_B]
    dt_e = _expand_heads(dt_or_sp[:, DT_LANE:DT_LANE + H_B], H_B, P_B)
    a_e = _expand_heads(a_s[:, DT_LANE:DT_LANE + H_B], H_B, P_B)
    xc = xs * dt_e
    half = D_B // G_B
    lane_b = _iota((1, D_B), 1)
    bm = [xa[:, D_B + g * N_B:D_B + (g + 1) * N_B] for g in range(G_B)]
    cm = [xa[:, D_B + G_B * N_B + g * N_B:D_B + G_B * N_B + (g + 1) * N_B] for g in range(G_B)]
    h0 = h0_ref[...]
    coff = _dot_nt(_pad_rows(cm).astype(BF16), h0.astype(BF16))
    y_off = jnp.where(lane_b < half, coff[0:1, :], coff[1:2, :]) * jnp.exp(a_e)
    cb = [jnp.sum(cm[g] * bm[g], axis=-1, keepdims=True) for g in range(G_B)]
    y_diag = jnp.where(lane_b < half, cb[0], cb[1]) * xc
    y = (y_diag + y_off + xs * dskip_ref[...]) * _silu(z[:, D_A:D_A + D_B])
    y = jnp.concatenate([_rms(y[:, g * half:(g + 1) * half]) for g in range(G_B)], axis=-1)
    ob_ref[...] = y * ngb_ref[...]
    xc_rows = _pad_rows([jnp.where(lane_b < half, xc, 0.0), jnp.where(lane_b >= half, xc, 0.0)])
    outer = _dot_tn(xc_rows.astype(BF16), _pad_rows(bm).astype(BF16))
    dec_col = _dot_tn_hi(_pad_rows([jnp.exp(a_e)]), jnp.ones((SUBLANES, N_B), F32))
    h1_ref[...] = h0 * dec_col + outer

    qkv = _silu(conv(qkv_ref[...], cst_c_ref, cwc_ref, cnew_c_ref))
    beta_s = jax.nn.sigmoid(sm)
    outs = []
    for h in range(H_C):
        q = qkv[:, h * DK_C:(h + 1) * DK_C]
        k = qkv[:, D_C + h * DK_C:D_C + (h + 1) * DK_C]
        v = qkv[:, 2 * D_C + h * DV_C:2 * D_C + (h + 1) * DV_C]
        q = q * lax.rsqrt(jnp.sum(q * q, axis=-1, keepdims=True) + EPS) * (DK_C ** -0.5)
        k = k * lax.rsqrt(jnp.sum(k * k, axis=-1, keepdims=True) + EPS)
        beta = beta_s[:, BETA_LANE + h:BETA_LANE + h + 1]
        eg = jnp.exp(a_s[:, AG_LANE + h:AG_LANE + h + 1])
        s_old = s0_ref[h * DK_C:(h + 1) * DK_C, :]
        ws = _dot(_pad_rows([k * (beta * eg), q * eg]).astype(BF16), s_old.astype(BF16))
        v_new = v * beta - ws[0:1, :]
        o = ws[1:2, :] + jnp.sum(q * k, axis=-1, keepdims=True) * v_new
        upd = _dot_tn(_pad_rows([k]).astype(BF16), _pad_rows([v_new]).astype(BF16))
        eg_col = _dot_tn_hi(_pad_rows([jnp.broadcast_to(eg, (1, DK_C))]), jnp.ones((SUBLANES, DV_C), F32))
        s1_ref[h * DK_C:(h + 1) * DK_C, :] = s_old * eg_col + upd
        outs.append(_rms(o) * ngc_ref[...])
    oc_ref[...] = jnp.concatenate(outs, axis=-1) * _silu(z[:, D_A + D_B:D_MIX])


def _sample_step(xbc, qkv, sm, z, cst_b, cst_c, h0, s0, p):
    bs = xbc.shape[0]
    per_b = lambda a: pl.BlockSpec((None,) + tuple(a.shape[1:]), lambda i: (i,) + (0,) * (len(a.shape) - 1))
    full = lambda a: pl.BlockSpec(a.shape, lambda i: (0,) * a.ndim)
    data = [xbc, qkv, sm, z, cst_b, cst_c, h0, s0]
    params = [p["ssm_conv_w"], p["ssm_conv_b"], p["gdn_conv_w"], p["dtb_small"], p["alog_small"],
              p["ssm_d_e"], p["ssm_norm_g"], p["gdn_norm_g"]]
    outs = [jax.ShapeDtypeStruct((bs, 1, D_B), F32), jax.ShapeDtypeStruct((bs, 1, D_C), F32),
            jax.ShapeDtypeStruct(cst_b.shape, F32), jax.ShapeDtypeStruct(cst_c.shape, F32),
            jax.ShapeDtypeStruct(h0.shape, F32), jax.ShapeDtypeStruct(s0.shape, F32)]
    return pl.pallas_call(
        _sample_step_kernel,
        grid=(bs,),
        in_specs=[per_b(a) for a in data] + [full(a) for a in params],
        out_specs=[per_b(a) for a in outs],
        out_shape=outs,
        compiler_params=_cparams("arbitrary"),
    )(*data, *params)


def _permute_w_in(w):
    sizes = (2 * H_A * DH_A, 2 * H_A * DH_A, D_A, CONV_B, H_B, CONV_C, H_C, H_C, D_MIX)
    offs = [0]
    for s in sizes:
        offs.append(offs[-1] + s)
    col = lambda i: w[:, offs[i]:offs[i + 1]]
    small = jnp.concatenate([col(4), col(6), col(7),
                             jnp.zeros((w.shape[0], SMALL_W - H_B - 2 * H_C), w.dtype)], axis=1)
    return jnp.concatenate([col(0), col(1), col(2), col(3), col(5), col(8), small], axis=1).astype(BF16)


def _layer_params(l, w):
    pad_small = lambda a_ssm, a_gdn: jnp.concatenate(
        [a_ssm, jnp.zeros((H_C,), F32), a_gdn, jnp.zeros((SMALL_W - H_B - 2 * H_C,), F32)]).reshape(1, SMALL_W)
    rep = lambda a: jnp.repeat(a, P_B).reshape(1, D_B)
    expand = (jnp.arange(SMALL_W)[:, None] == (jnp.arange(D_B)[None, :] // P_B)).astype(BF16)
    return {
        "ssm_conv_w": w["ssm_conv_w"][l], "ssm_conv_b": w["ssm_conv_b"][l].reshape(1, CONV_B),
        "gdn_conv_w": w["gdn_conv_w"][l],
        "dtb_small": pad_small(w["ssm_dt_bias"][l], w["gdn_dt_bias"][l]),
        "alog_small": pad_small(w["ssm_A_log"][l], w["gdn_A_log"][l]),
        "ssm_dtb_e": rep(w["ssm_dt_bias"][l]), "ssm_alog_e": rep(w["ssm_A_log"][l]),
        "ssm_d_e": rep(w["ssm_D"][l]),
        "ssm_norm_g": w["ssm_norm_g"][l].reshape(1, D_B), "gdn_norm_g": w["gdn_norm_g"][l].reshape(1, DV_C),
        "expand": expand,
    }


def _pick(n, prefs):
    for t in prefs:
        if n % t == 0:
            return t
    return n


def kernel(x_prompt, x_sample, c_prompt, c_sample, cache_k, cache_v, state_ssm, state_ssm_conv, state_gdn, state_gdn_conv, page_table, w_ada, b_ada, w_in, w_out, rel_bias, attn_lambda, attn_subln_g, ssm_conv_w, ssm_conv_b, ssm_dt_bias, ssm_A_log, ssm_D, ssm_norm_g, gdn_conv_w, gdn_dt_bias, gdn_A_log, gdn_norm_g, final_norm_g):
    depth = w_in.shape[0]
    bp, L, _ = x_prompt.shape
    bs = x_sample.shape[0]
    n_pages = page_table.shape[1]
    past = n_pages * PAGE_SIZE
    n_pool = cache_k.shape[1]
    wts = dict(ssm_conv_w=ssm_conv_w, ssm_conv_b=ssm_conv_b, ssm_dt_bias=ssm_dt_bias, ssm_A_log=ssm_A_log,
               ssm_D=ssm_D, ssm_norm_g=ssm_norm_g, gdn_conv_w=gdn_conv_w, gdn_dt_bias=gdn_dt_bias,
               gdn_A_log=gdn_A_log, gdn_norm_g=gdn_norm_g)

    tm = _pick(L, (256, 128, 64))
    T = _pick(L, (256, 128))
    Q = _pick(L, (128, 64))
    R = _pick(L, (256, 128, 64))
    G = _pick(n_pages, (8, 4, 2, 1))

    mod = _modulation(jnp.concatenate([c_prompt, c_sample], axis=0), w_ada, b_ada)
    bias_near = _near_bias_tiles(rel_bias, T)
    bias_pages = _decode_bias(rel_bias, past)
    bias_new = jnp.broadcast_to(jnp.tile(rel_bias[0].astype(F32), 2)[:, None], (2 * H_A, DV_A))
    ck = cache_k.reshape(depth, n_pool, PAGE_SIZE * H_A, 2 * DH_A)
    cv = cache_v.reshape(depth, n_pool, PAGE_SIZE * H_A, DV_A)
    lane_half = jnp.arange(DV_A) < DH_A

    xp = x_prompt
    xs = x_sample.reshape(1, bs, D_MODEL)
    new_p, new_s = [], []
    for l in range(depth):
        lam_init = 0.8 - 0.6 * math.exp(-0.3 * l)
        p = _layer_params(l, wts)
        w_perm = _permute_w_in(w_in[l])
        w_o = w_out[l].astype(BF16)
        shift, scale, gate = jnp.split(mod[l], 3, axis=-1)
        mp = lambda a: a[:bp].reshape(bp, 1, D_MODEL)
        ms = lambda a: a[bp:].reshape(1, bs, D_MODEL)

        q, k, v, xbc, qkv, z, sm = _in_projection(xp, mp(scale), mp(shift), w_perm, tm)
        o_a = _prompt_attention(q, k, v, z, bias_near, attn_lambda[l], attn_subln_g[l], lam_init, T)
        o_b, ssm_h = _prompt_ssd(xbc, sm, z, p, Q)
        o_c, gdn_s = _prompt_gdn(qkv, sm, z, p, R)
        final = final_norm_g if l == depth - 1 else None
        xp = _out_projection(xp, mp(gate), o_a, o_b, o_c, w_o, final, tm)
        new_p.append((k.reshape(bp, L, H_A, 2 * DH_A), v.reshape(bp, L, H_A, DV_A),
                      ssm_h.reshape(bp, H_B, P_B, N_B), xbc[:, L - (CONV_W - 1):, :],
                      gdn_s.reshape(bp, H_C, DK_C, DV_C), qkv[:, L - (CONV_W - 1):, :]))

        q, k, v, xbc, qkv, z, sm = _in_projection(xs, ms(scale), ms(shift), w_perm, bs)
        q4 = q.reshape(bs, H_A, DV_A) * (DH_A ** -0.5)
        qpat = jnp.concatenate([jnp.where(lane_half, q4, 0.0), jnp.where(lane_half, 0.0, q4)], axis=1)
        k4 = k.reshape(bs, H_A, DV_A)
        v4 = v.reshape(bs, H_A, DV_A)
        o_a = _decode_attention(l, page_table, ck, cv, qpat, jnp.concatenate([k4, k4], axis=1),
                                jnp.concatenate([v4, v4], axis=1), bias_pages, bias_new,
                                z[..., :D_A].reshape(bs, H_A, DV_A), attn_lambda[l], attn_subln_g[l],
                                lam_init, G)
        row = lambda a: a.reshape(bs, 1, a.shape[-1])
        o_b, o_c, conv_b1, conv_c1, ssm_h, gdn_s = _sample_step(
            row(xbc), row(qkv), row(sm), row(z), state_ssm_conv[l], state_gdn_conv[l],
            state_ssm[l].reshape(bs, D_B, N_B), state_gdn[l].reshape(bs, H_C * DK_C, DV_C), p)
        xs = _out_projection(xs, ms(gate), o_a.reshape(1, bs, D_A).astype(BF16),
                             o_b.reshape(1, bs, D_B).astype(BF16), o_c.reshape(1, bs, D_C).astype(BF16),
                             w_o, final, bs)
        new_s.append((k.reshape(bs, 1, H_A, 2 * DH_A), v.reshape(bs, 1, H_A, DV_A),
                      ssm_h.reshape(bs, H_B, P_B, N_B), conv_b1,
                      gdn_s.reshape(bs, H_C, DK_C, DV_C), conv_c1))

    k_p, v_p, ssm_p, ssm_conv_p, gdn_p, gdn_conv_p = [jnp.stack(a) for a in zip(*new_p)]
    k_s, v_s, ssm_s, ssm_conv_s, gdn_s, gdn_conv_s = [jnp.stack(a) for a in zip(*new_s)]
    return (xp, xs.reshape(bs, 1, D_MODEL), k_p, v_p, ssm_p, ssm_conv_p, gdn_p, gdn_conv_p,
            k_s, v_s, ssm_s, ssm_conv_s, gdn_s, gdn_conv_s)
```

```python
import functools
import math

import jax
import jax.numpy as jnp
from jax import lax
from jax.experimental import pallas as pl
from jax.experimental.pallas import tpu as pltpu

F32 = jnp.float32
BF16 = jnp.bfloat16
HIGHEST = lax.Precision.HIGHEST

D_MODEL = 1024
H_A, DH_A, DV_A = 4, 64, 128
D_A = H_A * DV_A
H_B, P_B, G_B, N_B = 8, 64, 2, 128
D_B = H_B * P_B
CONV_B = D_B + 2 * G_B * N_B
H_C, DK_C, DV_C = 4, 128, 128
D_C = H_C * DV_C
CONV_C = 2 * H_C * DK_C + D_C
CONV_W = 4
D_MIX = D_A + D_B + D_C
REL_BUCKETS, REL_MAX_DIST = 32, 128
PAGE_SIZE = 128
EPS = 1e-6
GDN_CHUNK = 64

LANES = 128
SUBLANES = 8
SMALL_W = LANES
DT_LANE, BETA_LANE, AG_LANE = 0, H_B, H_B + H_C
NEG_BIG = -1e30
LOG2E = math.log2(math.e)
ATTN_Q_SCALE = DH_A ** -0.5 * LOG2E
ATTN_STRIP = 64
VMEM_LIMIT = 56 * 1024 * 1024

SEGMENTS = (("q", 2 * H_A * DH_A), ("k", 2 * H_A * DH_A), ("v", D_A), ("xbc", CONV_B),
            ("qkv", CONV_C), ("z", D_MIX), ("small", SMALL_W))


def _cparams(*sem):
    return pltpu.CompilerParams(dimension_semantics=sem, vmem_limit_bytes=VMEM_LIMIT)


def _silu(x):
    return x * jax.nn.sigmoid(x)


def _softplus(x):
    return jnp.maximum(x, 0.0) + jnp.log1p(jnp.exp(-jnp.abs(x)))


def _rms(x):
    return x * lax.rsqrt(jnp.mean(x * x, axis=-1, keepdims=True) + EPS)


def _dot_nt(a, b):
    return lax.dot_general(a, b, (((1,), (1,)), ((), ())), preferred_element_type=F32)


def _dot_tn(a, b):
    return lax.dot_general(a, b, (((0,), (0,)), ((), ())), preferred_element_type=F32)


def _dot_tn_hi(a, b):
    return lax.dot_general(a, b, (((0,), (0,)), ((), ())), preferred_element_type=F32, precision=HIGHEST)


def _dot(a, b):
    return jnp.dot(a, b, preferred_element_type=F32)


def _iota(shape, dim):
    return lax.broadcasted_iota(jnp.int32, shape, dim)


def _mod_kernel(c_ref, w_ref, b_ref, o_ref):
    sc = _silu(c_ref[...]).astype(BF16)
    o_ref[...] = _dot(sc, w_ref[...].astype(BF16)) + b_ref[...]


def _modulation(c_all, w_ada, b_ada):
    depth = w_ada.shape[0]
    n = c_all.shape[0]
    return pl.pallas_call(
        _mod_kernel,
        grid=(depth, 3),
        in_specs=[pl.BlockSpec((n, D_MODEL), lambda l, j: (0, 0)),
                  pl.BlockSpec((None, D_MODEL, D_MODEL), lambda l, j: (l, 0, j)),
                  pl.BlockSpec((None, 1, D_MODEL), lambda l, j: (l, 0, j))],
        out_specs=pl.BlockSpec((None, n, D_MODEL), lambda l, j: (l, 0, j)),
        out_shape=jax.ShapeDtypeStruct((depth, n, 3 * D_MODEL), F32),
        compiler_params=_cparams("arbitrary", "arbitrary"),
    )(c_all, w_ada, b_ada.reshape(depth, 1, 3 * D_MODEL))


def _inproj_kernel(x_ref, scale_ref, shift_ref, w_ref, *out_refs, heads_out):
    h = _rms(x_ref[...]) * (1.0 + scale_ref[...]) + shift_ref[...]
    hb = h.astype(BF16)
    seg_refs = out_refs[:len(SEGMENTS)]
    head_refs = dict(zip(("k", "v"), out_refs[len(SEGMENTS):])) if heads_out else {}
    off = 0
    for (name, n), ref in zip(SEGMENTS, seg_refs):
        res = _dot(hb, w_ref[:, off:off + n])
        scaled = res * ATTN_Q_SCALE if heads_out and name == "q" else res
        ref[...] = scaled.astype(ref.dtype)
        if name in head_refs:
            for hd in range(H_A):
                head_refs[name][:, hd, :] = res[:, hd * DV_A:(hd + 1) * DV_A]
        off += n


def _in_projection(x, scale, shift, w_perm, tm, heads_out):
    b, L, _ = x.shape
    ms = scale.shape[1]
    mod_rows = tm if ms == L else 1
    mod_map = (lambda i, j: (i, j, 0)) if ms == L else (lambda i, j: (i, 0, 0))
    n_tot = w_perm.shape[1]
    dt = lambda name: BF16 if heads_out and name in ("q", "k", "v") else F32
    out_specs = [pl.BlockSpec((None, tm, w), lambda i, j: (i, j, 0)) for _, w in SEGMENTS]
    out_shape = [jax.ShapeDtypeStruct((b, L, w), dt(name)) for name, w in SEGMENTS]
    if heads_out:
        out_specs += [pl.BlockSpec((None, tm, H_A, DV_A), lambda i, j: (i, j, 0, 0))] * 2
        out_shape += [jax.ShapeDtypeStruct((b, L, H_A, DV_A), F32)] * 2
    return pl.pallas_call(
        functools.partial(_inproj_kernel, heads_out=heads_out),
        grid=(b, L // tm),
        in_specs=[pl.BlockSpec((None, tm, D_MODEL), lambda i, j: (i, j, 0)),
                  pl.BlockSpec((None, mod_rows, D_MODEL), mod_map),
                  pl.BlockSpec((None, mod_rows, D_MODEL), mod_map),
                  pl.BlockSpec((D_MODEL, n_tot), lambda i, j: (0, 0))],
        out_specs=out_specs,
        out_shape=out_shape,
        compiler_params=_cparams("arbitrary", "arbitrary"),
    )(x, scale, shift, w_perm)


def _outproj_kernel(x_ref, gate_ref, oa_ref, ob_ref, oc_ref, w_ref, *rest, final):
    acc = _dot(oa_ref[...], w_ref[0:D_A, :])
    acc += _dot(ob_ref[...], w_ref[D_A:D_A + D_B, :])
    acc += _dot(oc_ref[...], w_ref[D_A + D_B:D_MIX, :])
    y = x_ref[...] + gate_ref[...] * acc
    if final:
        g_ref, o_ref = rest
        o_ref[...] = _rms(y) * g_ref[...]
    else:
        (o_ref,) = rest
        o_ref[...] = y


def _out_projection(x, gate, oa, ob, oc, w_out, final_g, tm):
    b, L, _ = x.shape
    ms = gate.shape[1]
    mod_rows = tm if ms == L else 1
    mod_map = (lambda i, j: (i, j, 0)) if ms == L else (lambda i, j: (i, 0, 0))
    row = lambda w: pl.BlockSpec((None, tm, w), lambda i, j: (i, j, 0))
    in_specs = [row(D_MODEL), pl.BlockSpec((None, mod_rows, D_MODEL), mod_map),
                row(D_A), row(D_B), row(D_C),
                pl.BlockSpec((D_MIX, D_MODEL), lambda i, j: (0, 0))]
    args = [x, gate, oa, ob, oc, w_out]
    final = final_g is not None
    if final:
        in_specs.append(pl.BlockSpec((1, D_MODEL), lambda i, j: (0, 0)))
        args.append(final_g.reshape(1, D_MODEL))
    return pl.pallas_call(
        functools.partial(_outproj_kernel, final=final),
        grid=(b, L // tm),
        in_specs=in_specs,
        out_specs=row(D_MODEL),
        out_shape=jax.ShapeDtypeStruct((b, L, D_MODEL), F32),
        compiler_params=_cparams("arbitrary", "arbitrary"),
    )(*args)


def _lambda_value(lp, lam_init):
    s01 = jnp.sum(lp[0:1, :] * lp[1:2, :], axis=-1, keepdims=True)
    s23 = jnp.sum(lp[2:3, :] * lp[3:4, :], axis=-1, keepdims=True)
    return jnp.exp(s01) - jnp.exp(s23) + lam_init


def _rel_bias_values(table, n):
    exact = REL_BUCKETS // 2
    large = exact + (jnp.log(jnp.maximum(n, 1).astype(F32) / exact)
                     / math.log(REL_MAX_DIST / exact) * (REL_BUCKETS - exact)).astype(jnp.int32)
    bucket = jnp.where(n < exact, n, jnp.minimum(large, REL_BUCKETS - 1))
    onehot = bucket[..., None, None] == jnp.arange(REL_BUCKETS)[:, None]
    return jnp.sum(jnp.where(onehot, table.astype(F32), 0.0), axis=-2)


def _attn_kernel(q_ref, kb_scr, vb_scr, z_ref, bias_ref, lamp_ref, subln_ref, o_ref, s_bufs, w_bufs,
                 *, T, lam_init):
    L = q_ref.shape[0]
    nq = L // T
    lam = _lambda_value(lamp_ref[...], lam_init)
    lane = _iota((T, DV_A), 1)
    half = LANES

    def logit_tasks(i):
        q = q_ref[i * T:(i + 1) * T, :]
        s_scr = s_bufs.at[i % 2]
        tasks = []
        for sub in range(2):
            qm = jnp.where((lane < DH_A) if sub == 0 else (lane >= DH_A), q, jnp.zeros_like(q))
            for j in range(i + 1):
                def task(qm=qm, sub=sub, j=j):
                    s = _dot_nt(qm, kb_scr[j * T:(j + 1) * T, :])
                    if j == i:
                        s = s + bias_ref[:, T:2 * T]
                    elif j == i - 1:
                        s = s + bias_ref[:, 0:T]
                    s_scr[sub, j] = s
                tasks.append(task)
        return tasks

    def softmax_strip(i, r0):
        s_scr, w_scr = s_bufs.at[i % 2], w_bufs.at[i % 2]
        strip = slice(r0, r0 + ATTN_STRIP)
        scale = []
        for sub in range(2):
            mx = None
            for j in range(i + 1):
                s = s_scr[sub, j, strip, :]
                for c0 in range(0, T, half):
                    t = s[:, c0:c0 + half]
                    mx = t if mx is None else jnp.maximum(mx, t)
            m_row = jnp.max(mx, axis=-1, keepdims=True)
            ls = None
            for j in range(i + 1):
                p = jnp.exp2(s_scr[sub, j, strip, :] - m_row)
                s_scr[sub, j, strip, :] = p
                for c0 in range(0, T, half):
                    t = p[:, c0:c0 + half]
                    ls = t if ls is None else ls + t
            scale.append(1.0 / jnp.sum(ls, axis=-1, keepdims=True))
        w0, w1 = scale[0], lam * scale[1]
        for j in range(i + 1):
            w_scr[strip, j * T:(j + 1) * T] = (s_scr[0, j, strip, :] * w0
                                               - s_scr[1, j, strip, :] * w1).astype(BF16)

    for task in logit_tasks(0):
        task()
    for i in range(nq):
        nxt = logit_tasks(i + 1) if i + 1 < nq else []
        strips = list(range(0, T, ATTN_STRIP))
        per = -(-len(nxt) // len(strips))
        for n, r0 in enumerate(strips):
            softmax_strip(i, r0)
            for task in nxt[n * per:(n + 1) * per]:
                task()
        rows = slice(i * T, (i + 1) * T)
        o = _dot(w_bufs[i % 2, :, 0:(i + 1) * T], vb_scr[0:(i + 1) * T, :])
        o = _rms(o) * subln_ref[...] * (1.0 - lam_init) * _silu(z_ref[rows, :])
        o_ref[rows, :] = o.astype(o_ref.dtype)


def _prompt_attention(q, k, v, z, bias_near, lam_p, subln, lam_init, T):
    b, L, _ = q.shape
    seq = lambda: pl.BlockSpec((None, L, DV_A), lambda i, h: (i, 0, h))
    return pl.pallas_call(
        functools.partial(_attn_kernel, T=T, lam_init=lam_init),
        grid=(b, H_A),
        in_specs=[seq(), seq(), seq(), seq(),
                  pl.BlockSpec((None, T, 2 * T), lambda i, h: (h, 0, 0)),
                  pl.BlockSpec((4, DH_A), lambda i, h: (0, 0)),
                  pl.BlockSpec((1, DV_A), lambda i, h: (0, 0))],
        out_specs=seq(),
        out_shape=jax.ShapeDtypeStruct((b, L, D_A), BF16),
        scratch_shapes=[pltpu.VMEM((2, 2, L // T, T, T), F32), pltpu.VMEM((2, T, L), BF16)],
        compiler_params=_cparams("arbitrary", "arbitrary"),
    )(q, k, v, z, bias_near, lam_p, subln.reshape(1, DV_A))


def _near_bias_tiles(table, T):
    n = T + jnp.arange(T)[:, None] - jnp.arange(2 * T)[None, :]
    vals = _rel_bias_values(table, jnp.maximum(n, 0)) - table[REL_BUCKETS - 1].astype(F32)
    vals = jnp.where((n >= 0)[..., None], vals * LOG2E, NEG_BIG)
    return jnp.moveaxis(vals, -1, 0)


def _decode_kernel(pt_ref, qpat_ref, knew_ref, vnew_ref, bias_ref, biasnew_ref, z_ref, lamp_ref,
                   subln_ref, *rest, G, lam_init):
    del pt_ref
    k_refs, v_refs = rest[:G], rest[G:2 * G]
    o_ref, m_scr, l_scr, acc_scr = rest[2 * G:]
    j = pl.program_id(1)

    @pl.when(j == 0)
    def _():
        m_scr[...] = jnp.full(m_scr.shape, NEG_BIG, F32)
        l_scr[...] = jnp.zeros(l_scr.shape, F32)
        acc_scr[...] = jnp.zeros(acc_scr.shape, F32)

    qf = qpat_ref[...]
    qb = qf.astype(BF16)
    s = [_dot_nt(qb, k_refs[g][...].astype(BF16)) + bias_ref[g] for g in range(G)]
    m_old = m_scr[...]
    m_tile = s[0]
    for g in range(1, G):
        m_tile = jnp.maximum(m_tile, s[g])
    m = jnp.maximum(m_old, jnp.max(m_tile, axis=-1, keepdims=True))
    alpha = jnp.exp(m_old - m)
    p = [jnp.exp(s[g] - m) for g in range(G)]
    p_sum = p[0]
    for g in range(1, G):
        p_sum = p_sum + p[g]
    l = alpha * l_scr[...] + jnp.sum(p_sum, axis=-1, keepdims=True)
    pv = _dot(p[0].astype(BF16), v_refs[0][...].astype(BF16))
    for g in range(1, G):
        pv = pv + _dot(p[g].astype(BF16), v_refs[g][...].astype(BF16))
    acc = alpha * acc_scr[...] + pv
    m_scr[...] = m
    l_scr[...] = l
    acc_scr[...] = acc

    @pl.when(j == pl.num_programs(1) - 1)
    def _():
        s_new = jnp.sum(qf * knew_ref[...], axis=-1, keepdims=True) + biasnew_ref[:, 0:1]
        m_new = jnp.maximum(m, s_new)
        alpha = jnp.exp(m - m_new)
        p_new = jnp.exp(s_new - m_new)
        l_fin = alpha * l + p_new
        out = (alpha * acc + p_new * vnew_ref[...]) / l_fin
        lam = _lambda_value(lamp_ref[...], lam_init)
        o = out[0:H_A, :] - lam * out[H_A:2 * H_A, :]
        o_ref[...] = _rms(o) * subln_ref[...] * (1.0 - lam_init) * _silu(z_ref[...])


def _decode_attention(layer, page_table, cache_k, cache_v, qpat, knew, vnew, bias_pages, bias_new, z3,
                      lam_p, subln, lam_init, G):
    bs, n_pages = page_table.shape
    rows = PAGE_SIZE * H_A

    def page_spec(g):
        return pl.BlockSpec((None, None, rows, DV_A),
                            lambda i, j, pt: (layer, pt[i, j * G + g], 0, 0))

    per_b = lambda r: pl.BlockSpec((None, r, DV_A), lambda i, j, pt: (i, 0, 0))
    grid_spec = pltpu.PrefetchScalarGridSpec(
        num_scalar_prefetch=1,
        grid=(bs, n_pages // G),
        in_specs=[per_b(2 * H_A), per_b(2 * H_A), per_b(2 * H_A),
                  pl.BlockSpec((G, 2 * H_A, rows), lambda i, j, pt: (j, 0, 0)),
                  pl.BlockSpec((2 * H_A, DV_A), lambda i, j, pt: (0, 0)),
                  per_b(H_A),
                  pl.BlockSpec((4, DH_A), lambda i, j, pt: (0, 0)),
                  pl.BlockSpec((1, DV_A), lambda i, j, pt: (0, 0))]
                 + [page_spec(g) for g in range(G)] + [page_spec(g) for g in range(G)],
        out_specs=per_b(H_A),
        scratch_shapes=[pltpu.VMEM((2 * H_A, 1), F32), pltpu.VMEM((2 * H_A, 1), F32),
                        pltpu.VMEM((2 * H_A, DV_A), F32)],
    )
    return pl.pallas_call(
        functools.partial(_decode_kernel, G=G, lam_init=lam_init),
        grid_spec=grid_spec,
        out_shape=jax.ShapeDtypeStruct((bs, H_A, DV_A), F32),
        compiler_params=_cparams("arbitrary", "arbitrary"),
    )(page_table, qpat, knew, vnew, bias_pages, bias_new, z3, lam_p, subln.reshape(1, DV_A),
      *([cache_k] * G), *([cache_v] * G))


def _decode_bias(table, past):
    n = past - jnp.arange(past)
    vals = _rel_bias_values(table, n).reshape(past // PAGE_SIZE, PAGE_SIZE, H_A)
    vals = jnp.moveaxis(vals, -1, 1)[..., None]
    same = (jnp.arange(H_A)[:, None, None] == jnp.arange(H_A)[None, None, :])
    full = jnp.where(same[None], vals, NEG_BIG).reshape(past // PAGE_SIZE, H_A, PAGE_SIZE * H_A)
    return jnp.concatenate([full, full], axis=1)


def _conv_taps(xp_ref, x, w_ref, rows):
    xp_ref[SUBLANES:SUBLANES + rows, :] = x
    y = w_ref[CONV_W - 1:CONV_W, :] * x
    for i in range(CONV_W - 1):
        y = y + w_ref[i:i + 1, :] * xp_ref[pl.ds(SUBLANES - (CONV_W - 1) + i, rows), :]
    xp_ref[0:SUBLANES, :] = x[rows - SUBLANES:rows, :]
    return y


def _ssd_kernel(xbc_ref, sm_ref, z_ref, cw_ref, cb_ref, dtb_s_ref, alog_s_ref, dtb_e_ref, alog_e_ref,
                dskip_ref, ng_ref, exp_ref, y_ref, hout_ref, xp_scr, h_scr, *, Q):
    c = pl.program_id(1)

    @pl.when(c == 0)
    def _():
        xp_scr[0:SUBLANES, :] = jnp.zeros((SUBLANES, CONV_B), F32)
        h_scr[...] = jnp.zeros(h_scr.shape, F32)

    xa = _silu(_conv_taps(xp_scr, xbc_ref[...], cw_ref, Q) + cb_ref[...])
    xs = xa[:, 0:D_B]
    sm = sm_ref[...]
    row = _iota((Q, Q), 0)
    col = _iota((Q, Q), 1)
    tril = row >= col
    expand = exp_ref[...]

    a_s = _softplus(sm + dtb_s_ref[...]) * (-jnp.exp(alog_s_ref[...]))
    cum_s = _dot_exact_rhs(tril.astype(BF16), a_s)
    cum_t = cum_s.T
    dt_e = _softplus(_dot_exact_lhs(sm, expand) + dtb_e_ref[...])
    cum_e = _dot_exact_lhs(cum_s, expand)
    xc = xs * dt_e
    dec_end = jnp.exp(cum_e[Q - 1:Q, :] - cum_e)
    dec_in = jnp.exp(cum_e)
    xc_b = xc.astype(BF16)
    xe_b = (xc * dec_end).astype(BF16)
    lane = _iota((Q, LANES), 1)
    prow = _iota((LANES, LANES), 0)
    heads_per_group = H_B // G_B
    pairs = []
    for g in range(G_B):
        bm = xa[:, D_B + g * N_B:D_B + (g + 1) * N_B].astype(BF16)
        cm = xa[:, D_B + G_B * N_B + g * N_B:D_B + G_B * N_B + (g + 1) * N_B].astype(BF16)
        cb = _dot_nt(cm, bm)
        for pr in range(g * heads_per_group // 2, (g + 1) * heads_per_group // 2):
            lo, hi = pr * LANES, (pr + 1) * LANES
            yd = []
            for hh in (2 * pr, 2 * pr + 1):
                seg = cum_s[:, hh:hh + 1] - cum_t[hh:hh + 1, :]
                mat = cb * jnp.exp(jnp.where(tril, seg, -jnp.inf))
                yd.append(_dot(mat.astype(BF16), xc_b[:, lo:hi]))
            y_diag = jnp.where(lane < P_B, yd[0], yd[1])
            hp = h_scr[lo:hi, :]
            y_off = _dot_nt(cm, hp.astype(BF16)) * dec_in[:, lo:hi]
            st = _dot_tn(xe_b[:, lo:hi], bm)
            a_last = jnp.where(prow < P_B, cum_t[2 * pr:2 * pr + 1, Q - 1:Q],
                               cum_t[2 * pr + 1:2 * pr + 2, Q - 1:Q])
            h_scr[lo:hi, :] = hp * jnp.exp(a_last) + st
            pairs.append(y_diag + y_off + xs[:, lo:hi] * dskip_ref[:, lo:hi])
    y = jnp.concatenate(pairs, axis=-1) * _silu(z_ref[...])
    gw = D_B // G_B
    y = jnp.concatenate([_rms(y[:, g * gw:(g + 1) * gw]) for g in range(G_B)], axis=-1) * ng_ref[...]
    y_ref[...] = y.astype(y_ref.dtype)

    @pl.when(c == pl.num_programs(1) - 1)
    def _():
        hout_ref[...] = h_scr[...]


def _prompt_ssd(xbc, sm, z, p, Q):
    b, L, _ = xbc.shape
    full = lambda a: pl.BlockSpec(a.shape, lambda i, j: (0,) * a.ndim)
    params = [p["ssm_conv_w"], p["ssm_conv_b"], p["dtb_small"], p["alog_small"], p["ssm_dtb_e"],
              p["ssm_alog_e"], p["ssm_d_e"], p["ssm_norm_g"], p["expand"]]
    return pl.pallas_call(
        functools.partial(_ssd_kernel, Q=Q),
        grid=(b, L // Q),
        in_specs=[pl.BlockSpec((None, Q, CONV_B), lambda i, j: (i, j, 0)),
                  pl.BlockSpec((None, Q, SMALL_W), lambda i, j: (i, j, 0)),
                  pl.BlockSpec((None, Q, D_B), lambda i, j: (i, j, D_A // D_B))]
                 + [full(a) for a in params],
        out_specs=[pl.BlockSpec((None, Q, D_B), lambda i, j: (i, j, 0)),
                   pl.BlockSpec((None, D_B, N_B), lambda i, j: (i, 0, 0))],
        out_shape=[jax.ShapeDtypeStruct((b, L, D_B), BF16),
                   jax.ShapeDtypeStruct((b, D_B, N_B), F32)],
        scratch_shapes=[pltpu.VMEM((Q + SUBLANES, CONV_B), F32), pltpu.VMEM((D_B, N_B), F32)],
        compiler_params=_cparams("arbitrary", "arbitrary"),
    )(xbc, sm, z, *params)


def _split2(a):
    hi = a.astype(BF16)
    return hi, (a - hi.astype(F32)).astype(BF16)


def _split3(x):
    x1 = x.astype(BF16)
    r1 = x - x1.astype(F32)
    x2 = r1.astype(BF16)
    return x1, x2, (r1 - x2.astype(F32)).astype(BF16)


def _dot_exact_rhs(a_bf16, x):
    x1, x2, x3 = _split3(x)
    return _dot(a_bf16, x1) + (_dot(a_bf16, x2) + _dot(a_bf16, x3))


def _dot_exact_lhs(x, a_bf16):
    x1, x2, x3 = _split3(x)
    return _dot(x1, a_bf16) + (_dot(x2, a_bf16) + _dot(x3, a_bf16))


def _gdn_head_prepare(act, gcum, gcum_t, beta_s, r0, h):
    cs = GDN_CHUNK
    rows = slice(r0, r0 + cs)
    q = act[rows, h * DK_C:(h + 1) * DK_C]
    k = act[rows, D_C + h * DK_C:D_C + (h + 1) * DK_C]
    v = act[rows, 2 * D_C + h * DV_C:2 * D_C + (h + 1) * DV_C]
    q = q * lax.rsqrt(jnp.sum(q * q, axis=-1, keepdims=True) + EPS) * (DK_C ** -0.5)
    k = k * lax.rsqrt(jnp.sum(k * k, axis=-1, keepdims=True) + EPS)
    beta = beta_s[rows, BETA_LANE + h:BETA_LANE + h + 1]
    gc = gcum[rows, AG_LANE + h:AG_LANE + h + 1]
    gr = gcum_t[AG_LANE + h:AG_LANE + h + 1, rows]
    g_last = gcum[r0 + cs - 1:r0 + cs, AG_LANE + h:AG_LANE + h + 1]
    ri = _iota((cs, 2 * cs), 0)
    ci = _iota((cs, 2 * cs), 1) & (cs - 1)
    decay = jnp.exp(jnp.where(ri >= ci, gc - jnp.concatenate([gr, gr], axis=1), -jnp.inf))
    kb = k.astype(BF16)
    kk = _dot_nt(kb, jnp.concatenate([kb, kb], axis=0))
    p = jnp.where(ri > ci, -(beta * kk * decay), 0.0)
    rhs = jnp.concatenate([v * beta, k * (beta * jnp.exp(gc))], axis=-1)
    attn = (_dot_nt(q.astype(BF16), kb) * decay[:, 0:cs]).astype(BF16)
    qg = q * jnp.exp(gc)
    kd = (k * jnp.exp(g_last - gc)).astype(BF16)
    return p, rhs, attn, qg, kd, jnp.exp(g_last)


def _gdn_head_finish(sol, attn, qg, kd, e_last):
    x_b = sol.astype(BF16)
    ax = _dot(attn, x_b)
    kx = _dot_tn(kd, x_b)
    lhs = jnp.concatenate([kx[:, DV_C:2 * DV_C], qg - ax[:, DV_C:2 * DV_C]], axis=0).astype(BF16)
    return lhs, kx[:, 0:DV_C], ax[:, 0:DV_C], e_last


def _split_lhs(p):
    hi, lo = _split2(p)
    half = jnp.where(_iota(p.shape, 1) < GDN_CHUNK, hi, lo)
    return jnp.concatenate([half, half], axis=1)


def _split_rhs(x):
    hi, lo = _split2(x)
    return jnp.concatenate([hi, hi, lo, lo], axis=0)


def _neumann_solve(ps, xs):
    n_levels = GDN_CHUNK.bit_length() - 1
    for lvl in range(n_levels):
        lhs = [_split_lhs(p) for p in ps]
        xs = [x + _dot(a, _split_rhs(x)) for a, x in zip(lhs, xs)]
        if lvl + 1 < n_levels:
            ps = [_dot(a, _split_rhs(p)) for a, p in zip(lhs, ps)]
    return xs


def _gdn_kernel(qkv_ref, sm_ref, z_ref, cw_ref, dtb_s_ref, alog_s_ref, ng_ref, o_ref, sout_ref,
                xp_scr, s_scr, act_scr, *, R):
    c = pl.program_id(1)

    @pl.when(c == 0)
    def _():
        xp_scr[0:SUBLANES, :] = jnp.zeros((SUBLANES, CONV_C), F32)
        s_scr[...] = jnp.zeros(s_scr.shape, F32)

    cs = GDN_CHUNK
    act_scr[...] = _silu(_conv_taps(xp_scr, qkv_ref[...], cw_ref, R))
    sm = sm_ref[...]
    g_s = _softplus(sm + dtb_s_ref[...]) * (-jnp.exp(alog_s_ref[...]))
    beta_s = jax.nn.sigmoid(sm)
    ri = _iota((R, R), 0)
    ci = _iota((R, R), 1)
    chunk_tril = (lax.shift_right_logical(ri, 6) == lax.shift_right_logical(ci, 6)) & (ri >= ci)
    gcum = _dot_exact_rhs(chunk_tril.astype(BF16), g_s)
    gcum_t = gcum.T

    n_chunks = R // cs
    def state_free_part(chunks):
        prep = [_gdn_head_prepare(act_scr, gcum, gcum_t, beta_s, ch * cs, h) for ch in chunks for h in range(H_C)]
        sols = _neumann_solve([pr[0] for pr in prep], [pr[1] for pr in prep])
        return [_gdn_head_finish(sol, *pr[2:]) for sol, pr in zip(sols, prep)]

    def state_step(ch, heads):
        outs = []
        for h, (lhs, c_add, o_add, e_last) in enumerate(heads):
            s_old = s_scr[h * DK_C:(h + 1) * DK_C, :]
            r = _dot(lhs, s_old.astype(BF16))
            s_scr[h * DK_C:(h + 1) * DK_C, :] = s_old * e_last + (c_add - r[0:DK_C, :])
            outs.append(_rms(r[DK_C:DK_C + cs, :] + o_add) * ng_ref[...])
        rows = slice(ch * cs, (ch + 1) * cs)
        o_ref[rows, :] = (jnp.concatenate(outs, axis=-1) * _silu(z_ref[rows, :])).astype(o_ref.dtype)

    half = max(n_chunks // 2, 1)
    for chunks in (range(0, half), range(half, n_chunks)):
        parts = state_free_part(chunks)
        for n, ch in enumerate(chunks):
            state_step(ch, parts[n * H_C:(n + 1) * H_C])

    @pl.when(c == pl.num_programs(1) - 1)
    def _():
        sout_ref[...] = s_scr[...]


def _prompt_gdn(qkv, sm, z, p, R):
    b, L, _ = qkv.shape
    full = lambda a: pl.BlockSpec(a.shape, lambda i, j: (0,) * a.ndim)
    params = [p["gdn_conv_w"], p["dtb_small"], p["alog_small"], p["gdn_norm_g"]]
    return pl.pallas_call(
        functools.partial(_gdn_kernel, R=R),
        grid=(b, L // R),
        in_specs=[pl.BlockSpec((None, R, CONV_C), lambda i, j: (i, j, 0)),
                  pl.BlockSpec((None, R, SMALL_W), lambda i, j: (i, j, 0)),
                  pl.BlockSpec((None, R, D_C), lambda i, j: (i, j, (D_A + D_B) // D_C))]
                 + [full(a) for a in params],
        out_specs=[pl.BlockSpec((None, R, D_C), lambda i, j: (i, j, 0)),
                   pl.BlockSpec((None, H_C * DK_C, DV_C), lambda i, j: (i, 0, 0))],
        out_shape=[jax.ShapeDtypeStruct((b, L, D_C), BF16),
                   jax.ShapeDtypeStruct((b, H_C * DK_C, DV_C), F32)],
        scratch_shapes=[pltpu.VMEM((R + SUBLANES, CONV_C), F32),
                        pltpu.VMEM((H_C * DK_C, DV_C), F32),
                        pltpu.VMEM((R, CONV_C), F32)],
        compiler_params=_cparams("arbitrary", "arbitrary"),
    )(qkv, sm, z, *params)


def _expand_heads(vals, n_heads, width):
    lane = _iota((1, n_heads * width), 1)
    out = jnp.zeros((1, n_heads * width), F32)
    for h in range(n_heads):
        out = jnp.where((lane >= h * width) & (lane < (h + 1) * width), vals[:, h:h + 1], out)
    return out


def _pad_rows(rows):
    n = rows[0].shape[-1]
    r = _iota((SUBLANES, n), 0)
    out = jnp.zeros((SUBLANES, n), F32)
    for i, v in enumerate(rows):
        out = jnp.where(r == i, v, out)
    return out


def _sample_step_kernel(xbc_ref, qkv_ref, sm_ref, z_ref, cst_b_ref, cst_c_ref, h0_ref, s0_ref,
                        cwb_ref, cbb_ref, cwc_ref, dtb_s_ref, alog_s_ref, dskip_ref, ngb_ref, ngc_ref,
                        ob_ref, oc_ref, cnew_b_ref, cnew_c_ref, h1_ref, s1_ref):
    def conv(x, st_ref, w_ref, new_ref):
        y = w_ref[CONV_W - 1:CONV_W, :] * x
        for i in range(CONV_W - 1):
            y = y + w_ref[i:i + 1, :] * st_ref[i:i + 1, :]
        for i in range(CONV_W - 2):
            new_ref[i:i + 1, :] = st_ref[i + 1:i + 2, :]
        new_ref[CONV_W - 2:CONV_W - 1, :] = x
        return y

    sm = sm_ref[...]
    dt_or_sp = _softplus(sm + dtb_s_ref[...])
    a_s = dt_or_sp * (-jnp.exp(alog_s_ref[...]))
    z = z_ref[...]

    xa = _silu(conv(xbc_ref[...], cst_b_ref, cwb_ref, cnew_b_ref) + cbb_ref[...])
    xs = xa[:, 0:D_B]
    dt_e = _expand_heads(dt_or_sp[:, DT_LANE:DT_LANE + H_B], H_B, P_B)
    a_e = _expand_heads(a_s[:, DT_LANE:DT_LANE + H_B], H_B, P_B)
    xc = xs * dt_e
    half = D_B // G_B
    lane_b = _iota((1, D_B), 1)
    bm = [xa[:, D_B + g * N_B:D_B + (g + 1) * N_B] for g in range(G_B)]
    cm = [xa[:, D_B + G_B * N_B + g * N_B:D_B + G_B * N_B + (g + 1) * N_B] for g in range(G_B)]
    h0 = h0_ref[...]
    coff = _dot_nt(_pad_rows(cm).astype(BF16), h0.astype(BF16))
    y_off = jnp.where(lane_b < half, coff[0:1, :], coff[1:2, :]) * jnp.exp(a_e)
    cb = [jnp.sum(cm[g] * bm[g], axis=-1, keepdims=True) for g in range(G_B)]
    y_diag = jnp.where(lane_b < half, cb[0], cb[1]) * xc
    y = (y_diag + y_off + xs * dskip_ref[...]) * _silu(z[:, D_A:D_A + D_B])
    y = jnp.concatenate([_rms(y[:, g * half:(g + 1) * half]) for g in range(G_B)], axis=-1)
    ob_ref[...] = y * ngb_ref[...]
    xc_rows = _pad_rows([jnp.where(lane_b < half, xc, 0.0), jnp.where(lane_b >= half, xc, 0.0)])
    outer = _dot_tn(xc_rows.astype(BF16), _pad_rows(bm).astype(BF16))
    dec_col = _dot_tn_hi(_pad_rows([jnp.exp(a_e)]), jnp.ones((SUBLANES, N_B), F32))
    h1_ref[...] = h0 * dec_col + outer

    qkv = _silu(conv(qkv_ref[...], cst_c_ref, cwc_ref, cnew_c_ref))
    beta_s = jax.nn.sigmoid(sm)
    outs = []
    for h in range(H_C):
        q = qkv[:, h * DK_C:(h + 1) * DK_C]
        k = qkv[:, D_C + h * DK_C:D_C + (h + 1) * DK_C]
        v = qkv[:, 2 * D_C + h * DV_C:2 * D_C + (h + 1) * DV_C]
        q = q * lax.rsqrt(jnp.sum(q * q, axis=-1, keepdims=True) + EPS) * (DK_C ** -0.5)
        k = k * lax.rsqrt(jnp.sum(k * k, axis=-1, keepdims=True) + EPS)
        beta = beta_s[:, BETA_LANE + h:BETA_LANE + h + 1]
        eg = jnp.exp(a_s[:, AG_LANE + h:AG_LANE + h + 1])
        s_old = s0_ref[h * DK_C:(h + 1) * DK_C, :]
        ws = _dot(_pad_rows([k * (beta * eg), q * eg]).astype(BF16), s_old.astype(BF16))
        v_new = v * beta - ws[0:1, :]
        o = ws[1:2, :] + jnp.sum(q * k, axis=-1, keepdims=True) * v_new
        upd = _dot_tn(_pad_rows([k]).astype(BF16), _pad_rows([v_new]).astype(BF16))
        eg_col = _dot_tn_hi(_pad_rows([jnp.broadcast_to(eg, (1, DK_C))]), jnp.ones((SUBLANES, DV_C), F32))
        s1_ref[h * DK_C:(h + 1) * DK_C, :] = s_old * eg_col + upd
        outs.append(_rms(o) * ngc_ref[...])
    oc_ref[...] = jnp.concatenate(outs, axis=-1) * _silu(z[:, D_A + D_B:D_MIX])


def _sample_step(xbc, qkv, sm, z, cst_b, cst_c, h0, s0, p):
    bs = xbc.shape[0]
    per_b = lambda a: pl.BlockSpec((None,) + tuple(a.shape[1:]), lambda i: (i,) + (0,) * (len(a.shape) - 1))
    full = lambda a: pl.BlockSpec(a.shape, lambda i: (0,) * a.ndim)
    data = [xbc, qkv, sm, z, cst_b, cst_c, h0, s0]
    params = [p["ssm_conv_w"], p["ssm_conv_b"], p["gdn_conv_w"], p["dtb_small"], p["alog_small"],
              p["ssm_d_e"], p["ssm_norm_g"], p["gdn_norm_g"]]
    outs = [jax.ShapeDtypeStruct((bs, 1, D_B), F32), jax.ShapeDtypeStruct((bs, 1, D_C), F32),
            jax.ShapeDtypeStruct(cst_b.shape, F32), jax.ShapeDtypeStruct(cst_c.shape, F32),
            jax.ShapeDtypeStruct(h0.shape, F32), jax.ShapeDtypeStruct(s0.shape, F32)]
    return pl.pallas_call(
        _sample_step_kernel,
        grid=(bs,),
        in_specs=[per_b(a) for a in data] + [full(a) for a in params],
        out_specs=[per_b(a) for a in outs],
        out_shape=outs,
        compiler_params=_cparams("arbitrary"),
    )(*data, *params)


def _permute_w_in(w):
    sizes = (2 * H_A * DH_A, 2 * H_A * DH_A, D_A, CONV_B, H_B, CONV_C, H_C, H_C, D_MIX)
    offs = [0]
    for s in sizes:
        offs.append(offs[-1] + s)
    col = lambda i: w[:, offs[i]:offs[i + 1]]
    small = jnp.concatenate([col(4), col(6), col(7),
                             jnp.zeros((w.shape[0], SMALL_W - H_B - 2 * H_C), w.dtype)], axis=1)
    return jnp.concatenate([col(0), col(1), col(2), col(3), col(5), col(8), small], axis=1).astype(BF16)


def _layer_params(l, w):
    pad_small = lambda a_ssm, a_gdn: jnp.concatenate(
        [a_ssm, jnp.zeros((H_C,), F32), a_gdn, jnp.zeros((SMALL_W - H_B - 2 * H_C,), F32)]).reshape(1, SMALL_W)
    rep = lambda a: jnp.repeat(a, P_B).reshape(1, D_B)
    expand = (jnp.arange(SMALL_W)[:, None] == (jnp.arange(D_B)[None, :] // P_B)).astype(BF16)
    return {
        "ssm_conv_w": w["ssm_conv_w"][l], "ssm_conv_b": w["ssm_conv_b"][l].reshape(1, CONV_B),
        "gdn_conv_w": w["gdn_conv_w"][l],
        "dtb_small": pad_small(w["ssm_dt_bias"][l], w["gdn_dt_bias"][l]),
        "alog_small": pad_small(w["ssm_A_log"][l], w["gdn_A_log"][l]),
        "ssm_dtb_e": rep(w["ssm_dt_bias"][l]), "ssm_alog_e": rep(w["ssm_A_log"][l]),
        "ssm_d_e": rep(w["ssm_D"][l]),
        "ssm_norm_g": w["ssm_norm_g"][l].reshape(1, D_B), "gdn_norm_g": w["gdn_norm_g"][l].reshape(1, DV_C),
        "expand": expand,
    }


def _pick(n, prefs):
    for t in prefs:
        if n % t == 0:
            return t
    return n


def kernel(x_prompt, x_sample, c_prompt, c_sample, cache_k, cache_v, state_ssm, state_ssm_conv, state_gdn, state_gdn_conv, page_table, w_ada, b_ada, w_in, w_out, rel_bias, attn_lambda, attn_subln_g, ssm_conv_w, ssm_conv_b, ssm_dt_bias, ssm_A_log, ssm_D, ssm_norm_g, gdn_conv_w, gdn_dt_bias, gdn_A_log, gdn_norm_g, final_norm_g):
    depth = w_in.shape[0]
    bp, L, _ = x_prompt.shape
    bs = x_sample.shape[0]
    n_pages = page_table.shape[1]
    past = n_pages * PAGE_SIZE
    n_pool = cache_k.shape[1]
    wts = dict(ssm_conv_w=ssm_conv_w, ssm_conv_b=ssm_conv_b, ssm_dt_bias=ssm_dt_bias, ssm_A_log=ssm_A_log,
               ssm_D=ssm_D, ssm_norm_g=ssm_norm_g, gdn_conv_w=gdn_conv_w, gdn_dt_bias=gdn_dt_bias,
               gdn_A_log=gdn_A_log, gdn_norm_g=gdn_norm_g)

    tm = _pick(L, (256, 128, 64))
    T = _pick(L, (256, 128))
    Q = _pick(L, (256, 128, 64))
    R = _pick(L, (256, 128, 64))
    G = _pick(n_pages, (16, 8, 4, 2, 1))

    mod = _modulation(jnp.concatenate([c_prompt, c_sample], axis=0), w_ada, b_ada)
    bias_near = _near_bias_tiles(rel_bias, T)
    bias_pages = _decode_bias(rel_bias, past)
    bias_new = jnp.broadcast_to(jnp.tile(rel_bias[0].astype(F32), 2)[:, None], (2 * H_A, DV_A))
    ck = cache_k.reshape(depth, n_pool, PAGE_SIZE * H_A, 2 * DH_A)
    cv = cache_v.reshape(depth, n_pool, PAGE_SIZE * H_A, DV_A)
    lane_half = jnp.arange(DV_A) < DH_A

    xp = x_prompt
    xs = x_sample.reshape(1, bs, D_MODEL)
    new_p, new_s = [], []
    for l in range(depth):
        lam_init = 0.8 - 0.6 * math.exp(-0.3 * l)
        p = _layer_params(l, wts)
        w_perm = _permute_w_in(w_in[l])
        w_o = w_out[l].astype(BF16)
        shift, scale, gate = jnp.split(mod[l], 3, axis=-1)
        mp = lambda a: a[:bp].reshape(bp, 1, D_MODEL)
        ms = lambda a: a[bp:].reshape(1, bs, D_MODEL)

        q, k, v, xbc, qkv, z, sm, k_heads, v_heads = _in_projection(xp, mp(scale), mp(shift), w_perm, tm, True)
        o_a = _prompt_attention(q, k, v, z, bias_near, attn_lambda[l], attn_subln_g[l], lam_init, T)
        o_b, ssm_h = _prompt_ssd(xbc, sm, z, p, Q)
        o_c, gdn_s = _prompt_gdn(qkv, sm, z, p, R)
        final = final_norm_g if l == depth - 1 else None
        xp = _out_projection(xp, mp(gate), o_a, o_b, o_c, w_o, final, tm)
        new_p.append((k_heads, v_heads,
                      ssm_h.reshape(bp, H_B, P_B, N_B), xbc[:, L - (CONV_W - 1):, :],
                      gdn_s.reshape(bp, H_C, DK_C, DV_C), qkv[:, L - (CONV_W - 1):, :]))

        q, k, v, xbc, qkv, z, sm = _in_projection(xs, ms(scale), ms(shift), w_perm, bs, False)
        q4 = q.reshape(bs, H_A, DV_A) * (DH_A ** -0.5)
        qpat = jnp.concatenate([jnp.where(lane_half, q4, 0.0), jnp.where(lane_half, 0.0, q4)], axis=1)
        k4 = k.reshape(bs, H_A, DV_A)
        v4 = v.reshape(bs, H_A, DV_A)
        o_a = _decode_attention(l, page_table, ck, cv, qpat, jnp.concatenate([k4, k4], axis=1),
                                jnp.concatenate([v4, v4], axis=1), bias_pages, bias_new,
                                z[..., :D_A].reshape(bs, H_A, DV_A), attn_lambda[l], attn_subln_g[l],
                                lam_init, G)
        row = lambda a: a.reshape(bs, 1, a.shape[-1])
        o_b, o_c, conv_b1, conv_c1, ssm_h, gdn_s = _sample_step(
            row(xbc), row(qkv), row(sm), row(z), state_ssm_conv[l], state_gdn_conv[l],
            state_ssm[l].reshape(bs, D_B, N_B), state_gdn[l].reshape(bs, H_C * DK_C, DV_C), p)
        xs = _out_projection(xs, ms(gate), o_a.reshape(1, bs, D_A).astype(BF16),
                             o_b.reshape(1, bs, D_B).astype(BF16), o_c.reshape(1, bs, D_C).astype(BF16),
                             w_o, final, bs)
        new_s.append((k.reshape(bs, 1, H_A, 2 * DH_A), v.reshape(bs, 1, H_A, DV_A),
                      ssm_h.reshape(bs, H_B, P_B, N_B), conv_b1,
                      gdn_s.reshape(bs, H_C, DK_C, DV_C), conv_c1))

    k_p, v_p, ssm_p, ssm_conv_p, gdn_p, gdn_conv_p = [jnp.stack(a) for a in zip(*new_p)]
    k_s, v_s, ssm_s, ssm_conv_s, gdn_s, gdn_conv_s = [jnp.stack(a) for a in zip(*new_s)]
    return (xp, xs.reshape(bs, 1, D_MODEL), k_p, v_p, ssm_p, ssm_conv_p, gdn_p, gdn_conv_p,
            k_s, v_s, ssm_s, ssm_conv_s, gdn_s, gdn_conv_s)
```

```python
import functools
import math

import jax
import jax.numpy as jnp
from jax import lax
from jax.experimental import pallas as pl
from jax.experimental.pallas import tpu as pltpu

F32 = jnp.float32
BF16 = jnp.bfloat16

D_MODEL = 1024
H_A, DH_A, DV_A = 4, 64, 128
D_A = H_A * DV_A
H_B, P_B, G_B, N_B = 8, 64, 2, 128
D_B = H_B * P_B
CONV_B = D_B + 2 * G_B * N_B
H_C, DK_C, DV_C = 4, 128, 128
D_C = H_C * DV_C
CONV_C = 2 * H_C * DK_C + D_C
CONV_W = 4
D_MIX = D_A + D_B + D_C
REL_BUCKETS, REL_MAX_DIST = 32, 128
PAGE_SIZE = 128
EPS = 1e-6
GDN_CHUNK = 64

LANES = 128
SUBLANES = 8
SMALL_W = LANES
DT_LANE, BETA_LANE, AG_LANE = 0, H_B, H_B + H_C
NEG_BIG = -1e30
LOG2E = math.log2(math.e)
ATTN_Q_SCALE = DH_A ** -0.5 * LOG2E
ATTN_STRIP = 64
VMEM_LIMIT = 56 * 1024 * 1024

SEGMENTS = (("q", 2 * H_A * DH_A), ("k", 2 * H_A * DH_A), ("v", D_A), ("xbc", CONV_B),
            ("qkv", CONV_C), ("z", D_MIX), ("small", SMALL_W))


def _cparams(*sem):
    return pltpu.CompilerParams(dimension_semantics=sem, vmem_limit_bytes=VMEM_LIMIT)


def _silu(x):
    return x * jax.nn.sigmoid(x)


def _softplus(x):
    return jnp.maximum(x, 0.0) + jnp.log1p(jnp.exp(-jnp.abs(x)))


def _rms(x):
    return x * lax.rsqrt(jnp.mean(x * x, axis=-1, keepdims=True) + EPS)


def _dot_nt(a, b):
    return lax.dot_general(a, b, (((1,), (1,)), ((), ())), preferred_element_type=F32)


def _dot_tn(a, b):
    return lax.dot_general(a, b, (((0,), (0,)), ((), ())), preferred_element_type=F32)


def _dot(a, b):
    return jnp.dot(a, b, preferred_element_type=F32)


def _iota(shape, dim):
    return lax.broadcasted_iota(jnp.int32, shape, dim)


def _mod_kernel(c_ref, w_ref, b_ref, o_ref):
    sc = _silu(c_ref[...]).astype(BF16)
    o_ref[...] = _dot(sc, w_ref[...].astype(BF16)) + b_ref[...]


def _modulation(c_all, w_ada, b_ada):
    depth = w_ada.shape[0]
    n = c_all.shape[0]
    return pl.pallas_call(
        _mod_kernel,
        grid=(depth, 3),
        in_specs=[pl.BlockSpec((n, D_MODEL), lambda l, j: (0, 0)),
                  pl.BlockSpec((None, D_MODEL, D_MODEL), lambda l, j: (l, 0, j)),
                  pl.BlockSpec((None, 1, D_MODEL), lambda l, j: (l, 0, j))],
        out_specs=pl.BlockSpec((None, n, D_MODEL), lambda l, j: (l, 0, j)),
        out_shape=jax.ShapeDtypeStruct((depth, n, 3 * D_MODEL), F32),
        compiler_params=_cparams("arbitrary", "arbitrary"),
    )(c_all, w_ada, b_ada.reshape(depth, 1, 3 * D_MODEL))


def _inproj_kernel(x_ref, scale_ref, shift_ref, w_ref, *out_refs, heads_out):
    h = _rms(x_ref[...]) * (1.0 + scale_ref[...]) + shift_ref[...]
    hb = h.astype(BF16)
    seg_refs = out_refs[:len(SEGMENTS)]
    head_refs = dict(zip(("k", "v"), out_refs[len(SEGMENTS):])) if heads_out else {}
    off = 0
    for (name, n), ref in zip(SEGMENTS, seg_refs):
        res = _dot(hb, w_ref[:, off:off + n])
        scaled = res * ATTN_Q_SCALE if heads_out and name == "q" else res
        ref[...] = scaled.astype(ref.dtype)
        if name in head_refs:
            for hd in range(H_A):
                head_refs[name][:, hd, :] = res[:, hd * DV_A:(hd + 1) * DV_A]
        off += n


def _in_projection(x, scale, shift, w_perm, tm, heads_out):
    b, L, _ = x.shape
    ms = scale.shape[1]
    mod_rows = tm if ms == L else 1
    mod_map = (lambda i, j: (i, j, 0)) if ms == L else (lambda i, j: (i, 0, 0))
    n_tot = w_perm.shape[1]
    dt = lambda name: BF16 if heads_out and name in ("q", "k", "v") else F32
    out_specs = [pl.BlockSpec((None, tm, w), lambda i, j: (i, j, 0)) for _, w in SEGMENTS]
    out_shape = [jax.ShapeDtypeStruct((b, L, w), dt(name)) for name, w in SEGMENTS]
    if heads_out:
        out_specs += [pl.BlockSpec((None, tm, H_A, DV_A), lambda i, j: (i, j, 0, 0))] * 2
        out_shape += [jax.ShapeDtypeStruct((b, L, H_A, DV_A), F32)] * 2
    return pl.pallas_call(
        functools.partial(_inproj_kernel, heads_out=heads_out),
        grid=(b, L // tm),
        in_specs=[pl.BlockSpec((None, tm, D_MODEL), lambda i, j: (i, j, 0)),
                  pl.BlockSpec((None, mod_rows, D_MODEL), mod_map),
                  pl.BlockSpec((None, mod_rows, D_MODEL), mod_map),
                  pl.BlockSpec((D_MODEL, n_tot), lambda i, j: (0, 0), pipeline_mode=pl.Buffered(1))],
        out_specs=out_specs,
        out_shape=out_shape,
        compiler_params=_cparams("arbitrary", "arbitrary"),
    )(x, scale, shift, w_perm)


def _outproj_kernel(x_ref, gate_ref, oa_ref, ob_ref, oc_ref, w_ref, *rest, final):
    acc = _dot(oa_ref[...], w_ref[0:D_A, :])
    acc += _dot(ob_ref[...], w_ref[D_A:D_A + D_B, :])
    acc += _dot(oc_ref[...], w_ref[D_A + D_B:D_MIX, :])
    y = x_ref[...] + gate_ref[...] * acc
    if final:
        g_ref, o_ref = rest
        o_ref[...] = _rms(y) * g_ref[...]
    else:
        (o_ref,) = rest
        o_ref[...] = y


def _out_projection(x, gate, oa, ob, oc, w_out, final_g, tm):
    b, L, _ = x.shape
    ms = gate.shape[1]
    mod_rows = tm if ms == L else 1
    mod_map = (lambda i, j: (i, j, 0)) if ms == L else (lambda i, j: (i, 0, 0))
    row = lambda w: pl.BlockSpec((None, tm, w), lambda i, j: (i, j, 0))
    in_specs = [row(D_MODEL), pl.BlockSpec((None, mod_rows, D_MODEL), mod_map),
                row(D_A), row(D_B), row(D_C),
                pl.BlockSpec((D_MIX, D_MODEL), lambda i, j: (0, 0))]
    args = [x, gate, oa, ob, oc, w_out]
    final = final_g is not None
    if final:
        in_specs.append(pl.BlockSpec((1, D_MODEL), lambda i, j: (0, 0)))
        args.append(final_g.reshape(1, D_MODEL))
    return pl.pallas_call(
        functools.partial(_outproj_kernel, final=final),
        grid=(b, L // tm),
        in_specs=in_specs,
        out_specs=row(D_MODEL),
        out_shape=jax.ShapeDtypeStruct((b, L, D_MODEL), F32),
        compiler_params=_cparams("arbitrary", "arbitrary"),
    )(*args)


def _lambda_value(lp, lam_init):
    s01 = jnp.sum(lp[0:1, :] * lp[1:2, :], axis=-1, keepdims=True)
    s23 = jnp.sum(lp[2:3, :] * lp[3:4, :], axis=-1, keepdims=True)
    return jnp.exp(s01) - jnp.exp(s23) + lam_init


def _rel_bias_values(table, n):
    exact = REL_BUCKETS // 2
    large = exact + (jnp.log(jnp.maximum(n, 1).astype(F32) / exact)
                     / math.log(REL_MAX_DIST / exact) * (REL_BUCKETS - exact)).astype(jnp.int32)
    bucket = jnp.where(n < exact, n, jnp.minimum(large, REL_BUCKETS - 1))
    onehot = bucket[..., None, None] == jnp.arange(REL_BUCKETS)[:, None]
    return jnp.sum(jnp.where(onehot, table.astype(F32), 0.0), axis=-2)


def _attn_kernel(q_ref, kb_scr, vb_scr, z_ref, bias_ref, lamp_ref, subln_ref, o_ref, s_bufs, w_bufs,
                 *, T, lam_init):
    L = q_ref.shape[0]
    nq = L // T
    lam = _lambda_value(lamp_ref[...], lam_init)
    lane = _iota((T, DV_A), 1)
    half = LANES

    def logit_tasks(i):
        q = q_ref[i * T:(i + 1) * T, :]
        s_scr = s_bufs.at[i % 2]
        tasks = []
        for sub in range(2):
            qm = jnp.where((lane < DH_A) if sub == 0 else (lane >= DH_A), q, jnp.zeros_like(q))
            for j in range(i + 1):
                def task(qm=qm, sub=sub, j=j):
                    s = _dot_nt(qm, kb_scr[j * T:(j + 1) * T, :])
                    if j == i:
                        s = s + bias_ref[:, T:2 * T]
                    elif j == i - 1:
                        s = s + bias_ref[:, 0:T]
                    s_scr[sub, j] = s
                tasks.append(task)
        return tasks

    def softmax_strip(i, r0):
        s_scr, w_scr = s_bufs.at[i % 2], w_bufs.at[i % 2]
        strip = slice(r0, r0 + ATTN_STRIP)
        scale = []
        for sub in range(2):
            mx = None
            for j in range(i + 1):
                s = s_scr[sub, j, strip, :]
                for c0 in range(0, T, half):
                    t = s[:, c0:c0 + half]
                    mx = t if mx is None else jnp.maximum(mx, t)
            m_row = jnp.max(mx, axis=-1, keepdims=True)
            ls = None
            for j in range(i + 1):
                p = jnp.exp2(s_scr[sub, j, strip, :] - m_row)
                s_scr[sub, j, strip, :] = p
                for c0 in range(0, T, half):
                    t = p[:, c0:c0 + half]
                    ls = t if ls is None else ls + t
            scale.append(1.0 / jnp.sum(ls, axis=-1, keepdims=True))
        w0, w1 = scale[0], lam * scale[1]
        for j in range(i + 1):
            w_scr[strip, j * T:(j + 1) * T] = (s_scr[0, j, strip, :] * w0
                                               - s_scr[1, j, strip, :] * w1).astype(BF16)

    for task in logit_tasks(0):
        task()
    for i in range(nq):
        nxt = logit_tasks(i + 1) if i + 1 < nq else []
        strips = list(range(0, T, ATTN_STRIP))
        per = -(-len(nxt) // len(strips))
        for n, r0 in enumerate(strips):
            softmax_strip(i, r0)
            for task in nxt[n * per:(n + 1) * per]:
                task()
        rows = slice(i * T, (i + 1) * T)
        o = _dot(w_bufs[i % 2, :, 0:(i + 1) * T], vb_scr[0:(i + 1) * T, :])
        o = _rms(o) * subln_ref[...] * (1.0 - lam_init) * _silu(z_ref[rows, :])
        o_ref[rows, :] = o.astype(o_ref.dtype)


def _prompt_attention(q, k, v, z, bias_near, lam_p, subln, lam_init, T):
    b, L, _ = q.shape
    seq = lambda: pl.BlockSpec((None, L, DV_A), lambda i, h: (i, 0, h))
    return pl.pallas_call(
        functools.partial(_attn_kernel, T=T, lam_init=lam_init),
        grid=(b, H_A),
        in_specs=[seq(), seq(), seq(), seq(),
                  pl.BlockSpec((None, T, 2 * T), lambda i, h: (h, 0, 0)),
                  pl.BlockSpec((4, DH_A), lambda i, h: (0, 0)),
                  pl.BlockSpec((1, DV_A), lambda i, h: (0, 0))],
        out_specs=seq(),
        out_shape=jax.ShapeDtypeStruct((b, L, D_A), BF16),
        scratch_shapes=[pltpu.VMEM((2, 2, L // T, T, T), F32), pltpu.VMEM((2, T, L), BF16)],
        compiler_params=_cparams("arbitrary", "arbitrary"),
    )(q, k, v, z, bias_near, lam_p, subln.reshape(1, DV_A))


def _near_bias_tiles(table, T):
    period = 3 * T
    k = jnp.arange(period)
    n = jnp.where(k < 2 * T, T - k, T + period - k)
    vals = _rel_bias_values(table, jnp.maximum(n, 0)) - table[REL_BUCKETS - 1].astype(F32)
    w = jnp.where((n >= 0)[:, None], vals * LOG2E, NEG_BIG).T
    skew = jnp.tile(w, (1, T + 1))[:, :T * (period - 1)].reshape(H_A, T, period - 1)
    return skew[:, :, :2 * T]


def _decode_kernel(pt_ref, qpat_ref, knew_ref, vnew_ref, bias_ref, biasnew_ref, z_ref, lamp_ref,
                   subln_ref, *rest, G, lam_init):
    del pt_ref
    k_refs, v_refs = rest[:G], rest[G:2 * G]
    o_ref, m_scr, l_scr, acc_scr = rest[2 * G:]
    j = pl.program_id(1)

    @pl.when(j == 0)
    def _():
        m_scr[...] = jnp.full(m_scr.shape, NEG_BIG, F32)
        l_scr[...] = jnp.zeros(l_scr.shape, F32)
        acc_scr[...] = jnp.zeros(acc_scr.shape, F32)

    qf = qpat_ref[...]
    qb = qf.astype(BF16)
    s = [_dot_nt(qb, k_refs[g][...].astype(BF16)) + bias_ref[g] for g in range(G)]
    m_old = m_scr[...]
    m_tile = s[0]
    for g in range(1, G):
        m_tile = jnp.maximum(m_tile, s[g])
    m = jnp.maximum(m_old, jnp.max(m_tile, axis=-1, keepdims=True))
    alpha = jnp.exp(m_old - m)
    p = [jnp.exp(s[g] - m) for g in range(G)]
    p_sum = p[0]
    for g in range(1, G):
        p_sum = p_sum + p[g]
    l = alpha * l_scr[...] + jnp.sum(p_sum, axis=-1, keepdims=True)
    pv = _dot(p[0].astype(BF16), v_refs[0][...].astype(BF16))
    for g in range(1, G):
        pv = pv + _dot(p[g].astype(BF16), v_refs[g][...].astype(BF16))
    acc = alpha * acc_scr[...] + pv
    m_scr[...] = m
    l_scr[...] = l
    acc_scr[...] = acc

    @pl.when(j == pl.num_programs(1) - 1)
    def _():
        s_new = jnp.sum(qf * knew_ref[...], axis=-1, keepdims=True) + biasnew_ref[:, 0:1]
        m_new = jnp.maximum(m, s_new)
        alpha = jnp.exp(m - m_new)
        p_new = jnp.exp(s_new - m_new)
        l_fin = alpha * l + p_new
        out = (alpha * acc + p_new * vnew_ref[...]) / l_fin
        lam = _lambda_value(lamp_ref[...], lam_init)
        o = out[0:H_A, :] - lam * out[H_A:2 * H_A, :]
        o_ref[...] = _rms(o) * subln_ref[...] * (1.0 - lam_init) * _silu(z_ref[...])


def _decode_attention(layer, page_table, cache_k, cache_v, qpat, knew, vnew, bias_pages, bias_new, z3,
                      lam_p, subln, lam_init, G):
    bs, n_pages = page_table.shape
    rows = PAGE_SIZE * H_A

    def page_spec(g):
        return pl.BlockSpec((None, None, rows, DV_A),
                            lambda i, j, pt: (layer, pt[i, j * G + g], 0, 0))

    per_b = lambda r: pl.BlockSpec((None, r, DV_A), lambda i, j, pt: (i, 0, 0))
    grid_spec = pltpu.PrefetchScalarGridSpec(
        num_scalar_prefetch=1,
        grid=(bs, n_pages // G),
        in_specs=[per_b(2 * H_A), per_b(2 * H_A), per_b(2 * H_A),
                  pl.BlockSpec((G, 2 * H_A, rows), lambda i, j, pt: (j, 0, 0)),
                  pl.BlockSpec((2 * H_A, DV_A), lambda i, j, pt: (0, 0)),
                  per_b(H_A),
                  pl.BlockSpec((4, DH_A), lambda i, j, pt: (0, 0)),
                  pl.BlockSpec((1, DV_A), lambda i, j, pt: (0, 0))]
                 + [page_spec(g) for g in range(G)] + [page_spec(g) for g in range(G)],
        out_specs=per_b(H_A),
        scratch_shapes=[pltpu.VMEM((2 * H_A, 1), F32), pltpu.VMEM((2 * H_A, 1), F32),
                        pltpu.VMEM((2 * H_A, DV_A), F32)],
    )
    return pl.pallas_call(
        functools.partial(_decode_kernel, G=G, lam_init=lam_init),
        grid_spec=grid_spec,
        out_shape=jax.ShapeDtypeStruct((bs, H_A, DV_A), F32),
        compiler_params=_cparams("arbitrary", "arbitrary"),
    )(page_table, qpat, knew, vnew, bias_pages, bias_new, z3, lam_p, subln.reshape(1, DV_A),
      *([cache_k] * G), *([cache_v] * G))


def _decode_bias(table, past):
    n = past - jnp.arange(past)
    vals = _rel_bias_values(table, n).reshape(past // PAGE_SIZE, PAGE_SIZE, H_A)
    vals = jnp.moveaxis(vals, -1, 1)[..., None]
    same = (jnp.arange(H_A)[:, None, None] == jnp.arange(H_A)[None, None, :])
    full = jnp.where(same[None], vals, NEG_BIG).reshape(past // PAGE_SIZE, H_A, PAGE_SIZE * H_A)
    return jnp.concatenate([full, full], axis=1)


def _conv_taps(xp_ref, x, w_ref, rows):
    xp_ref[SUBLANES:SUBLANES + rows, :] = x
    xe = xp_ref[...]
    t = w_ref[0:1, :] * xe
    for i in range(1, CONV_W):
        t = w_ref[i:i + 1, :] * xe + pltpu.roll(t, 1, axis=0)
    xp_ref[0:SUBLANES, :] = x[rows - SUBLANES:rows, :]
    return t[SUBLANES:SUBLANES + rows, :]


def _ssd_kernel(xbc_ref, sm_ref, z_ref, cw_ref, cb_ref, dtb_s_ref, alog_s_ref,
                dskip_ref, ng_ref, exp_ref, y_ref, hout_ref, xp_scr, h_scr, *, Q):
    c = pl.program_id(1)

    @pl.when(c == 0)
    def _():
        xp_scr[0:SUBLANES, :] = jnp.zeros((SUBLANES, CONV_B), F32)
        h_scr[...] = jnp.zeros(h_scr.shape, F32)

    xa = _silu(_conv_taps(xp_scr, xbc_ref[...], cw_ref, Q) + cb_ref[...])
    xs = xa[:, 0:D_B]
    sm = sm_ref[...]
    row = _iota((Q, Q), 0)
    col = _iota((Q, Q), 1)
    tril = row >= col
    expand = exp_ref[...]

    dt_s = _softplus(sm + dtb_s_ref[...])
    a_s = dt_s * (-jnp.exp(alog_s_ref[...]))
    cum_s = _dot_exact_rhs(tril.astype(BF16), a_s)
    cum_t = cum_s.T
    dt_e = _dot_exact_lhs(dt_s, expand)
    cum_e = _dot_exact_lhs(cum_s, expand)
    xc = xs * dt_e
    dec_end = jnp.exp(cum_e[Q - 1:Q, :] - cum_e)
    dec_in = jnp.exp(cum_e)
    xc_b = xc.astype(BF16)
    xe_b = (xc * dec_end).astype(BF16)
    lane = _iota((Q, LANES), 1)
    prow = _iota((LANES, LANES), 0)
    heads_per_group = H_B // G_B
    pairs = []
    for g in range(G_B):
        bm = xa[:, D_B + g * N_B:D_B + (g + 1) * N_B].astype(BF16)
        cm = xa[:, D_B + G_B * N_B + g * N_B:D_B + G_B * N_B + (g + 1) * N_B].astype(BF16)
        cb = _dot_nt(cm, bm)
        for pr in range(g * heads_per_group // 2, (g + 1) * heads_per_group // 2):
            lo, hi = pr * LANES, (pr + 1) * LANES
            yd = []
            for hh in (2 * pr, 2 * pr + 1):
                seg = cum_s[:, hh:hh + 1] - cum_t[hh:hh + 1, :]
                mat = cb * jnp.exp(jnp.where(tril, seg, -jnp.inf))
                yd.append(_dot(mat.astype(BF16), xc_b[:, lo:hi]))
            y_diag = jnp.where(lane < P_B, yd[0], yd[1])
            hp = h_scr[lo:hi, :]
            y_off = _dot_nt(cm, hp.astype(BF16)) * dec_in[:, lo:hi]
            st = _dot_tn(xe_b[:, lo:hi], bm)
            a_last = jnp.where(prow < P_B, cum_t[2 * pr:2 * pr + 1, Q - 1:Q],
                               cum_t[2 * pr + 1:2 * pr + 2, Q - 1:Q])
            h_scr[lo:hi, :] = hp * jnp.exp(a_last) + st
            pairs.append(y_diag + y_off + xs[:, lo:hi] * dskip_ref[:, lo:hi])
    y = jnp.concatenate(pairs, axis=-1) * _silu(z_ref[...])
    gw = D_B // G_B
    y = jnp.concatenate([_rms(y[:, g * gw:(g + 1) * gw]) for g in range(G_B)], axis=-1) * ng_ref[...]
    y_ref[...] = y.astype(y_ref.dtype)

    @pl.when(c == pl.num_programs(1) - 1)
    def _():
        hout_ref[...] = h_scr[...]


def _prompt_ssd(xbc, sm, z, p, Q):
    b, L, _ = xbc.shape
    full = lambda a: pl.BlockSpec(a.shape, lambda i, j: (0,) * a.ndim)
    params = [p["ssm_conv_w"], p["ssm_conv_b"], p["dtb_small"], p["alog_small"],
              p["ssm_d_e"], p["ssm_norm_g"], p["expand"]]
    return pl.pallas_call(
        functools.partial(_ssd_kernel, Q=Q),
        grid=(b, L // Q),
        in_specs=[pl.BlockSpec((None, Q, CONV_B), lambda i, j: (i, j, 0)),
                  pl.BlockSpec((None, Q, SMALL_W), lambda i, j: (i, j, 0)),
                  pl.BlockSpec((None, Q, D_B), lambda i, j: (i, j, D_A // D_B))]
                 + [full(a) for a in params],
        out_specs=[pl.BlockSpec((None, Q, D_B), lambda i, j: (i, j, 0)),
                   pl.BlockSpec((None, D_B, N_B), lambda i, j: (i, 0, 0))],
        out_shape=[jax.ShapeDtypeStruct((b, L, D_B), BF16),
                   jax.ShapeDtypeStruct((b, D_B, N_B), F32)],
        scratch_shapes=[pltpu.VMEM((Q + SUBLANES, CONV_B), F32), pltpu.VMEM((D_B, N_B), F32)],
        compiler_params=_cparams("arbitrary", "arbitrary"),
    )(xbc, sm, z, *params)


def _split2(a):
    hi = a.astype(BF16)
    return hi, (a - hi.astype(F32)).astype(BF16)


def _split3(x):
    x1 = x.astype(BF16)
    r1 = x - x1.astype(F32)
    x2 = r1.astype(BF16)
    return x1, x2, (r1 - x2.astype(F32)).astype(BF16)


def _dot_exact_rhs(a_bf16, x):
    x1, x2, x3 = _split3(x)
    return _dot(a_bf16, x1) + (_dot(a_bf16, x2) + _dot(a_bf16, x3))


def _dot_exact_lhs(x, a_bf16):
    x1, x2, x3 = _split3(x)
    return _dot(x1, a_bf16) + (_dot(x2, a_bf16) + _dot(x3, a_bf16))


def _gdn_head_prepare(act, gcum, gcum_t, beta_s, r0, h):
    cs = GDN_CHUNK
    rows = slice(r0, r0 + cs)
    q = act[rows, h * DK_C:(h + 1) * DK_C]
    k = act[rows, D_C + h * DK_C:D_C + (h + 1) * DK_C]
    v = act[rows, 2 * D_C + h * DV_C:2 * D_C + (h + 1) * DV_C]
    q = q * lax.rsqrt(jnp.sum(q * q, axis=-1, keepdims=True) + EPS) * (DK_C ** -0.5)
    k = k * lax.rsqrt(jnp.sum(k * k, axis=-1, keepdims=True) + EPS)
    beta = beta_s[rows, BETA_LANE + h:BETA_LANE + h + 1]
    gc = gcum[rows, AG_LANE + h:AG_LANE + h + 1]
    gr = gcum_t[AG_LANE + h:AG_LANE + h + 1, rows]
    g_last = gcum[r0 + cs - 1:r0 + cs, AG_LANE + h:AG_LANE + h + 1]
    ri = _iota((cs, 2 * cs), 0)
    ci = _iota((cs, 2 * cs), 1) & (cs - 1)
    decay = jnp.exp(jnp.where(ri >= ci, gc - jnp.concatenate([gr, gr], axis=1), -jnp.inf))
    kb = k.astype(BF16)
    kk = _dot_nt(kb, jnp.concatenate([kb, kb], axis=0))
    p = jnp.where(ri > ci, -(beta * kk * decay), 0.0)
    rhs = jnp.concatenate([v * beta, k * (beta * jnp.exp(gc))], axis=-1)
    attn = (_dot_nt(q.astype(BF16), kb) * decay[:, 0:cs]).astype(BF16)
    qg = q * jnp.exp(gc)
    kd = (k * jnp.exp(g_last - gc)).astype(BF16)
    return p, rhs, attn, qg, kd, jnp.exp(g_last)


def _gdn_head_finish(sol, attn, qg, kd, e_last):
    x_b = sol.astype(BF16)
    ax = _dot(attn, x_b)
    kx = _dot_tn(kd, x_b)
    lhs = jnp.concatenate([kx[:, DV_C:2 * DV_C], qg - ax[:, DV_C:2 * DV_C]], axis=0).astype(BF16)
    return lhs, kx[:, 0:DV_C], ax[:, 0:DV_C], e_last


def _split_lhs(p):
    hi, lo = _split2(p)
    half = jnp.where(_iota(p.shape, 1) < GDN_CHUNK, hi, lo)
    return jnp.concatenate([half, half], axis=1)


def _split_rhs(x):
    hi, lo = _split2(x)
    return jnp.concatenate([hi, hi, lo, lo], axis=0)


def _neumann_solve(ps, xs):
    n_levels = GDN_CHUNK.bit_length() - 1
    for lvl in range(n_levels):
        lhs = [_split_lhs(p) for p in ps]
        xs = [x + _dot(a, _split_rhs(x)) for a, x in zip(lhs, xs)]
        if lvl + 1 < n_levels:
            ps = [_dot(a, _split_rhs(p)) for a, p in zip(lhs, ps)]
    return xs


def _gdn_kernel(qkv_ref, sm_ref, z_ref, cw_ref, dtb_s_ref, alog_s_ref, ng_ref, o_ref, sout_ref,
                xp_scr, s_scr, act_scr, *, R):
    c = pl.program_id(1)

    @pl.when(c == 0)
    def _():
        xp_scr[0:SUBLANES, :] = jnp.zeros((SUBLANES, CONV_C), F32)
        s_scr[...] = jnp.zeros(s_scr.shape, F32)

    cs = GDN_CHUNK
    act_scr[...] = _silu(_conv_taps(xp_scr, qkv_ref[...], cw_ref, R))
    sm = sm_ref[...]
    g_s = _softplus(sm + dtb_s_ref[...]) * (-jnp.exp(alog_s_ref[...]))
    beta_s = jax.nn.sigmoid(sm)
    ri = _iota((R, R), 0)
    ci = _iota((R, R), 1)
    chunk_tril = (lax.shift_right_logical(ri, 6) == lax.shift_right_logical(ci, 6)) & (ri >= ci)
    gcum = _dot_exact_rhs(chunk_tril.astype(BF16), g_s)
    gcum_t = gcum.T

    n_chunks = R // cs
    def state_free_part(chunks):
        prep = [_gdn_head_prepare(act_scr, gcum, gcum_t, beta_s, ch * cs, h) for ch in chunks for h in range(H_C)]
        sols = _neumann_solve([pr[0] for pr in prep], [pr[1] for pr in prep])
        return [_gdn_head_finish(sol, *pr[2:]) for sol, pr in zip(sols, prep)]

    def state_step(ch, heads):
        outs = []
        for h, (lhs, c_add, o_add, e_last) in enumerate(heads):
            s_old = s_scr[h * DK_C:(h + 1) * DK_C, :]
            r = _dot(lhs, s_old.astype(BF16))
            s_scr[h * DK_C:(h + 1) * DK_C, :] = s_old * e_last + (c_add - r[0:DK_C, :])
            outs.append(_rms(r[DK_C:DK_C + cs, :] + o_add) * ng_ref[...])
        rows = slice(ch * cs, (ch + 1) * cs)
        o_ref[rows, :] = (jnp.concatenate(outs, axis=-1) * _silu(z_ref[rows, :])).astype(o_ref.dtype)

    half = max(n_chunks // 2, 1)
    for chunks in (range(0, half), range(half, n_chunks)):
        parts = state_free_part(chunks)
        for n, ch in enumerate(chunks):
            state_step(ch, parts[n * H_C:(n + 1) * H_C])

    @pl.when(c == pl.num_programs(1) - 1)
    def _():
        sout_ref[...] = s_scr[...]


def _prompt_gdn(qkv, sm, z, p, R):
    b, L, _ = qkv.shape
    full = lambda a: pl.BlockSpec(a.shape, lambda i, j: (0,) * a.ndim)
    params = [p["gdn_conv_w"], p["dtb_small"], p["alog_small"], p["gdn_norm_g"]]
    return pl.pallas_call(
        functools.partial(_gdn_kernel, R=R),
        grid=(b, L // R),
        in_specs=[pl.BlockSpec((None, R, CONV_C), lambda i, j: (i, j, 0)),
                  pl.BlockSpec((None, R, SMALL_W), lambda i, j: (i, j, 0)),
                  pl.BlockSpec((None, R, D_C), lambda i, j: (i, j, (D_A + D_B) // D_C))]
                 + [full(a) for a in params],
        out_specs=[pl.BlockSpec((None, R, D_C), lambda i, j: (i, j, 0)),
                   pl.BlockSpec((None, H_C * DK_C, DV_C), lambda i, j: (i, 0, 0))],
        out_shape=[jax.ShapeDtypeStruct((b, L, D_C), BF16),
                   jax.ShapeDtypeStruct((b, H_C * DK_C, DV_C), F32)],
        scratch_shapes=[pltpu.VMEM((R + SUBLANES, CONV_C), F32),
                        pltpu.VMEM((H_C * DK_C, DV_C), F32),
                        pltpu.VMEM((R, CONV_C), F32)],
        compiler_params=_cparams("arbitrary", "arbitrary"),
    )(qkv, sm, z, *params)


def _expand_heads(vals, n_heads, width):
    lane = _iota((1, n_heads * width), 1)
    out = jnp.zeros((1, n_heads * width), F32)
    for h in range(n_heads):
        out = jnp.where((lane >= h * width) & (lane < (h + 1) * width), vals[:, h:h + 1], out)
    return out


def _pad_rows(rows):
    n = rows[0].shape[-1]
    r = _iota((SUBLANES, n), 0)
    out = jnp.zeros((SUBLANES, n), F32)
    for i, v in enumerate(rows):
        out = jnp.where(r == i, v, out)
    return out


def _sample_step_kernel(*refs, n_data, n_params, block_b):
    data, params, outs = refs[:n_data], refs[n_data:n_data + n_params], refs[n_data + n_params:]
    for bi in range(block_b):
        _sample_step_one(*[r.at[bi] for r in data], *params, *[r.at[bi] for r in outs])


def _sample_step_one(xbc_ref, qkv_ref, sm_ref, z_ref, cst_b_ref, cst_c_ref, h0_ref, s0_ref,
                     cwb_ref, cbb_ref, cwc_ref, dtb_s_ref, alog_s_ref, dskip_ref, ngb_ref, ngc_ref,
                     ob_ref, oc_ref, cnew_b_ref, cnew_c_ref, h1_ref, s1_ref):
    def conv(x, st_ref, w_ref, new_ref):
        y = w_ref[CONV_W - 1:CONV_W, :] * x
        for i in range(CONV_W - 1):
            y = y + w_ref[i:i + 1, :] * st_ref[i:i + 1, :]
        for i in range(CONV_W - 2):
            new_ref[i:i + 1, :] = st_ref[i + 1:i + 2, :]
        new_ref[CONV_W - 2:CONV_W - 1, :] = x
        return y

    sm = sm_ref[...]
    dt_or_sp = _softplus(sm + dtb_s_ref[...])
    a_s = dt_or_sp * (-jnp.exp(alog_s_ref[...]))
    z = z_ref[...]

    xa = _silu(conv(xbc_ref[...], cst_b_ref, cwb_ref, cnew_b_ref) + cbb_ref[...])
    xs = xa[:, 0:D_B]
    dt_e = _expand_heads(dt_or_sp[:, DT_LANE:DT_LANE + H_B], H_B, P_B)
    a_e = _expand_heads(a_s[:, DT_LANE:DT_LANE + H_B], H_B, P_B)
    xc = xs * dt_e
    half = D_B // G_B
    lane_b = _iota((1, D_B), 1)
    bm = [xa[:, D_B + g * N_B:D_B + (g + 1) * N_B] for g in range(G_B)]
    cm = [xa[:, D_B + G_B * N_B + g * N_B:D_B + G_B * N_B + (g + 1) * N_B] for g in range(G_B)]
    h0 = h0_ref[...]
    coff = _dot_nt(_pad_rows(cm).astype(BF16), h0.astype(BF16))
    y_off = jnp.where(lane_b < half, coff[0:1, :], coff[1:2, :]) * jnp.exp(a_e)
    cb = [jnp.sum(cm[g] * bm[g], axis=-1, keepdims=True) for g in range(G_B)]
    y_diag = jnp.where(lane_b < half, cb[0], cb[1]) * xc
    y = (y_diag + y_off + xs * dskip_ref[...]) * _silu(z[:, D_A:D_A + D_B])
    y = jnp.concatenate([_rms(y[:, g * half:(g + 1) * half]) for g in range(G_B)], axis=-1)
    ob_ref[...] = y * ngb_ref[...]
    xc_rows = _pad_rows([jnp.where(lane_b < half, xc, 0.0), jnp.where(lane_b >= half, xc, 0.0)])
    outer = _dot_tn(xc_rows.astype(BF16), _pad_rows(bm).astype(BF16))
    state_row = _iota((D_B, N_B), 0)
    dec = jnp.exp(a_s)
    dec_col = jnp.zeros((D_B, N_B), F32)
    for h in range(H_B):
        in_head = (state_row >= h * P_B) & (state_row < (h + 1) * P_B)
        dec_col = jnp.where(in_head, dec[:, DT_LANE + h:DT_LANE + h + 1], dec_col)
    h1_ref[...] = h0 * dec_col + outer

    qkv = _silu(conv(qkv_ref[...], cst_c_ref, cwc_ref, cnew_c_ref))
    beta_s = jax.nn.sigmoid(sm)
    outs = []
    for h in range(H_C):
        q = qkv[:, h * DK_C:(h + 1) * DK_C]
        k = qkv[:, D_C + h * DK_C:D_C + (h + 1) * DK_C]
        v = qkv[:, 2 * D_C + h * DV_C:2 * D_C + (h + 1) * DV_C]
        q = q * lax.rsqrt(jnp.sum(q * q, axis=-1, keepdims=True) + EPS) * (DK_C ** -0.5)
        k = k * lax.rsqrt(jnp.sum(k * k, axis=-1, keepdims=True) + EPS)
        beta = beta_s[:, BETA_LANE + h:BETA_LANE + h + 1]
        eg = jnp.exp(a_s[:, AG_LANE + h:AG_LANE + h + 1])
        s_old = s0_ref[h * DK_C:(h + 1) * DK_C, :]
        ws = _dot(_pad_rows([k * (beta * eg), q * eg]).astype(BF16), s_old.astype(BF16))
        v_new = v * beta - ws[0:1, :]
        o = ws[1:2, :] + jnp.sum(q * k, axis=-1, keepdims=True) * v_new
        upd = _dot_tn(_pad_rows([k]).astype(BF16), _pad_rows([v_new]).astype(BF16))
        s1_ref[h * DK_C:(h + 1) * DK_C, :] = s_old * eg + upd
        outs.append(_rms(o) * ngc_ref[...])
    oc_ref[...] = jnp.concatenate(outs, axis=-1) * _silu(z[:, D_A + D_B:D_MIX])


def _sample_step(xbc, qkv, sm, z, cst_b, cst_c, h0, s0, p):
    bs = xbc.shape[0]
    block_b = _pick(bs, (4, 2, 1))
    per_b = lambda a: pl.BlockSpec((block_b,) + tuple(a.shape[1:]), lambda i: (i,) + (0,) * (len(a.shape) - 1))
    full = lambda a: pl.BlockSpec(a.shape, lambda i: (0,) * a.ndim)
    data = [xbc, qkv, sm, z, cst_b, cst_c, h0, s0]
    params = [p["ssm_conv_w"], p["ssm_conv_b"], p["gdn_conv_w"], p["dtb_small"], p["alog_small"],
              p["ssm_d_e"], p["ssm_norm_g"], p["gdn_norm_g"]]
    outs = [jax.ShapeDtypeStruct((bs, 1, D_B), F32), jax.ShapeDtypeStruct((bs, 1, D_C), F32),
            jax.ShapeDtypeStruct(cst_b.shape, F32), jax.ShapeDtypeStruct(cst_c.shape, F32),
            jax.ShapeDtypeStruct(h0.shape, F32), jax.ShapeDtypeStruct(s0.shape, F32)]
    return pl.pallas_call(
        functools.partial(_sample_step_kernel, n_data=len(data), n_params=len(params), block_b=block_b),
        grid=(bs // block_b,),
        in_specs=[per_b(a) for a in data] + [full(a) for a in params],
        out_specs=[per_b(a) for a in outs],
        out_shape=outs,
        compiler_params=_cparams("arbitrary"),
    )(*data, *params)


def _permute_w_in(w):
    sizes = (2 * H_A * DH_A, 2 * H_A * DH_A, D_A, CONV_B, H_B, CONV_C, H_C, H_C, D_MIX)
    offs = [0]
    for s in sizes:
        offs.append(offs[-1] + s)
    col = lambda i: w[:, offs[i]:offs[i + 1]]
    small = jnp.concatenate([col(4), col(6), col(7),
                             jnp.zeros((w.shape[0], SMALL_W - H_B - 2 * H_C), w.dtype)], axis=1)
    return jnp.concatenate([col(0), col(1), col(2), col(3), col(5), col(8), small], axis=1).astype(BF16)


def _layer_params(l, w):
    pad_small = lambda a_ssm, a_gdn: jnp.concatenate(
        [a_ssm, jnp.zeros((H_C,), F32), a_gdn, jnp.zeros((SMALL_W - H_B - 2 * H_C,), F32)]).reshape(1, SMALL_W)
    rep = lambda a: jnp.repeat(a, P_B).reshape(1, D_B)
    expand = (jnp.arange(SMALL_W)[:, None] == (jnp.arange(D_B)[None, :] // P_B)).astype(BF16)
    return {
        "ssm_conv_w": w["ssm_conv_w"][l], "ssm_conv_b": w["ssm_conv_b"][l].reshape(1, CONV_B),
        "gdn_conv_w": w["gdn_conv_w"][l],
        "dtb_small": pad_small(w["ssm_dt_bias"][l], w["gdn_dt_bias"][l]),
        "alog_small": pad_small(w["ssm_A_log"][l], w["gdn_A_log"][l]),
        "ssm_d_e": rep(w["ssm_D"][l]),
        "ssm_norm_g": w["ssm_norm_g"][l].reshape(1, D_B), "gdn_norm_g": w["gdn_norm_g"][l].reshape(1, DV_C),
        "expand": expand,
    }


def _pick(n, prefs):
    for t in prefs:
        if n % t == 0:
            return t
    return n


def kernel(x_prompt, x_sample, c_prompt, c_sample, cache_k, cache_v, state_ssm, state_ssm_conv, state_gdn, state_gdn_conv, page_table, w_ada, b_ada, w_in, w_out, rel_bias, attn_lambda, attn_subln_g, ssm_conv_w, ssm_conv_b, ssm_dt_bias, ssm_A_log, ssm_D, ssm_norm_g, gdn_conv_w, gdn_dt_bias, gdn_A_log, gdn_norm_g, final_norm_g):
    depth = w_in.shape[0]
    bp, L, _ = x_prompt.shape
    bs = x_sample.shape[0]
    n_pages = page_table.shape[1]
    past = n_pages * PAGE_SIZE
    n_pool = cache_k.shape[1]
    wts = dict(ssm_conv_w=ssm_conv_w, ssm_conv_b=ssm_conv_b, ssm_dt_bias=ssm_dt_bias, ssm_A_log=ssm_A_log,
               ssm_D=ssm_D, ssm_norm_g=ssm_norm_g, gdn_conv_w=gdn_conv_w, gdn_dt_bias=gdn_dt_bias,
               gdn_A_log=gdn_A_log, gdn_norm_g=gdn_norm_g)

    tm = _pick(L, (512, 256, 128, 64))
    tm_out = _pick(L, (512, 256, 128, 64))
    T = _pick(L, (256, 128))
    Q = _pick(L, (256, 128, 64))
    R = _pick(L, (256, 128, 64))
    G = _pick(n_pages, (16, 8, 4, 2, 1))

    mod = _modulation(jnp.concatenate([c_prompt, c_sample], axis=0), w_ada, b_ada)
    bias_near = _near_bias_tiles(rel_bias, T)
    bias_pages = _decode_bias(rel_bias, past)
    bias_new = jnp.broadcast_to(jnp.tile(rel_bias[0].astype(F32), 2)[:, None], (2 * H_A, DV_A))
    ck = cache_k.reshape(depth, n_pool, PAGE_SIZE * H_A, 2 * DH_A)
    cv = cache_v.reshape(depth, n_pool, PAGE_SIZE * H_A, DV_A)
    lane_half = jnp.arange(DV_A) < DH_A

    xp = x_prompt
    xs = x_sample.reshape(1, bs, D_MODEL)
    new_p, new_s = [], []
    for l in range(depth):
        lam_init = 0.8 - 0.6 * math.exp(-0.3 * l)
        p = _layer_params(l, wts)
        w_perm = _permute_w_in(w_in[l])
        w_o = w_out[l].astype(BF16)
        shift, scale, gate = jnp.split(mod[l], 3, axis=-1)
        mp = lambda a: a[:bp].reshape(bp, 1, D_MODEL)
        ms = lambda a: a[bp:].reshape(1, bs, D_MODEL)

        q, k, v, xbc, qkv, z, sm, k_heads, v_heads = _in_projection(xp, mp(scale), mp(shift), w_perm, tm, True)
        o_a = _prompt_attention(q, k, v, z, bias_near, attn_lambda[l], attn_subln_g[l], lam_init, T)
        o_b, ssm_h = _prompt_ssd(xbc, sm, z, p, Q)
        o_c, gdn_s = _prompt_gdn(qkv, sm, z, p, R)
        final = final_norm_g if l == depth - 1 else None
        xp = _out_projection(xp, mp(gate), o_a, o_b, o_c, w_o, final, tm_out)
        new_p.append((k_heads, v_heads,
                      ssm_h.reshape(bp, H_B, P_B, N_B), xbc[:, L - (CONV_W - 1):, :],
                      gdn_s.reshape(bp, H_C, DK_C, DV_C), qkv[:, L - (CONV_W - 1):, :]))

        q, k, v, xbc, qkv, z, sm = _in_projection(xs, ms(scale), ms(shift), w_perm, bs, False)
        q4 = q.reshape(bs, H_A, DV_A) * (DH_A ** -0.5)
        qpat = jnp.concatenate([jnp.where(lane_half, q4, 0.0), jnp.where(lane_half, 0.0, q4)], axis=1)
        k4 = k.reshape(bs, H_A, DV_A)
        v4 = v.reshape(bs, H_A, DV_A)
        o_a = _decode_attention(l, page_table, ck, cv, qpat, jnp.concatenate([k4, k4], axis=1),
                                jnp.concatenate([v4, v4], axis=1), bias_pages, bias_new,
                                z[..., :D_A].reshape(bs, H_A, DV_A), attn_lambda[l], attn_subln_g[l],
                                lam_init, G)
        row = lambda a: a.reshape(bs, 1, a.shape[-1])
        o_b, o_c, conv_b1, conv_c1, ssm_h, gdn_s = _sample_step(
            row(xbc), row(qkv), row(sm), row(z), state_ssm_conv[l], state_gdn_conv[l],
            state_ssm[l].reshape(bs, D_B, N_B), state_gdn[l].reshape(bs, H_C * DK_C, DV_C), p)
        xs = _out_projection(xs, ms(gate), o_a.reshape(1, bs, D_A).astype(BF16),
                             o_b.reshape(1, bs, D_B).astype(BF16), o_c.reshape(1, bs, D_C).astype(BF16),
                             w_o, final, bs)
        new_s.append((k.reshape(bs, 1, H_A, 2 * DH_A), v.reshape(bs, 1, H_A, DV_A),
                      ssm_h.reshape(bs, H_B, P_B, N_B), conv_b1,
                      gdn_s.reshape(bs, H_C, DK_C, DV_C), conv_c1))

    k_p, v_p, ssm_p, ssm_conv_p, gdn_p, gdn_conv_p = [jnp.stack(a) for a in zip(*new_p)]
    k_s, v_s, ssm_s, ssm_conv_s, gdn_s, gdn_conv_s = [jnp.stack(a) for a in zip(*new_s)]
    return (xp, xs.reshape(bs, 1, D_MODEL), k_p, v_p, ssm_p, ssm_conv_p, gdn_p, gdn_conv_p,
            k_s, v_s, ssm_s, ssm_conv_s, gdn_s, gdn_conv_s)
```

```python
import functools
import math

import jax
import jax.numpy as jnp
from jax import lax
from jax.experimental import pallas as pl
from jax.experimental.pallas import tpu as pltpu

F32 = jnp.float32
BF16 = jnp.bfloat16

D_MODEL = 1024
H_A, DH_A, DV_A = 4, 64, 128
D_A = H_A * DV_A
H_B, P_B, G_B, N_B = 8, 64, 2, 128
D_B = H_B * P_B
CONV_B = D_B + 2 * G_B * N_B
H_C, DK_C, DV_C = 4, 128, 128
D_C = H_C * DV_C
CONV_C = 2 * H_C * DK_C + D_C
CONV_W = 4
D_MIX = D_A + D_B + D_C
REL_BUCKETS, REL_MAX_DIST = 32, 128
PAGE_SIZE = 128
EPS = 1e-6
GDN_CHUNK = 64

LANES = 128
SUBLANES = 8
SMALL_W = LANES
DT_LANE, BETA_LANE, AG_LANE = 0, H_B, H_B + H_C
NEG_BIG = -1e30
LOG2E = math.log2(math.e)
ATTN_Q_SCALE = DH_A ** -0.5 * LOG2E
ATTN_STRIP = 64
VMEM_LIMIT = 56 * 1024 * 1024

SEGMENTS = (("q", 2 * H_A * DH_A), ("k", 2 * H_A * DH_A), ("v", D_A), ("xbc", CONV_B),
            ("qkv", CONV_C), ("z", D_MIX), ("small", SMALL_W))
SEGMENT_GROUPS = (("q", "k", "v", "xbc"), ("qkv",), ("z",), ("small",))


def _cparams(*sem):
    return pltpu.CompilerParams(dimension_semantics=sem, vmem_limit_bytes=VMEM_LIMIT)


def _silu(x):
    return x * jax.nn.sigmoid(x)


def _softplus(x):
    return jnp.maximum(x, 0.0) + jnp.log1p(jnp.exp(-jnp.abs(x)))


def _rms(x):
    return x * lax.rsqrt(jnp.mean(x * x, axis=-1, keepdims=True) + EPS)


def _dot_nt(a, b):
    return lax.dot_general(a, b, (((1,), (1,)), ((), ())), preferred_element_type=F32)


def _dot_tn(a, b):
    return lax.dot_general(a, b, (((0,), (0,)), ((), ())), preferred_element_type=F32)


def _dot(a, b):
    return jnp.dot(a, b, preferred_element_type=F32)


def _iota(shape, dim):
    return lax.broadcasted_iota(jnp.int32, shape, dim)


def _mod_kernel(c_ref, w_ref, b_ref, o_ref):
    sc = _silu(c_ref[...]).astype(BF16)
    o_ref[...] = _dot(sc, w_ref[...].astype(BF16)) + b_ref[...]


def _modulation(c_all, w_ada, b_ada):
    depth = w_ada.shape[0]
    n = c_all.shape[0]
    return pl.pallas_call(
        _mod_kernel,
        grid=(depth, 3),
        in_specs=[pl.BlockSpec((n, D_MODEL), lambda l, j: (0, 0)),
                  pl.BlockSpec((None, D_MODEL, D_MODEL), lambda l, j: (l, 0, j)),
                  pl.BlockSpec((None, 1, D_MODEL), lambda l, j: (l, 0, j))],
        out_specs=pl.BlockSpec((None, n, D_MODEL), lambda l, j: (l, 0, j)),
        out_shape=jax.ShapeDtypeStruct((depth, n, 3 * D_MODEL), F32),
        compiler_params=_cparams("arbitrary", "arbitrary"),
    )(c_all, w_ada, b_ada.reshape(depth, 1, 3 * D_MODEL))


def _inproj_kernel(x_ref, scale_ref, shift_ref, *rest, heads_out, n_carried):
    n_groups = len(SEGMENT_GROUPS)
    w_refs = rest[:n_groups]
    out_refs = rest[n_groups + n_carried:]
    h = _rms(x_ref[...]) * (1.0 + scale_ref[...]) + shift_ref[...]
    hb = h.astype(BF16)
    seg_refs = dict(zip((name for name, _ in SEGMENTS), out_refs))
    head_refs = dict(zip(("k", "v"), out_refs[len(SEGMENTS):])) if heads_out else {}
    widths = dict(SEGMENTS)
    for w_ref, names in zip(w_refs, SEGMENT_GROUPS):
        off = 0
        for name in names:
            n = widths[name]
            res = _dot(hb, w_ref[:, off:off + n])
            scaled = res * ATTN_Q_SCALE if heads_out and name == "q" else res
            seg_refs[name][...] = scaled.astype(seg_refs[name].dtype)
            if name in head_refs:
                for hd in range(H_A):
                    head_refs[name][:, hd, :] = res[:, hd * DV_A:(hd + 1) * DV_A]
            off += n


def _in_projection(x, scale, shift, w_groups, tm, cache_rows=None):
    b, L, _ = x.shape
    ms = scale.shape[1]
    mod_rows = tm if ms == L else 1
    mod_map = (lambda i, j: (i, j, 0)) if ms == L else (lambda i, j: (i, 0, 0))
    heads_out = cache_rows is not None
    dt = lambda name: BF16 if heads_out and name in ("q", "k", "v") else F32
    out_specs = [pl.BlockSpec((None, tm, w), lambda i, j: (i, j, 0)) for _, w in SEGMENTS]
    out_shape = [jax.ShapeDtypeStruct((b, L, w), dt(name)) for name, w in SEGMENTS]
    carried, aliases = (), {}
    if heads_out:
        layer, depth, prev = cache_rows
        out_specs += [pl.BlockSpec((None, None, tm, H_A, DV_A), lambda i, j: (layer, i, j, 0, 0))] * 2
        out_shape += [jax.ShapeDtypeStruct((depth, b, L, H_A, DV_A), F32)] * 2
        if prev is not None:
            carried = tuple(prev)
            aliases = {3 + len(w_groups) + n: len(SEGMENTS) + n for n in range(len(carried))}
    return pl.pallas_call(
        functools.partial(_inproj_kernel, heads_out=heads_out, n_carried=len(carried)),
        grid=(b, L // tm),
        in_specs=[pl.BlockSpec((None, tm, D_MODEL), lambda i, j: (i, j, 0)),
                  pl.BlockSpec((None, mod_rows, D_MODEL), mod_map),
                  pl.BlockSpec((None, mod_rows, D_MODEL), mod_map)]
                 + [pl.BlockSpec(w.shape, lambda i, j: (0, 0), pipeline_mode=pl.Buffered(1)) for w in w_groups]
                 + [pl.BlockSpec(memory_space=pl.ANY)] * len(carried),
        out_specs=out_specs,
        out_shape=out_shape,
        input_output_aliases=aliases,
        compiler_params=_cparams("arbitrary", "arbitrary"),
    )(x, scale, shift, *w_groups, *carried)


def _outproj_kernel(x_ref, gate_ref, oa_ref, ob_ref, oc_ref, w_ref, *rest, final):
    acc = _dot(oa_ref[...], w_ref[0:D_A, :])
    acc += _dot(ob_ref[...], w_ref[D_A:D_A + D_B, :])
    acc += _dot(oc_ref[...], w_ref[D_A + D_B:D_MIX, :])
    y = x_ref[...] + gate_ref[...] * acc
    if final:
        g_ref, o_ref = rest
        o_ref[...] = _rms(y) * g_ref[...]
    else:
        (o_ref,) = rest
        o_ref[...] = y


def _out_projection(x, gate, oa, ob, oc, w_out, final_g, tm):
    b, L, _ = x.shape
    ms = gate.shape[1]
    mod_rows = tm if ms == L else 1
    mod_map = (lambda i, j: (i, j, 0)) if ms == L else (lambda i, j: (i, 0, 0))
    row = lambda w: pl.BlockSpec((None, tm, w), lambda i, j: (i, j, 0))
    in_specs = [row(D_MODEL), pl.BlockSpec((None, mod_rows, D_MODEL), mod_map),
                row(D_A), row(D_B), row(D_C),
                pl.BlockSpec((D_MIX, D_MODEL), lambda i, j: (0, 0))]
    args = [x, gate, oa, ob, oc, w_out]
    final = final_g is not None
    if final:
        in_specs.append(pl.BlockSpec((1, D_MODEL), lambda i, j: (0, 0)))
        args.append(final_g.reshape(1, D_MODEL))
    return pl.pallas_call(
        functools.partial(_outproj_kernel, final=final),
        grid=(b, L // tm),
        in_specs=in_specs,
        out_specs=row(D_MODEL),
        out_shape=jax.ShapeDtypeStruct((b, L, D_MODEL), F32),
        compiler_params=_cparams("arbitrary", "arbitrary"),
    )(*args)


def _lambda_value(lp, lam_init):
    s01 = jnp.sum(lp[0:1, :] * lp[1:2, :], axis=-1, keepdims=True)
    s23 = jnp.sum(lp[2:3, :] * lp[3:4, :], axis=-1, keepdims=True)
    return jnp.exp(s01) - jnp.exp(s23) + lam_init


def _rel_bias_values(table, n):
    exact = REL_BUCKETS // 2
    large = exact + (jnp.log(jnp.maximum(n, 1).astype(F32) / exact)
                     / math.log(REL_MAX_DIST / exact) * (REL_BUCKETS - exact)).astype(jnp.int32)
    bucket = jnp.where(n < exact, n, jnp.minimum(large, REL_BUCKETS - 1))
    onehot = bucket[..., None, None] == jnp.arange(REL_BUCKETS)[:, None]
    return jnp.sum(jnp.where(onehot, table.astype(F32), 0.0), axis=-2)


def _attn_kernel(q_ref, kb_scr, vb_scr, z_ref, bias_ref, lamp_ref, subln_ref, o_ref, s_bufs, w_bufs,
                 *, T, lam_init):
    L = q_ref.shape[0]
    nq = L // T
    lam = _lambda_value(lamp_ref[...], lam_init)
    lane = _iota((T, DV_A), 1)
    half = LANES

    def logit_tasks(i):
        q = q_ref[i * T:(i + 1) * T, :]
        s_scr = s_bufs.at[i % 2]
        tasks = []
        for sub in range(2):
            qm = jnp.where((lane < DH_A) if sub == 0 else (lane >= DH_A), q, jnp.zeros_like(q))
            for j in range(i + 1):
                def task(qm=qm, sub=sub, j=j):
                    s = _dot_nt(qm, kb_scr[j * T:(j + 1) * T, :])
                    if j == i:
                        s = s + bias_ref[:, T:2 * T]
                    elif j == i - 1:
                        s = s + bias_ref[:, 0:T]
                    s_scr[sub, j] = s
                tasks.append(task)
        return tasks

    def softmax_strip(i, r0):
        s_scr, w_scr = s_bufs.at[i % 2], w_bufs.at[i % 2]
        strip = slice(r0, r0 + ATTN_STRIP)
        scale = []
        for sub in range(2):
            mx = None
            for j in range(i + 1):
                s = s_scr[sub, j, strip, :]
                for c0 in range(0, T, half):
                    t = s[:, c0:c0 + half]
                    mx = t if mx is None else jnp.maximum(mx, t)
            m_row = jnp.max(mx, axis=-1, keepdims=True)
            ls = None
            for j in range(i + 1):
                p = jnp.exp2(s_scr[sub, j, strip, :] - m_row)
                s_scr[sub, j, strip, :] = p
                for c0 in range(0, T, half):
                    t = p[:, c0:c0 + half]
                    ls = t if ls is None else ls + t
            scale.append(1.0 / jnp.sum(ls, axis=-1, keepdims=True))
        w0, w1 = scale[0], lam * scale[1]
        for j in range(i + 1):
            w_scr[strip, j * T:(j + 1) * T] = (s_scr[0, j, strip, :] * w0
                                               - s_scr[1, j, strip, :] * w1).astype(BF16)

    for task in logit_tasks(0):
        task()
    for i in range(nq):
        nxt = logit_tasks(i + 1) if i + 1 < nq else []
        strips = list(range(0, T, ATTN_STRIP))
        per = -(-len(nxt) // len(strips))
        for n, r0 in enumerate(strips):
            softmax_strip(i, r0)
            for task in nxt[n * per:(n + 1) * per]:
                task()
        rows = slice(i * T, (i + 1) * T)
        o = _dot(w_bufs[i % 2, :, 0:(i + 1) * T], vb_scr[0:(i + 1) * T, :])
        o = _rms(o) * subln_ref[...] * (1.0 - lam_init) * _silu(z_ref[rows, :])
        o_ref[rows, :] = o.astype(o_ref.dtype)


def _prompt_attention(q, k, v, z, bias_near, lam_p, subln, lam_init, T):
    b, L, _ = q.shape
    seq = lambda: pl.BlockSpec((None, L, DV_A), lambda i, h: (i, 0, h))
    return pl.pallas_call(
        functools.partial(_attn_kernel, T=T, lam_init=lam_init),
        grid=(b, H_A),
        in_specs=[seq(), seq(), seq(), seq(),
                  pl.BlockSpec((None, T, 2 * T), lambda i, h: (h, 0, 0)),
                  pl.BlockSpec((4, DH_A), lambda i, h: (0, 0)),
                  pl.BlockSpec((1, DV_A), lambda i, h: (0, 0))],
        out_specs=seq(),
        out_shape=jax.ShapeDtypeStruct((b, L, D_A), BF16),
        scratch_shapes=[pltpu.VMEM((2, 2, L // T, T, T), F32), pltpu.VMEM((2, T, L), BF16)],
        compiler_params=_cparams("arbitrary", "arbitrary"),
    )(q, k, v, z, bias_near, lam_p, subln.reshape(1, DV_A))


def _near_bias_tiles(table, T):
    period = 3 * T
    k = jnp.arange(period)
    n = jnp.where(k < 2 * T, T - k, T + period - k)
    vals = _rel_bias_values(table, jnp.maximum(n, 0)) - table[REL_BUCKETS - 1].astype(F32)
    w = jnp.where((n >= 0)[:, None], vals * LOG2E, NEG_BIG).T
    skew = jnp.tile(w, (1, T + 1))[:, :T * (period - 1)].reshape(H_A, T, period - 1)
    return skew[:, :, :2 * T]


def _decode_kernel(pt_ref, qpat_ref, knew_ref, vnew_ref, bias_ref, biasnew_ref, z_ref, lamp_ref,
                   subln_ref, *rest, G, lam_init):
    del pt_ref
    k_refs, v_refs = rest[:G], rest[G:2 * G]
    o_ref, m_scr, l_scr, acc_scr = rest[2 * G:]
    j = pl.program_id(1)

    @pl.when(j == 0)
    def _():
        m_scr[...] = jnp.full(m_scr.shape, NEG_BIG, F32)
        l_scr[...] = jnp.zeros(l_scr.shape, F32)
        acc_scr[...] = jnp.zeros(acc_scr.shape, F32)

    qf = qpat_ref[...]
    qb = qf.astype(BF16)
    s = [_dot_nt(qb, k_refs[g][...].astype(BF16)) + bias_ref[g] for g in range(G)]
    m_old = m_scr[...]
    m_tile = s[0]
    for g in range(1, G):
        m_tile = jnp.maximum(m_tile, s[g])
    m = jnp.maximum(m_old, jnp.max(m_tile, axis=-1, keepdims=True))
    alpha = jnp.exp(m_old - m)
    p = [jnp.exp(s[g] - m) for g in range(G)]
    p_sum = p[0]
    for g in range(1, G):
        p_sum = p_sum + p[g]
    l = alpha * l_scr[...] + jnp.sum(p_sum, axis=-1, keepdims=True)
    pv = _dot(p[0].astype(BF16), v_refs[0][...].astype(BF16))
    for g in range(1, G):
        pv = pv + _dot(p[g].astype(BF16), v_refs[g][...].astype(BF16))
    acc = alpha * acc_scr[...] + pv
    m_scr[...] = m
    l_scr[...] = l
    acc_scr[...] = acc

    @pl.when(j == pl.num_programs(1) - 1)
    def _():
        s_new = jnp.sum(qf * knew_ref[...], axis=-1, keepdims=True) + biasnew_ref[:, 0:1]
        m_new = jnp.maximum(m, s_new)
        alpha = jnp.exp(m - m_new)
        p_new = jnp.exp(s_new - m_new)
        l_fin = alpha * l + p_new
        out = (alpha * acc + p_new * vnew_ref[...]) / l_fin
        lam = _lambda_value(lamp_ref[...], lam_init)
        o = out[0:H_A, :] - lam * out[H_A:2 * H_A, :]
        o_ref[...] = _rms(o) * subln_ref[...] * (1.0 - lam_init) * _silu(z_ref[...])


def _decode_attention(layer, page_table, cache_k, cache_v, qpat, knew, vnew, bias_pages, bias_new, z3,
                      lam_p, subln, lam_init, G):
    bs, n_pages = page_table.shape
    rows = PAGE_SIZE * H_A

    def page_spec(g):
        return pl.BlockSpec((None, None, rows, DV_A),
                            lambda i, j, pt: (layer, pt[i, j * G + g], 0, 0))

    per_b = lambda r: pl.BlockSpec((None, r, DV_A), lambda i, j, pt: (i, 0, 0))
    grid_spec = pltpu.PrefetchScalarGridSpec(
        num_scalar_prefetch=1,
        grid=(bs, n_pages // G),
        in_specs=[per_b(2 * H_A), per_b(2 * H_A), per_b(2 * H_A),
                  pl.BlockSpec((G, 2 * H_A, rows), lambda i, j, pt: (j, 0, 0)),
                  pl.BlockSpec((2 * H_A, DV_A), lambda i, j, pt: (0, 0)),
                  per_b(H_A),
                  pl.BlockSpec((4, DH_A), lambda i, j, pt: (0, 0)),
                  pl.BlockSpec((1, DV_A), lambda i, j, pt: (0, 0))]
                 + [page_spec(g) for g in range(G)] + [page_spec(g) for g in range(G)],
        out_specs=per_b(H_A),
        scratch_shapes=[pltpu.VMEM((2 * H_A, 1), F32), pltpu.VMEM((2 * H_A, 1), F32),
                        pltpu.VMEM((2 * H_A, DV_A), F32)],
    )
    return pl.pallas_call(
        functools.partial(_decode_kernel, G=G, lam_init=lam_init),
        grid_spec=grid_spec,
        out_shape=jax.ShapeDtypeStruct((bs, H_A, DV_A), F32),
        compiler_params=_cparams("arbitrary", "arbitrary"),
    )(page_table, qpat, knew, vnew, bias_pages, bias_new, z3, lam_p, subln.reshape(1, DV_A),
      *([cache_k] * G), *([cache_v] * G))


def _decode_bias(table, past):
    n = past - jnp.arange(past)
    vals = _rel_bias_values(table, n).reshape(past // PAGE_SIZE, PAGE_SIZE, H_A)
    vals = jnp.moveaxis(vals, -1, 1)[..., None]
    same = (jnp.arange(H_A)[:, None, None] == jnp.arange(H_A)[None, None, :])
    full = jnp.where(same[None], vals, NEG_BIG).reshape(past // PAGE_SIZE, H_A, PAGE_SIZE * H_A)
    return jnp.concatenate([full, full], axis=1)


def _conv_taps(xp_ref, x, w_ref, rows):
    xp_ref[SUBLANES:SUBLANES + rows, :] = x
    xe = xp_ref[...]
    t = w_ref[0:1, :] * xe
    for i in range(1, CONV_W):
        t = w_ref[i:i + 1, :] * xe + pltpu.roll(t, 1, axis=0)
    xp_ref[0:SUBLANES, :] = x[rows - SUBLANES:rows, :]
    return t[SUBLANES:SUBLANES + rows, :]


def _ssd_kernel(xbc_ref, sm_ref, z_ref, cw_ref, cb_ref, dtb_s_ref, alog_s_ref,
                dskip_ref, ng_ref, exp_ref, y_ref, hout_ref, xp_scr, h_scr, *, Q):
    c = pl.program_id(1)

    @pl.when(c == 0)
    def _():
        xp_scr[0:SUBLANES, :] = jnp.zeros((SUBLANES, CONV_B), F32)
        h_scr[...] = jnp.zeros(h_scr.shape, F32)

    xa = _silu(_conv_taps(xp_scr, xbc_ref[...], cw_ref, Q) + cb_ref[...])
    xs = xa[:, 0:D_B]
    sm = sm_ref[...]
    row = _iota((Q, Q), 0)
    col = _iota((Q, Q), 1)
    tril = row >= col
    expand = exp_ref[...]

    dt_s = _softplus(sm + dtb_s_ref[...])
    a_s = dt_s * (-jnp.exp(alog_s_ref[...]))
    cum_s = _dot_exact_rhs(tril.astype(BF16), a_s)
    cum_t = cum_s.T
    dt_e = _dot_exact_lhs(dt_s, expand)
    cum_e = _dot_exact_lhs(cum_s, expand)
    xc = xs * dt_e
    dec_end = jnp.exp(cum_e[Q - 1:Q, :] - cum_e)
    dec_in = jnp.exp(cum_e)
    xc_b = xc.astype(BF16)
    xe_b = (xc * dec_end).astype(BF16)
    lane = _iota((Q, LANES), 1)
    prow = _iota((LANES, LANES), 0)
    heads_per_group = H_B // G_B
    pairs = []
    for g in range(G_B):
        bm = xa[:, D_B + g * N_B:D_B + (g + 1) * N_B].astype(BF16)
        cm = xa[:, D_B + G_B * N_B + g * N_B:D_B + G_B * N_B + (g + 1) * N_B].astype(BF16)
        cb = _dot_nt(cm, bm)
        for pr in range(g * heads_per_group // 2, (g + 1) * heads_per_group // 2):
            lo, hi = pr * LANES, (pr + 1) * LANES
            yd = []
            for hh in (2 * pr, 2 * pr + 1):
                seg = cum_s[:, hh:hh + 1] - cum_t[hh:hh + 1, :]
                mat = cb * jnp.exp(jnp.where(tril, seg, -jnp.inf))
                yd.append(_dot(mat.astype(BF16), xc_b[:, lo:hi]))
            y_diag = jnp.where(lane < P_B, yd[0], yd[1])
            hp = h_scr[lo:hi, :]
            y_off = _dot_nt(cm, hp.astype(BF16)) * dec_in[:, lo:hi]
            st = _dot_tn(xe_b[:, lo:hi], bm)
            a_last = jnp.where(prow < P_B, cum_t[2 * pr:2 * pr + 1, Q - 1:Q],
                               cum_t[2 * pr + 1:2 * pr + 2, Q - 1:Q])
            h_scr[lo:hi, :] = hp * jnp.exp(a_last) + st
            pairs.append(y_diag + y_off + xs[:, lo:hi] * dskip_ref[:, lo:hi])
    y = jnp.concatenate(pairs, axis=-1) * _silu(z_ref[...])
    gw = D_B // G_B
    y = jnp.concatenate([_rms(y[:, g * gw:(g + 1) * gw]) for g in range(G_B)], axis=-1) * ng_ref[...]
    y_ref[...] = y.astype(y_ref.dtype)

    @pl.when(c == pl.num_programs(1) - 1)
    def _():
        hout_ref[...] = h_scr[...]


def _prompt_ssd(xbc, sm, z, p, Q):
    b, L, _ = xbc.shape
    full = lambda a: pl.BlockSpec(a.shape, lambda i, j: (0,) * a.ndim)
    params = [p["ssm_conv_w"], p["ssm_conv_b"], p["dtb_small"], p["alog_small"],
              p["ssm_d_e"], p["ssm_norm_g"], p["expand"]]
    return pl.pallas_call(
        functools.partial(_ssd_kernel, Q=Q),
        grid=(b, L // Q),
        in_specs=[pl.BlockSpec((None, Q, CONV_B), lambda i, j: (i, j, 0)),
                  pl.BlockSpec((None, Q, SMALL_W), lambda i, j: (i, j, 0)),
                  pl.BlockSpec((None, Q, D_B), lambda i, j: (i, j, D_A // D_B))]
                 + [full(a) for a in params],
        out_specs=[pl.BlockSpec((None, Q, D_B), lambda i, j: (i, j, 0)),
                   pl.BlockSpec((None, D_B, N_B), lambda i, j: (i, 0, 0))],
        out_shape=[jax.ShapeDtypeStruct((b, L, D_B), BF16),
                   jax.ShapeDtypeStruct((b, D_B, N_B), F32)],
        scratch_shapes=[pltpu.VMEM((Q + SUBLANES, CONV_B), F32), pltpu.VMEM((D_B, N_B), F32)],
        compiler_params=_cparams("arbitrary", "arbitrary"),
    )(xbc, sm, z, *params)


def _split2(a):
    hi = a.astype(BF16)
    return hi, (a - hi.astype(F32)).astype(BF16)


def _split3(x):
    x1 = x.astype(BF16)
    r1 = x - x1.astype(F32)
    x2 = r1.astype(BF16)
    return x1, x2, (r1 - x2.astype(F32)).astype(BF16)


def _dot_exact_rhs(a_bf16, x):
    x1, x2, x3 = _split3(x)
    return _dot(a_bf16, x1) + (_dot(a_bf16, x2) + _dot(a_bf16, x3))


def _dot_exact_lhs(x, a_bf16):
    x1, x2, x3 = _split3(x)
    return _dot(x1, a_bf16) + (_dot(x2, a_bf16) + _dot(x3, a_bf16))


def _gdn_head_prepare(act, gcum, gcum_t, beta_s, r0, h):
    cs = GDN_CHUNK
    rows = slice(r0, r0 + cs)
    q = act[rows, h * DK_C:(h + 1) * DK_C]
    k = act[rows, D_C + h * DK_C:D_C + (h + 1) * DK_C]
    v = act[rows, 2 * D_C + h * DV_C:2 * D_C + (h + 1) * DV_C]
    q = q * lax.rsqrt(jnp.sum(q * q, axis=-1, keepdims=True) + EPS) * (DK_C ** -0.5)
    k = k * lax.rsqrt(jnp.sum(k * k, axis=-1, keepdims=True) + EPS)
    beta = beta_s[rows, BETA_LANE + h:BETA_LANE + h + 1]
    gc = gcum[rows, AG_LANE + h:AG_LANE + h + 1]
    gr = gcum_t[AG_LANE + h:AG_LANE + h + 1, rows]
    g_last = gcum[r0 + cs - 1:r0 + cs, AG_LANE + h:AG_LANE + h + 1]
    ri = _iota((cs, 2 * cs), 0)
    ci = _iota((cs, 2 * cs), 1) & (cs - 1)
    decay = jnp.exp(jnp.where(ri >= ci, gc - jnp.concatenate([gr, gr], axis=1), -jnp.inf))
    kb = k.astype(BF16)
    kk = _dot_nt(kb, jnp.concatenate([kb, kb], axis=0))
    p = jnp.where(ri > ci, -(beta * kk * decay), 0.0)
    rhs = jnp.concatenate([v * beta, k * (beta * jnp.exp(gc))], axis=-1)
    attn = (_dot_nt(q.astype(BF16), kb) * decay[:, 0:cs]).astype(BF16)
    qg = q * jnp.exp(gc)
    kd = (k * jnp.exp(g_last - gc)).astype(BF16)
    return p, rhs, attn, qg, kd, jnp.exp(g_last)


def _gdn_head_finish(sol, attn, qg, kd, e_last):
    x_b = sol.astype(BF16)
    ax = _dot(attn, x_b)
    kx = _dot_tn(kd, x_b)
    lhs = jnp.concatenate([kx[:, DV_C:2 * DV_C], qg - ax[:, DV_C:2 * DV_C]], axis=0).astype(BF16)
    return lhs, kx[:, 0:DV_C], ax[:, 0:DV_C], e_last


def _split_lhs(p):
    hi, lo = _split2(p)
    half = jnp.where(_iota(p.shape, 1) < GDN_CHUNK, hi, lo)
    return jnp.concatenate([half, half], axis=1)


def _split_rhs(x):
    hi, lo = _split2(x)
    return jnp.concatenate([hi, hi, lo, lo], axis=0)


def _neumann_solve(ps, xs):
    n_levels = GDN_CHUNK.bit_length() - 1
    for lvl in range(n_levels):
        lhs = [_split_lhs(p) for p in ps]
        xs = [x + _dot(a, _split_rhs(x)) for a, x in zip(lhs, xs)]
        if lvl + 1 < n_levels:
            ps = [_dot(a, _split_rhs(p)) for a, p in zip(lhs, ps)]
    return xs


def _gdn_kernel(qkv_ref, sm_ref, z_ref, cw_ref, dtb_s_ref, alog_s_ref, ng_ref, o_ref, sout_ref,
                xp_scr, s_scr, act_scr, *, R):
    c = pl.program_id(1)

    @pl.when(c == 0)
    def _():
        xp_scr[0:SUBLANES, :] = jnp.zeros((SUBLANES, CONV_C), F32)
        s_scr[...] = jnp.zeros(s_scr.shape, F32)

    cs = GDN_CHUNK
    act_scr[...] = _silu(_conv_taps(xp_scr, qkv_ref[...], cw_ref, R))
    sm = sm_ref[...]
    g_s = _softplus(sm + dtb_s_ref[...]) * (-jnp.exp(alog_s_ref[...]))
    beta_s = jax.nn.sigmoid(sm)
    ri = _iota((R, R), 0)
    ci = _iota((R, R), 1)
    chunk_tril = (lax.shift_right_logical(ri, 6) == lax.shift_right_logical(ci, 6)) & (ri >= ci)
    gcum = _dot_exact_rhs(chunk_tril.astype(BF16), g_s)
    gcum_t = gcum.T

    n_chunks = R // cs
    def state_free_part(chunks):
        prep = [_gdn_head_prepare(act_scr, gcum, gcum_t, beta_s, ch * cs, h) for ch in chunks for h in range(H_C)]
        sols = _neumann_solve([pr[0] for pr in prep], [pr[1] for pr in prep])
        return [_gdn_head_finish(sol, *pr[2:]) for sol, pr in zip(sols, prep)]

    def state_step(ch, heads):
        outs = []
        for h, (lhs, c_add, o_add, e_last) in enumerate(heads):
            s_old = s_scr[h * DK_C:(h + 1) * DK_C, :]
            r = _dot(lhs, s_old.astype(BF16))
            s_scr[h * DK_C:(h + 1) * DK_C, :] = s_old * e_last + (c_add - r[0:DK_C, :])
            outs.append(_rms(r[DK_C:DK_C + cs, :] + o_add) * ng_ref[...])
        rows = slice(ch * cs, (ch + 1) * cs)
        o_ref[rows, :] = (jnp.concatenate(outs, axis=-1) * _silu(z_ref[rows, :])).astype(o_ref.dtype)

    half = max(n_chunks // 2, 1)
    for chunks in (range(0, half), range(half, n_chunks)):
        parts = state_free_part(chunks)
        for n, ch in enumerate(chunks):
            state_step(ch, parts[n * H_C:(n + 1) * H_C])

    @pl.when(c == pl.num_programs(1) - 1)
    def _():
        sout_ref[...] = s_scr[...]


def _prompt_gdn(qkv, sm, z, p, R):
    b, L, _ = qkv.shape
    full = lambda a: pl.BlockSpec(a.shape, lambda i, j: (0,) * a.ndim)
    params = [p["gdn_conv_w"], p["dtb_small"], p["alog_small"], p["gdn_norm_g"]]
    return pl.pallas_call(
        functools.partial(_gdn_kernel, R=R),
        grid=(b, L // R),
        in_specs=[pl.BlockSpec((None, R, CONV_C), lambda i, j: (i, j, 0)),
                  pl.BlockSpec((None, R, SMALL_W), lambda i, j: (i, j, 0)),
                  pl.BlockSpec((None, R, D_C), lambda i, j: (i, j, (D_A + D_B) // D_C))]
                 + [full(a) for a in params],
        out_specs=[pl.BlockSpec((None, R, D_C), lambda i, j: (i, j, 0)),
                   pl.BlockSpec((None, H_C * DK_C, DV_C), lambda i, j: (i, 0, 0))],
        out_shape=[jax.ShapeDtypeStruct((b, L, D_C), BF16),
                   jax.ShapeDtypeStruct((b, H_C * DK_C, DV_C), F32)],
        scratch_shapes=[pltpu.VMEM((R + SUBLANES, CONV_C), F32),
                        pltpu.VMEM((H_C * DK_C, DV_C), F32),
                        pltpu.VMEM((R, CONV_C), F32)],
        compiler_params=_cparams("arbitrary", "arbitrary"),
    )(qkv, sm, z, *params)


def _expand_heads(vals, n_heads, width):
    lane = _iota((1, n_heads * width), 1)
    out = jnp.zeros((1, n_heads * width), F32)
    for h in range(n_heads):
        out = jnp.where((lane >= h * width) & (lane < (h + 1) * width), vals[:, h:h + 1], out)
    return out


def _pad_rows(rows):
    n = rows[0].shape[-1]
    r = _iota((SUBLANES, n), 0)
    out = jnp.zeros((SUBLANES, n), F32)
    for i, v in enumerate(rows):
        out = jnp.where(r == i, v, out)
    return out


def _sample_step_kernel(*refs, n_data, n_params, block_b):
    data, params, outs = refs[:n_data], refs[n_data:n_data + n_params], refs[n_data + n_params:]
    for bi in range(block_b):
        _sample_step_one(*[r.at[bi] for r in data], *params, *[r.at[bi] for r in outs])


def _sample_step_one(xbc_ref, qkv_ref, sm_ref, z_ref, cst_b_ref, cst_c_ref, h0_ref, s0_ref,
                     cwb_ref, cbb_ref, cwc_ref, dtb_s_ref, alog_s_ref, dskip_ref, ngb_ref, ngc_ref,
                     ob_ref, oc_ref, cnew_b_ref, cnew_c_ref, h1_ref, s1_ref):
    def conv(x, st_ref, w_ref, new_ref):
        y = w_ref[CONV_W - 1:CONV_W, :] * x
        for i in range(CONV_W - 1):
            y = y + w_ref[i:i + 1, :] * st_ref[i:i + 1, :]
        for i in range(CONV_W - 2):
            new_ref[i:i + 1, :] = st_ref[i + 1:i + 2, :]
        new_ref[CONV_W - 2:CONV_W - 1, :] = x
        return y

    sm = sm_ref[...]
    dt_or_sp = _softplus(sm + dtb_s_ref[...])
    a_s = dt_or_sp * (-jnp.exp(alog_s_ref[...]))
    z = z_ref[...]

    xa = _silu(conv(xbc_ref[...], cst_b_ref, cwb_ref, cnew_b_ref) + cbb_ref[...])
    xs = xa[:, 0:D_B]
    dt_e = _expand_heads(dt_or_sp[:, DT_LANE:DT_LANE + H_B], H_B, P_B)
    a_e = _expand_heads(a_s[:, DT_LANE:DT_LANE + H_B], H_B, P_B)
    xc = xs * dt_e
    half = D_B // G_B
    lane_b = _iota((1, D_B), 1)
    bm = [xa[:, D_B + g * N_B:D_B + (g + 1) * N_B] for g in range(G_B)]
    cm = [xa[:, D_B + G_B * N_B + g * N_B:D_B + G_B * N_B + (g + 1) * N_B] for g in range(G_B)]
    h0 = h0_ref[...]
    coff = _dot_nt(_pad_rows(cm).astype(BF16), h0.astype(BF16))
    y_off = jnp.where(lane_b < half, coff[0:1, :], coff[1:2, :]) * jnp.exp(a_e)
    cb = [jnp.sum(cm[g] * bm[g], axis=-1, keepdims=True) for g in range(G_B)]
    y_diag = jnp.where(lane_b < half, cb[0], cb[1]) * xc
    y = (y_diag + y_off + xs * dskip_ref[...]) * _silu(z[:, D_A:D_A + D_B])
    y = jnp.concatenate([_rms(y[:, g * half:(g + 1) * half]) for g in range(G_B)], axis=-1)
    ob_ref[...] = y * ngb_ref[...]
    xc_rows = _pad_rows([jnp.where(lane_b < half, xc, 0.0), jnp.where(lane_b >= half, xc, 0.0)])
    outer = _dot_tn(xc_rows.astype(BF16), _pad_rows(bm).astype(BF16))
    state_row = _iota((D_B, N_B), 0)
    dec = jnp.exp(a_s)
    dec_col = jnp.zeros((D_B, N_B), F32)
    for h in range(H_B):
        in_head = (state_row >= h * P_B) & (state_row < (h + 1) * P_B)
        dec_col = jnp.where(in_head, dec[:, DT_LANE + h:DT_LANE + h + 1], dec_col)
    h1_ref[...] = h0 * dec_col + outer

    qkv = _silu(conv(qkv_ref[...], cst_c_ref, cwc_ref, cnew_c_ref))
    beta_s = jax.nn.sigmoid(sm)
    outs = []
    for h in range(H_C):
        q = qkv[:, h * DK_C:(h + 1) * DK_C]
        k = qkv[:, D_C + h * DK_C:D_C + (h + 1) * DK_C]
        v = qkv[:, 2 * D_C + h * DV_C:2 * D_C + (h + 1) * DV_C]
        q = q * lax.rsqrt(jnp.sum(q * q, axis=-1, keepdims=True) + EPS) * (DK_C ** -0.5)
        k = k * lax.rsqrt(jnp.sum(k * k, axis=-1, keepdims=True) + EPS)
        beta = beta_s[:, BETA_LANE + h:BETA_LANE + h + 1]
        eg = jnp.exp(a_s[:, AG_LANE + h:AG_LANE + h + 1])
        s_old = s0_ref[h * DK_C:(h + 1) * DK_C, :]
        ws = _dot(_pad_rows([k * (beta * eg), q * eg]).astype(BF16), s_old.astype(BF16))
        v_new = v * beta - ws[0:1, :]
        o = ws[1:2, :] + jnp.sum(q * k, axis=-1, keepdims=True) * v_new
        upd = _dot_tn(_pad_rows([k]).astype(BF16), _pad_rows([v_new]).astype(BF16))
        s1_ref[h * DK_C:(h + 1) * DK_C, :] = s_old * eg + upd
        outs.append(_rms(o) * ngc_ref[...])
    oc_ref[...] = jnp.concatenate(outs, axis=-1) * _silu(z[:, D_A + D_B:D_MIX])


def _sample_step(xbc, qkv, sm, z, cst_b, cst_c, h0, s0, p):
    bs = xbc.shape[0]
    block_b = _pick(bs, (4, 2, 1))
    per_b = lambda a: pl.BlockSpec((block_b,) + tuple(a.shape[1:]), lambda i: (i,) + (0,) * (len(a.shape) - 1))
    full = lambda a: pl.BlockSpec(a.shape, lambda i: (0,) * a.ndim)
    data = [xbc, qkv, sm, z, cst_b, cst_c, h0, s0]
    params = [p["ssm_conv_w"], p["ssm_conv_b"], p["gdn_conv_w"], p["dtb_small"], p["alog_small"],
              p["ssm_d_e"], p["ssm_norm_g"], p["gdn_norm_g"]]
    outs = [jax.ShapeDtypeStruct((bs, 1, D_B), F32), jax.ShapeDtypeStruct((bs, 1, D_C), F32),
            jax.ShapeDtypeStruct(cst_b.shape, F32), jax.ShapeDtypeStruct(cst_c.shape, F32),
            jax.ShapeDtypeStruct(h0.shape, F32), jax.ShapeDtypeStruct(s0.shape, F32)]
    return pl.pallas_call(
        functools.partial(_sample_step_kernel, n_data=len(data), n_params=len(params), block_b=block_b),
        grid=(bs // block_b,),
        in_specs=[per_b(a) for a in data] + [full(a) for a in params],
        out_specs=[per_b(a) for a in outs],
        out_shape=outs,
        compiler_params=_cparams("arbitrary"),
    )(*data, *params)


def _split_w_in(w):
    sizes = (2 * H_A * DH_A, 2 * H_A * DH_A, D_A, CONV_B, H_B, CONV_C, H_C, H_C, D_MIX)
    offs = [0]
    for s in sizes:
        offs.append(offs[-1] + s)
    col = lambda i, j=None: w[:, offs[i]:offs[(i if j is None else j) + 1]].astype(BF16)
    small = jnp.concatenate([col(4), col(6), col(7),
                             jnp.zeros((w.shape[0], SMALL_W - H_B - 2 * H_C), BF16)], axis=1)
    return [col(0, 3), col(5), col(8), small]


def _layer_params(l, w):
    pad_small = lambda a_ssm, a_gdn: jnp.concatenate(
        [a_ssm, jnp.zeros((H_C,), F32), a_gdn, jnp.zeros((SMALL_W - H_B - 2 * H_C,), F32)]).reshape(1, SMALL_W)
    rep = lambda a: jnp.repeat(a, P_B).reshape(1, D_B)
    expand = (jnp.arange(SMALL_W)[:, None] == (jnp.arange(D_B)[None, :] // P_B)).astype(BF16)
    return {
        "ssm_conv_w": w["ssm_conv_w"][l], "ssm_conv_b": w["ssm_conv_b"][l].reshape(1, CONV_B),
        "gdn_conv_w": w["gdn_conv_w"][l],
        "dtb_small": pad_small(w["ssm_dt_bias"][l], w["gdn_dt_bias"][l]),
        "alog_small": pad_small(w["ssm_A_log"][l], w["gdn_A_log"][l]),
        "ssm_d_e": rep(w["ssm_D"][l]),
        "ssm_norm_g": w["ssm_norm_g"][l].reshape(1, D_B), "gdn_norm_g": w["gdn_norm_g"][l].reshape(1, DV_C),
        "expand": expand,
    }


def _pick(n, prefs):
    for t in prefs:
        if n % t == 0:
            return t
    return n


def kernel(x_prompt, x_sample, c_prompt, c_sample, cache_k, cache_v, state_ssm, state_ssm_conv, state_gdn, state_gdn_conv, page_table, w_ada, b_ada, w_in, w_out, rel_bias, attn_lambda, attn_subln_g, ssm_conv_w, ssm_conv_b, ssm_dt_bias, ssm_A_log, ssm_D, ssm_norm_g, gdn_conv_w, gdn_dt_bias, gdn_A_log, gdn_norm_g, final_norm_g):
    depth = w_in.shape[0]
    bp, L, _ = x_prompt.shape
    bs = x_sample.shape[0]
    n_pages = page_table.shape[1]
    past = n_pages * PAGE_SIZE
    n_pool = cache_k.shape[1]
    wts = dict(ssm_conv_w=ssm_conv_w, ssm_conv_b=ssm_conv_b, ssm_dt_bias=ssm_dt_bias, ssm_A_log=ssm_A_log,
               ssm_D=ssm_D, ssm_norm_g=ssm_norm_g, gdn_conv_w=gdn_conv_w, gdn_dt_bias=gdn_dt_bias,
               gdn_A_log=gdn_A_log, gdn_norm_g=gdn_norm_g)

    tm = _pick(L, (512, 256, 128, 64))
    tm_out = _pick(L, (512, 256, 128, 64))
    T = _pick(L, (256, 128))
    Q = _pick(L, (256, 128, 64))
    R = _pick(L, (256, 128, 64))
    G = _pick(n_pages, (32, 16, 8, 4, 2, 1))

    mod = _modulation(jnp.concatenate([c_prompt, c_sample], axis=0), w_ada, b_ada)
    bias_near = _near_bias_tiles(rel_bias, T)
    bias_pages = _decode_bias(rel_bias, past)
    bias_new = jnp.broadcast_to(jnp.tile(rel_bias[0].astype(F32), 2)[:, None], (2 * H_A, DV_A))
    ck = cache_k.reshape(depth, n_pool, PAGE_SIZE * H_A, 2 * DH_A)
    cv = cache_v.reshape(depth, n_pool, PAGE_SIZE * H_A, DV_A)
    lane_half = jnp.arange(DV_A) < DH_A

    xp = x_prompt
    xs = x_sample.reshape(1, bs, D_MODEL)
    new_p, new_s = [], []
    kv_rows = None
    for l in range(depth):
        lam_init = 0.8 - 0.6 * math.exp(-0.3 * l)
        p = _layer_params(l, wts)
        w_perm = _split_w_in(w_in[l])
        w_o = w_out[l].astype(BF16)
        shift, scale, gate = jnp.split(mod[l], 3, axis=-1)
        mp = lambda a: a[:bp].reshape(bp, 1, D_MODEL)
        ms = lambda a: a[bp:].reshape(1, bs, D_MODEL)

        q, k, v, xbc, qkv, z, sm, *kv_rows = _in_projection(xp, mp(scale), mp(shift), w_perm, tm,
                                                            (l, depth, kv_rows))
        o_a = _prompt_attention(q, k, v, z, bias_near, attn_lambda[l], attn_subln_g[l], lam_init, T)
        o_b, ssm_h = _prompt_ssd(xbc, sm, z, p, Q)
        o_c, gdn_s = _prompt_gdn(qkv, sm, z, p, R)
        final = final_norm_g if l == depth - 1 else None
        xp = _out_projection(xp, mp(gate), o_a, o_b, o_c, w_o, final, tm_out)
        new_p.append((ssm_h.reshape(bp, H_B, P_B, N_B), xbc[:, L - (CONV_W - 1):, :],
                      gdn_s.reshape(bp, H_C, DK_C, DV_C), qkv[:, L - (CONV_W - 1):, :]))

        q, k, v, xbc, qkv, z, sm = _in_projection(xs, ms(scale), ms(shift), w_perm, bs)
        q4 = q.reshape(bs, H_A, DV_A) * (DH_A ** -0.5)
        qpat = jnp.concatenate([jnp.where(lane_half, q4, 0.0), jnp.where(lane_half, 0.0, q4)], axis=1)
        k4 = k.reshape(bs, H_A, DV_A)
        v4 = v.reshape(bs, H_A, DV_A)
        o_a = _decode_attention(l, page_table, ck, cv, qpat, jnp.concatenate([k4, k4], axis=1),
                                jnp.concatenate([v4, v4], axis=1), bias_pages, bias_new,
                                z[..., :D_A].reshape(bs, H_A, DV_A), attn_lambda[l], attn_subln_g[l],
                                lam_init, G)
        row = lambda a: a.reshape(bs, 1, a.shape[-1])
        o_b, o_c, conv_b1, conv_c1, ssm_h, gdn_s = _sample_step(
            row(xbc), row(qkv), row(sm), row(z), state_ssm_conv[l], state_gdn_conv[l],
            state_ssm[l].reshape(bs, D_B, N_B), state_gdn[l].reshape(bs, H_C * DK_C, DV_C), p)
        xs = _out_projection(xs, ms(gate), o_a.reshape(1, bs, D_A).astype(BF16),
                             o_b.reshape(1, bs, D_B).astype(BF16), o_c.reshape(1, bs, D_C).astype(BF16),
                             w_o, final, bs)
        new_s.append((k.reshape(bs, 1, H_A, 2 * DH_A), v.reshape(bs, 1, H_A, DV_A),
                      ssm_h.reshape(bs, H_B, P_B, N_B), conv_b1,
                      gdn_s.reshape(bs, H_C, DK_C, DV_C), conv_c1))

    k_p, v_p = kv_rows
    ssm_p, ssm_conv_p, gdn_p, gdn_conv_p = [jnp.stack(a) for a in zip(*new_p)]
    k_s, v_s, ssm_s, ssm_conv_s, gdn_s, gdn_conv_s = [jnp.stack(a) for a in zip(*new_s)]
    return (xp, xs.reshape(bs, 1, D_MODEL), k_p, v_p, ssm_p, ssm_conv_p, gdn_p, gdn_conv_p,
            k_s, v_s, ssm_s, ssm_conv_s, gdn_s, gdn_conv_s)
```

```python
import functools
import math

import jax
import jax.numpy as jnp
from jax import lax
from jax.experimental import pallas as pl
from jax.experimental.pallas import tpu as pltpu

F32 = jnp.float32
BF16 = jnp.bfloat16

D_MODEL = 1024
H_A, DH_A, DV_A = 4, 64, 128
D_A = H_A * DV_A
H_B, P_B, G_B, N_B = 8, 64, 2, 128
D_B = H_B * P_B
CONV_B = D_B + 2 * G_B * N_B
H_C, DK_C, DV_C = 4, 128, 128
D_C = H_C * DV_C
CONV_C = 2 * H_C * DK_C + D_C
CONV_W = 4
D_MIX = D_A + D_B + D_C
REL_BUCKETS, REL_MAX_DIST = 32, 128
PAGE_SIZE = 128
EPS = 1e-6
GDN_CHUNK = 64

LANES = 128
SUBLANES = 8
SMALL_W = LANES
DT_LANE, BETA_LANE, AG_LANE = 0, H_B, H_B + H_C
NEG_BIG = -1e30
LOG2E = math.log2(math.e)
ATTN_Q_SCALE = DH_A ** -0.5 * LOG2E
ATTN_STRIP = 64
MAX_PAGES_PER_STEP = 32
VMEM_LIMIT = 56 * 1024 * 1024

SEGMENTS = (("q", 2 * H_A * DH_A), ("k", 2 * H_A * DH_A), ("v", D_A), ("xbc", CONV_B),
            ("qkv", CONV_C), ("z", D_MIX), ("small", SMALL_W))
SEGMENT_GROUPS = (("q", "k", "v", "xbc"), ("qkv",), ("z",), ("small",))


def _cparams(*sem):
    return pltpu.CompilerParams(dimension_semantics=sem, vmem_limit_bytes=VMEM_LIMIT)


def _silu(x):
    return x * jax.nn.sigmoid(x)


def _softplus(x):
    return jnp.maximum(x, 0.0) + jnp.log1p(jnp.exp(-jnp.abs(x)))


def _rms(x):
    return x * lax.rsqrt(jnp.mean(x * x, axis=-1, keepdims=True) + EPS)


def _dot_nt(a, b):
    return lax.dot_general(a, b, (((1,), (1,)), ((), ())), preferred_element_type=F32)


def _dot_tn(a, b):
    return lax.dot_general(a, b, (((0,), (0,)), ((), ())), preferred_element_type=F32)


def _dot(a, b):
    return jnp.dot(a, b, preferred_element_type=F32)


def _iota(shape, dim):
    return lax.broadcasted_iota(jnp.int32, shape, dim)


def _mod_kernel(c_ref, w_ref, b_ref, o_ref):
    sc = _silu(c_ref[...]).astype(BF16)
    o_ref[...] = _dot(sc, w_ref[...].astype(BF16)) + b_ref[...]


def _modulation(c_all, w_ada, b_ada):
    depth = w_ada.shape[0]
    n = c_all.shape[0]
    return pl.pallas_call(
        _mod_kernel,
        grid=(depth, 3),
        in_specs=[pl.BlockSpec((n, D_MODEL), lambda l, j: (0, 0)),
                  pl.BlockSpec((None, D_MODEL, D_MODEL), lambda l, j: (l, 0, j)),
                  pl.BlockSpec((None, 1, D_MODEL), lambda l, j: (l, 0, j))],
        out_specs=pl.BlockSpec((None, n, D_MODEL), lambda l, j: (l, 0, j)),
        out_shape=jax.ShapeDtypeStruct((depth, n, 3 * D_MODEL), F32),
        compiler_params=_cparams("arbitrary", "arbitrary"),
    )(c_all, w_ada, b_ada.reshape(depth, 1, 3 * D_MODEL))


def _inproj_kernel(x_ref, scale_ref, shift_ref, *rest, heads_out, n_carried):
    n_groups = len(SEGMENT_GROUPS)
    w_refs = rest[:n_groups]
    out_refs = rest[n_groups + n_carried:]
    h = _rms(x_ref[...]) * (1.0 + scale_ref[...]) + shift_ref[...]
    hb = h.astype(BF16)
    seg_refs = dict(zip((name for name, _ in SEGMENTS), out_refs))
    head_refs = dict(zip(("k", "v"), out_refs[len(SEGMENTS):])) if heads_out else {}
    widths = dict(SEGMENTS)
    for w_ref, names in zip(w_refs, SEGMENT_GROUPS):
        off = 0
        for name in names:
            n = widths[name]
            res = _dot(hb, w_ref[:, off:off + n])
            scaled = res * ATTN_Q_SCALE if heads_out and name == "q" else res
            seg_refs[name][...] = scaled.astype(seg_refs[name].dtype)
            if name in head_refs:
                for hd in range(H_A):
                    head_refs[name][:, hd, :] = res[:, hd * DV_A:(hd + 1) * DV_A]
            off += n


def _in_projection(x, scale, shift, w_groups, tm, cache_rows=None):
    b, L, _ = x.shape
    ms = scale.shape[1]
    mod_rows = tm if ms == L else 1
    mod_map = (lambda i, j: (i, j, 0)) if ms == L else (lambda i, j: (i, 0, 0))
    heads_out = cache_rows is not None
    dt = lambda name: BF16 if heads_out and name in ("q", "k", "v") else F32
    out_specs = [pl.BlockSpec((None, tm, w), lambda i, j: (i, j, 0)) for _, w in SEGMENTS]
    out_shape = [jax.ShapeDtypeStruct((b, L, w), dt(name)) for name, w in SEGMENTS]
    carried, aliases = (), {}
    if heads_out:
        layer, depth, prev = cache_rows
        out_specs += [pl.BlockSpec((None, None, tm, H_A, DV_A), lambda i, j: (layer, i, j, 0, 0))] * 2
        out_shape += [jax.ShapeDtypeStruct((depth, b, L, H_A, DV_A), F32)] * 2
        if prev is not None:
            carried = tuple(prev)
            aliases = {3 + len(w_groups) + n: len(SEGMENTS) + n for n in range(len(carried))}
    return pl.pallas_call(
        functools.partial(_inproj_kernel, heads_out=heads_out, n_carried=len(carried)),
        grid=(b, L // tm),
        in_specs=[pl.BlockSpec((None, tm, D_MODEL), lambda i, j: (i, j, 0)),
                  pl.BlockSpec((None, mod_rows, D_MODEL), mod_map),
                  pl.BlockSpec((None, mod_rows, D_MODEL), mod_map)]
                 + [pl.BlockSpec(w.shape, lambda i, j: (0, 0), pipeline_mode=pl.Buffered(1)) for w in w_groups]
                 + [pl.BlockSpec(memory_space=pl.ANY)] * len(carried),
        out_specs=out_specs,
        out_shape=out_shape,
        input_output_aliases=aliases,
        compiler_params=_cparams("arbitrary", "arbitrary"),
    )(x, scale, shift, *w_groups, *carried)


def _outproj_kernel(x_ref, gate_ref, oa_ref, ob_ref, oc_ref, w_ref, *rest, final):
    acc = _dot(oa_ref[...], w_ref[0:D_A, :])
    acc += _dot(ob_ref[...], w_ref[D_A:D_A + D_B, :])
    acc += _dot(oc_ref[...], w_ref[D_A + D_B:D_MIX, :])
    y = x_ref[...] + gate_ref[...] * acc
    if final:
        g_ref, o_ref = rest
        o_ref[...] = _rms(y) * g_ref[...]
    else:
        (o_ref,) = rest
        o_ref[...] = y


def _out_projection(x, gate, oa, ob, oc, w_out, final_g, tm):
    b, L, _ = x.shape
    ms = gate.shape[1]
    mod_rows = tm if ms == L else 1
    mod_map = (lambda i, j: (i, j, 0)) if ms == L else (lambda i, j: (i, 0, 0))
    row = lambda w: pl.BlockSpec((None, tm, w), lambda i, j: (i, j, 0))
    in_specs = [row(D_MODEL), pl.BlockSpec((None, mod_rows, D_MODEL), mod_map),
                row(D_A), row(D_B), row(D_C),
                pl.BlockSpec((D_MIX, D_MODEL), lambda i, j: (0, 0))]
    args = [x, gate, oa, ob, oc, w_out]
    final = final_g is not None
    if final:
        in_specs.append(pl.BlockSpec((1, D_MODEL), lambda i, j: (0, 0)))
        args.append(final_g.reshape(1, D_MODEL))
    return pl.pallas_call(
        functools.partial(_outproj_kernel, final=final),
        grid=(b, L // tm),
        in_specs=in_specs,
        out_specs=row(D_MODEL),
        out_shape=jax.ShapeDtypeStruct((b, L, D_MODEL), F32),
        compiler_params=_cparams("arbitrary", "arbitrary"),
    )(*args)


def _lambda_value(lp, lam_init):
    s01 = jnp.sum(lp[0:1, :] * lp[1:2, :], axis=-1, keepdims=True)
    s23 = jnp.sum(lp[2:3, :] * lp[3:4, :], axis=-1, keepdims=True)
    return jnp.exp(s01) - jnp.exp(s23) + lam_init


def _rel_bias_values(table, n):
    exact = REL_BUCKETS // 2
    large = exact + (jnp.log(jnp.maximum(n, 1).astype(F32) / exact)
                     / math.log(REL_MAX_DIST / exact) * (REL_BUCKETS - exact)).astype(jnp.int32)
    bucket = jnp.where(n < exact, n, jnp.minimum(large, REL_BUCKETS - 1))
    onehot = bucket[..., None, None] == jnp.arange(REL_BUCKETS)[:, None]
    return jnp.sum(jnp.where(onehot, table.astype(F32), 0.0), axis=-2)


def _attn_kernel(q_ref, kb_scr, vb_scr, z_ref, bias_ref, lamp_ref, subln_ref, o_ref, s_bufs, w_bufs,
                 *, T, lam_init):
    L = q_ref.shape[0]
    nq = L // T
    lam = _lambda_value(lamp_ref[...], lam_init)
    lane = _iota((T, DV_A), 1)
    half = LANES

    def logit_tasks(i):
        q = q_ref[i * T:(i + 1) * T, :]
        s_scr = s_bufs.at[i % 2]
        tasks = []
        for sub in range(2):
            qm = jnp.where((lane < DH_A) if sub == 0 else (lane >= DH_A), q, jnp.zeros_like(q))
            for j in range(i + 1):
                def task(qm=qm, sub=sub, j=j):
                    s = _dot_nt(qm, kb_scr[j * T:(j + 1) * T, :])
                    if j == i:
                        s = s + bias_ref[:, T:2 * T]
                    elif j == i - 1:
                        s = s + bias_ref[:, 0:T]
                    s_scr[sub, j] = s
                tasks.append(task)
        return tasks

    def softmax_strip(i, r0):
        s_scr, w_scr = s_bufs.at[i % 2], w_bufs.at[i % 2]
        strip = slice(r0, r0 + ATTN_STRIP)
        scale = []
        for sub in range(2):
            mx = None
            for j in range(i + 1):
                s = s_scr[sub, j, strip, :]
                for c0 in range(0, T, half):
                    t = s[:, c0:c0 + half]
                    mx = t if mx is None else jnp.maximum(mx, t)
            m_row = jnp.max(mx, axis=-1, keepdims=True)
            ls = None
            for j in range(i + 1):
                p = jnp.exp2(s_scr[sub, j, strip, :] - m_row)
                s_scr[sub, j, strip, :] = p
                for c0 in range(0, T, half):
                    t = p[:, c0:c0 + half]
                    ls = t if ls is None else ls + t
            scale.append(1.0 / jnp.sum(ls, axis=-1, keepdims=True))
        w0, w1 = scale[0], lam * scale[1]
        for j in range(i + 1):
            w_scr[strip, j * T:(j + 1) * T] = (s_scr[0, j, strip, :] * w0
                                               - s_scr[1, j, strip, :] * w1).astype(BF16)

    for task in logit_tasks(0):
        task()
    for i in range(nq):
        nxt = logit_tasks(i + 1) if i + 1 < nq else []
        strips = list(range(0, T, ATTN_STRIP))
        per = -(-len(nxt) // len(strips))
        for n, r0 in enumerate(strips):
            softmax_strip(i, r0)
            for task in nxt[n * per:(n + 1) * per]:
                task()
        rows = slice(i * T, (i + 1) * T)
        o = _dot(w_bufs[i % 2, :, 0:(i + 1) * T], vb_scr[0:(i + 1) * T, :])
        o = _rms(o) * subln_ref[...] * (1.0 - lam_init) * _silu(z_ref[rows, :])
        o_ref[rows, :] = o.astype(o_ref.dtype)


def _prompt_attention(q, k, v, z, bias_near, lam_p, subln, lam_init, T):
    b, L, _ = q.shape
    seq = lambda: pl.BlockSpec((None, L, DV_A), lambda i, h: (i, 0, h))
    return pl.pallas_call(
        functools.partial(_attn_kernel, T=T, lam_init=lam_init),
        grid=(b, H_A),
        in_specs=[seq(), seq(), seq(), seq(),
                  pl.BlockSpec((None, T, 2 * T), lambda i, h: (h, 0, 0)),
                  pl.BlockSpec((4, DH_A), lambda i, h: (0, 0)),
                  pl.BlockSpec((1, DV_A), lambda i, h: (0, 0))],
        out_specs=seq(),
        out_shape=jax.ShapeDtypeStruct((b, L, D_A), BF16),
        scratch_shapes=[pltpu.VMEM((2, 2, L // T, T, T), F32), pltpu.VMEM((2, T, L), BF16)],
        compiler_params=_cparams("arbitrary", "arbitrary"),
    )(q, k, v, z, bias_near, lam_p, subln.reshape(1, DV_A))


def _near_bias_tiles(table, T):
    period = 3 * T
    k = jnp.arange(period)
    n = jnp.where(k < 2 * T, T - k, T + period - k)
    vals = _rel_bias_values(table, jnp.maximum(n, 0)) - table[REL_BUCKETS - 1].astype(F32)
    w = jnp.where((n >= 0)[:, None], vals * LOG2E, NEG_BIG).T
    skew = jnp.tile(w, (1, T + 1))[:, :T * (period - 1)].reshape(H_A, T, period - 1)
    return skew[:, :, :2 * T]


N_DECODE_IN = 8


def _decode_stages(in_refs, o_ref, scratch, first, G, lam_init):
    qpat_ref, knew_ref, vnew_ref, bias_ref, biasnew_ref, z_ref, lamp_ref, subln_ref = in_refs[:N_DECODE_IN]
    k_refs, v_refs = in_refs[N_DECODE_IN:N_DECODE_IN + G], in_refs[N_DECODE_IN + G:N_DECODE_IN + 2 * G]
    m_scr, l_scr, acc_scr = scratch

    @pl.when(first)
    def _():
        m_scr[...] = jnp.full(m_scr.shape, NEG_BIG, F32)
        l_scr[...] = jnp.zeros(l_scr.shape, F32)
        acc_scr[...] = jnp.zeros(acc_scr.shape, F32)

    qf = qpat_ref[...]
    qb = qf.astype(BF16)
    st = {"s": [None] * G, "pv": None}

    def logits(g):
        st["s"][g] = _dot_nt(qb, k_refs[g][...].astype(BF16)) + bias_ref[g]

    def softmax_stats():
        s = st["s"]
        m_old, l_old = m_scr[...], l_scr[...]
        m_tile = s[0]
        for g in range(1, G):
            m_tile = jnp.maximum(m_tile, s[g])
        m = jnp.maximum(m_old, jnp.max(m_tile, axis=-1, keepdims=True))
        st["alpha"] = jnp.exp(m_old - m)
        p = [jnp.exp(s[g] - m) for g in range(G)]
        p_sum = p[0]
        for g in range(1, G):
            p_sum = p_sum + p[g]
        st["p"] = [x.astype(BF16) for x in p]
        st["m"] = m
        st["l"] = st["alpha"] * l_old + jnp.sum(p_sum, axis=-1, keepdims=True)

    def values(g):
        d = _dot(st["p"][g], v_refs[g][...].astype(BF16))
        st["pv"] = d if st["pv"] is None else st["pv"] + d

    def finish():
        m, l = st["m"], st["l"]
        acc = st["alpha"] * acc_scr[...] + st["pv"]
        m_scr[...] = m
        l_scr[...] = l
        acc_scr[...] = acc
        s_new = jnp.sum(qf * knew_ref[...], axis=-1, keepdims=True) + biasnew_ref[:, 0:1]
        m_new = jnp.maximum(m, s_new)
        alpha = jnp.exp(m - m_new)
        p_new = jnp.exp(s_new - m_new)
        l_fin = alpha * l + p_new
        out = (alpha * acc + p_new * vnew_ref[...]) / l_fin
        lam = _lambda_value(lamp_ref[...], lam_init)
        o = out[0:H_A, :] - lam * out[H_A:2 * H_A, :]
        o_ref[...] = _rms(o) * subln_ref[...] * (1.0 - lam_init) * _silu(z_ref[...])

    return ([functools.partial(logits, g) for g in range(G)] + [softmax_stats]
            + [functools.partial(values, g) for g in range(G)] + [finish])


class _Interleaver:
    def __init__(self, thunks, n_ticks):
        self.thunks, self.n_ticks, self.ticks, self.done = thunks, n_ticks, 0, 0

    def tick(self):
        self.ticks += 1
        target = min(len(self.thunks), -(-len(self.thunks) * self.ticks // self.n_ticks))
        while self.done < target:
            self.thunks[self.done]()
            self.done += 1

    def finish(self):
        self.ticks = self.n_ticks - 1
        self.tick()


def _decode_operands(layer, G, flat_step, page_table, cache_k, cache_v, qpat, knew, vnew, bias_pages,
                     bias_new, z3, lam_p, subln):
    bs, n_pages = page_table.shape
    spb = n_pages // G
    rows = PAGE_SIZE * H_A
    if spb & (spb - 1) == 0:
        seq = lambda ids: lax.shift_right_logical(flat_step(*ids), spb.bit_length() - 1)
        grp = lambda ids: flat_step(*ids) & (spb - 1)
    else:
        seq = lambda ids: flat_step(*ids) // spb
        grp = lambda ids: flat_step(*ids) % spb

    def page_spec(g):
        return pl.BlockSpec((None, None, rows, DV_A), lambda *a: (layer, a[-1][flat_step(*a[:-1]), g], 0, 0))

    per_b = lambda r: pl.BlockSpec((None, r, DV_A), lambda *a: (seq(a[:-1]), 0, 0))
    const = lambda shape: pl.BlockSpec(shape, lambda *a: (0,) * len(shape))
    in_specs = ([per_b(2 * H_A), per_b(2 * H_A), per_b(2 * H_A),
                 pl.BlockSpec((G, 2 * H_A, rows), lambda *a: (grp(a[:-1]), 0, 0)),
                 const((2 * H_A, DV_A)), per_b(H_A), const((4, DH_A)), const((1, DV_A))]
                + [page_spec(g) for g in range(G)] * 2)
    args = [qpat, knew, vnew, bias_pages, bias_new, z3, lam_p, subln.reshape(1, DV_A),
            *([cache_k] * G), *([cache_v] * G)]
    scratch = [pltpu.VMEM((2 * H_A, 1), F32), pltpu.VMEM((2 * H_A, 1), F32), pltpu.VMEM((2 * H_A, DV_A), F32)]
    steps_table = page_table.reshape(bs * spb, G)
    return steps_table, in_specs, args, per_b(H_A), jax.ShapeDtypeStruct((bs, H_A, DV_A), F32), scratch, spb


def _decode_kernel(pt_ref, *refs, G, spb, lam_init):
    del pt_ref
    n_in = N_DECODE_IN + 2 * G
    for stage in _decode_stages(refs[:n_in], refs[n_in], refs[n_in + 1:], pl.program_id(1) == 0, G, lam_init):
        stage()


def _decode_attention(layer, page_table, lam_init, G, *decode_args):
    bs, n_pages = page_table.shape
    steps_table, in_specs, args, out_spec, out_shape, scratch, spb = _decode_operands(
        layer, G, lambda i, j: i * (n_pages // G) + j, page_table, *decode_args)
    return pl.pallas_call(
        functools.partial(_decode_kernel, G=G, spb=spb, lam_init=lam_init),
        grid_spec=pltpu.PrefetchScalarGridSpec(num_scalar_prefetch=1, grid=(bs, spb), in_specs=in_specs,
                                               out_specs=out_spec, scratch_shapes=scratch),
        out_shape=out_shape,
        compiler_params=_cparams("arbitrary", "arbitrary"),
    )(steps_table, *args)


def _decode_bias(table, past):
    n = past - jnp.arange(past)
    vals = _rel_bias_values(table, n).reshape(past // PAGE_SIZE, PAGE_SIZE, H_A)
    vals = jnp.moveaxis(vals, -1, 1)[..., None]
    same = (jnp.arange(H_A)[:, None, None] == jnp.arange(H_A)[None, None, :])
    full = jnp.where(same[None], vals, NEG_BIG).reshape(past // PAGE_SIZE, H_A, PAGE_SIZE * H_A)
    return jnp.concatenate([full, full], axis=1)


def _conv_taps(xp_ref, x, w_ref, rows):
    xp_ref[SUBLANES:SUBLANES + rows, :] = x
    xe = xp_ref[...]
    t = w_ref[0:1, :] * xe
    for i in range(1, CONV_W):
        t = w_ref[i:i + 1, :] * xe + pltpu.roll(t, 1, axis=0)
    xp_ref[0:SUBLANES, :] = x[rows - SUBLANES:rows, :]
    return t[SUBLANES:SUBLANES + rows, :]


def _ssd_kernel(xbc_ref, sm_ref, z_ref, cw_ref, cb_ref, dtb_s_ref, alog_s_ref,
                dskip_ref, ng_ref, exp_ref, y_ref, hout_ref, xp_scr, h_scr, *, Q):
    c = pl.program_id(1)

    @pl.when(c == 0)
    def _():
        xp_scr[0:SUBLANES, :] = jnp.zeros((SUBLANES, CONV_B), F32)
        h_scr[...] = jnp.zeros(h_scr.shape, F32)

    xa = _silu(_conv_taps(xp_scr, xbc_ref[...], cw_ref, Q) + cb_ref[...])
    xs = xa[:, 0:D_B]
    sm = sm_ref[...]
    row = _iota((Q, Q), 0)
    col = _iota((Q, Q), 1)
    tril = row >= col
    expand = exp_ref[...]

    dt_s = _softplus(sm + dtb_s_ref[...])
    a_s = dt_s * (-jnp.exp(alog_s_ref[...]))
    cum_s = _dot_exact_rhs(tril.astype(BF16), a_s)
    cum_t = cum_s.T
    dt_e = _dot_exact_lhs(dt_s, expand)
    cum_e = _dot_exact_lhs(cum_s, expand)
    xc = xs * dt_e
    dec_end = jnp.exp(cum_e[Q - 1:Q, :] - cum_e)
    dec_in = jnp.exp(cum_e)
    xc_b = xc.astype(BF16)
    xe_b = (xc * dec_end).astype(BF16)
    lane = _iota((Q, LANES), 1)
    prow = _iota((LANES, LANES), 0)
    heads_per_group = H_B // G_B
    pairs = []
    for g in range(G_B):
        bm = xa[:, D_B + g * N_B:D_B + (g + 1) * N_B].astype(BF16)
        cm = xa[:, D_B + G_B * N_B + g * N_B:D_B + G_B * N_B + (g + 1) * N_B].astype(BF16)
        cb = _dot_nt(cm, bm)
        for pr in range(g * heads_per_group // 2, (g + 1) * heads_per_group // 2):
            lo, hi = pr * LANES, (pr + 1) * LANES
            yd = []
            for hh in (2 * pr, 2 * pr + 1):
                seg = cum_s[:, hh:hh + 1] - cum_t[hh:hh + 1, :]
                mat = cb * jnp.exp(jnp.where(tril, seg, -jnp.inf))
                yd.append(_dot(mat.astype(BF16), xc_b[:, lo:hi]))
            y_diag = jnp.where(lane < P_B, yd[0], yd[1])
            hp = h_scr[lo:hi, :]
            y_off = _dot_nt(cm, hp.astype(BF16)) * dec_in[:, lo:hi]
            st = _dot_tn(xe_b[:, lo:hi], bm)
            a_last = jnp.where(prow < P_B, cum_t[2 * pr:2 * pr + 1, Q - 1:Q],
                               cum_t[2 * pr + 1:2 * pr + 2, Q - 1:Q])
            h_scr[lo:hi, :] = hp * jnp.exp(a_last) + st
            pairs.append(y_diag + y_off + xs[:, lo:hi] * dskip_ref[:, lo:hi])
    y = jnp.concatenate(pairs, axis=-1) * _silu(z_ref[...])
    gw = D_B // G_B
    y = jnp.concatenate([_rms(y[:, g * gw:(g + 1) * gw]) for g in range(G_B)], axis=-1) * ng_ref[...]
    y_ref[...] = y.astype(y_ref.dtype)

    @pl.when(c == pl.num_programs(1) - 1)
    def _():
        hout_ref[...] = h_scr[...]


def _prompt_ssd(xbc, sm, z, p, Q):
    b, L, _ = xbc.shape
    full = lambda a: pl.BlockSpec(a.shape, lambda i, j: (0,) * a.ndim)
    params = [p["ssm_conv_w"], p["ssm_conv_b"], p["dtb_small"], p["alog_small"],
              p["ssm_d_e"], p["ssm_norm_g"], p["expand"]]
    return pl.pallas_call(
        functools.partial(_ssd_kernel, Q=Q),
        grid=(b, L // Q),
        in_specs=[pl.BlockSpec((None, Q, CONV_B), lambda i, j: (i, j, 0)),
                  pl.BlockSpec((None, Q, SMALL_W), lambda i, j: (i, j, 0)),
                  pl.BlockSpec((None, Q, D_B), lambda i, j: (i, j, D_A // D_B))]
                 + [full(a) for a in params],
        out_specs=[pl.BlockSpec((None, Q, D_B), lambda i, j: (i, j, 0)),
                   pl.BlockSpec((None, D_B, N_B), lambda i, j: (i, 0, 0))],
        out_shape=[jax.ShapeDtypeStruct((b, L, D_B), BF16),
                   jax.ShapeDtypeStruct((b, D_B, N_B), F32)],
        scratch_shapes=[pltpu.VMEM((Q + SUBLANES, CONV_B), F32), pltpu.VMEM((D_B, N_B), F32)],
        compiler_params=_cparams("arbitrary", "arbitrary"),
    )(xbc, sm, z, *params)


def _split2(a):
    hi = a.astype(BF16)
    return hi, (a - hi.astype(F32)).astype(BF16)


def _split3(x):
    x1 = x.astype(BF16)
    r1 = x - x1.astype(F32)
    x2 = r1.astype(BF16)
    return x1, x2, (r1 - x2.astype(F32)).astype(BF16)


def _dot_exact_rhs(a_bf16, x):
    x1, x2, x3 = _split3(x)
    return _dot(a_bf16, x1) + (_dot(a_bf16, x2) + _dot(a_bf16, x3))


def _dot_exact_lhs(x, a_bf16):
    x1, x2, x3 = _split3(x)
    return _dot(x1, a_bf16) + (_dot(x2, a_bf16) + _dot(x3, a_bf16))


def _gdn_head_prepare(act, gcum, gcum_t, beta_s, r0, h):
    cs = GDN_CHUNK
    rows = slice(r0, r0 + cs)
    q = act[rows, h * DK_C:(h + 1) * DK_C]
    k = act[rows, D_C + h * DK_C:D_C + (h + 1) * DK_C]
    v = act[rows, 2 * D_C + h * DV_C:2 * D_C + (h + 1) * DV_C]
    q = q * lax.rsqrt(jnp.sum(q * q, axis=-1, keepdims=True) + EPS) * (DK_C ** -0.5)
    k = k * lax.rsqrt(jnp.sum(k * k, axis=-1, keepdims=True) + EPS)
    beta = beta_s[rows, BETA_LANE + h:BETA_LANE + h + 1]
    gc = gcum[rows, AG_LANE + h:AG_LANE + h + 1]
    gr = gcum_t[AG_LANE + h:AG_LANE + h + 1, rows]
    g_last = gcum[r0 + cs - 1:r0 + cs, AG_LANE + h:AG_LANE + h + 1]
    ri = _iota((cs, 2 * cs), 0)
    ci = _iota((cs, 2 * cs), 1) & (cs - 1)
    decay = jnp.exp(jnp.where(ri >= ci, gc - jnp.concatenate([gr, gr], axis=1), -jnp.inf))
    kb = k.astype(BF16)
    kk = _dot_nt(kb, jnp.concatenate([kb, kb], axis=0))
    p = jnp.where(ri > ci, -(beta * kk * decay), 0.0)
    rhs = jnp.concatenate([v * beta, k * (beta * jnp.exp(gc))], axis=-1)
    attn = (_dot_nt(q.astype(BF16), kb) * decay[:, 0:cs]).astype(BF16)
    qg = q * jnp.exp(gc)
    kd = (k * jnp.exp(g_last - gc)).astype(BF16)
    return p, rhs, attn, qg, kd, jnp.exp(g_last)


def _gdn_head_finish(sol, attn, qg, kd, e_last):
    x_b = sol.astype(BF16)
    ax = _dot(attn, x_b)
    kx = _dot_tn(kd, x_b)
    lhs = jnp.concatenate([kx[:, DV_C:2 * DV_C], qg - ax[:, DV_C:2 * DV_C]], axis=0).astype(BF16)
    return lhs, kx[:, 0:DV_C], ax[:, 0:DV_C], e_last


def _split_lhs(p):
    hi, lo = _split2(p)
    half = jnp.where(_iota(p.shape, 1) < GDN_CHUNK, hi, lo)
    return jnp.concatenate([half, half], axis=1)


def _split_rhs(x):
    hi, lo = _split2(x)
    return jnp.concatenate([hi, hi, lo, lo], axis=0)


def _neumann_solve(ps, xs, tick):
    n_levels = GDN_CHUNK.bit_length() - 1
    for lvl in range(n_levels):
        lhs = [_split_lhs(p) for p in ps]
        xs = [x + _dot(a, _split_rhs(x)) for a, x in zip(lhs, xs)]
        if lvl + 1 < n_levels:
            ps = [_dot(a, _split_rhs(p)) for a, p in zip(lhs, ps)]
        tick()
    return xs


N_GDN_IN, N_GDN_OUT, N_GDN_SCRATCH = 7, 2, 3


def _gdn_kernel(*refs, R, rider=None):
    if rider is None:
        ins, refs = refs[:N_GDN_IN], refs[N_GDN_IN:]
        outs, scratch = refs[:N_GDN_OUT], refs[N_GDN_OUT:]
    else:
        G, spb, lam_init = rider
        n_dec = N_DECODE_IN + 2 * G
        refs = refs[1:]
        ins, dec_ins, refs = refs[:N_GDN_IN], refs[N_GDN_IN:N_GDN_IN + n_dec], refs[N_GDN_IN + n_dec:]
        outs, dec_out, refs = refs[:N_GDN_OUT], refs[N_GDN_OUT], refs[N_GDN_OUT + 1:]
        scratch, dec_scratch = refs[:N_GDN_SCRATCH], refs[N_GDN_SCRATCH:]
    qkv_ref, sm_ref, z_ref, cw_ref, dtb_s_ref, alog_s_ref, ng_ref = ins
    o_ref, sout_ref = outs
    xp_scr, s_scr, act_scr = scratch
    c = pl.program_id(1)

    @pl.when(c == 0)
    def _():
        xp_scr[0:SUBLANES, :] = jnp.zeros((SUBLANES, CONV_C), F32)
        s_scr[...] = jnp.zeros(s_scr.shape, F32)

    cs = GDN_CHUNK
    n_chunks = R // cs
    if rider is not None:
        step = pl.program_id(0) * pl.num_programs(1) + c
        stages = _decode_stages(dec_ins, dec_out, dec_scratch, lax.rem(step, spb) == 0, G, lam_init)
        n_ticks = n_chunks * (H_C + 1) + 2 * (GDN_CHUNK.bit_length() - 1)
        rider_stages = _Interleaver(stages, n_ticks)
    else:
        rider_stages = _Interleaver([], 1)
    tick = rider_stages.tick

    act_scr[...] = _silu(_conv_taps(xp_scr, qkv_ref[...], cw_ref, R))
    sm = sm_ref[...]
    g_s = _softplus(sm + dtb_s_ref[...]) * (-jnp.exp(alog_s_ref[...]))
    beta_s = jax.nn.sigmoid(sm)
    ri = _iota((R, R), 0)
    ci = _iota((R, R), 1)
    chunk_tril = (lax.shift_right_logical(ri, 6) == lax.shift_right_logical(ci, 6)) & (ri >= ci)
    gcum = _dot_exact_rhs(chunk_tril.astype(BF16), g_s)
    gcum_t = gcum.T

    def state_free_part(chunks):
        prep = []
        for ch in chunks:
            for h in range(H_C):
                prep.append(_gdn_head_prepare(act_scr, gcum, gcum_t, beta_s, ch * cs, h))
                tick()
        sols = _neumann_solve([pr[0] for pr in prep], [pr[1] for pr in prep], tick)
        return [_gdn_head_finish(sol, *pr[2:]) for sol, pr in zip(sols, prep)]

    def state_step(ch, heads):
        tick()
        outs = []
        for h, (lhs, c_add, o_add, e_last) in enumerate(heads):
            s_old = s_scr[h * DK_C:(h + 1) * DK_C, :]
            r = _dot(lhs, s_old.astype(BF16))
            s_scr[h * DK_C:(h + 1) * DK_C, :] = s_old * e_last + (c_add - r[0:DK_C, :])
            outs.append(_rms(r[DK_C:DK_C + cs, :] + o_add) * ng_ref[...])
        rows = slice(ch * cs, (ch + 1) * cs)
        o_ref[rows, :] = (jnp.concatenate(outs, axis=-1) * _silu(z_ref[rows, :])).astype(o_ref.dtype)

    half = max(n_chunks // 2, 1)
    for chunks in (range(0, half), range(half, n_chunks)):
        parts = state_free_part(chunks)
        for n, ch in enumerate(chunks):
            state_step(ch, parts[n * H_C:(n + 1) * H_C])
    rider_stages.finish()

    @pl.when(c == pl.num_programs(1) - 1)
    def _():
        sout_ref[...] = s_scr[...]


def _prompt_gdn(qkv, sm, z, p, R, decode=None):
    b, L, _ = qkv.shape
    nj = L // R
    full = lambda a: pl.BlockSpec(a.shape, lambda *_: (0,) * a.ndim)
    params = [p["gdn_conv_w"], p["dtb_small"], p["alog_small"], p["gdn_norm_g"]]
    in_specs = [pl.BlockSpec((None, R, CONV_C), lambda i, j, *_: (i, j, 0)),
                pl.BlockSpec((None, R, SMALL_W), lambda i, j, *_: (i, j, 0)),
                pl.BlockSpec((None, R, D_C), lambda i, j, *_: (i, j, (D_A + D_B) // D_C))] + [full(a) for a in params]
    out_specs = [pl.BlockSpec((None, R, D_C), lambda i, j, *_: (i, j, 0)),
                 pl.BlockSpec((None, H_C * DK_C, DV_C), lambda i, j, *_: (i, 0, 0))]
    out_shape = [jax.ShapeDtypeStruct((b, L, D_C), BF16), jax.ShapeDtypeStruct((b, H_C * DK_C, DV_C), F32)]
    scratch = [pltpu.VMEM((R + SUBLANES, CONV_C), F32), pltpu.VMEM((H_C * DK_C, DV_C), F32),
               pltpu.VMEM((R, CONV_C), F32)]
    args = [qkv, sm, z, *params]
    if decode is None:
        return pl.pallas_call(
            functools.partial(_gdn_kernel, R=R), grid=(b, nj), in_specs=in_specs, out_specs=out_specs,
            out_shape=out_shape, scratch_shapes=scratch, compiler_params=_cparams("arbitrary", "arbitrary"),
        )(*args)
    layer, page_table, lam_init, G, *decode_args = decode
    steps_table, d_in, d_args, d_out, d_shape, d_scratch, spb = _decode_operands(
        layer, G, lambda i, j: i * nj + j, page_table, *decode_args)
    return pl.pallas_call(
        functools.partial(_gdn_kernel, R=R, rider=(G, spb, lam_init)),
        grid_spec=pltpu.PrefetchScalarGridSpec(
            num_scalar_prefetch=1, grid=(b, nj), in_specs=in_specs + d_in, out_specs=out_specs + [d_out],
            scratch_shapes=scratch + d_scratch),
        out_shape=out_shape + [d_shape],
        compiler_params=_cparams("arbitrary", "arbitrary"),
    )(steps_table, *args, *d_args)


def _expand_heads(vals, n_heads, width):
    lane = _iota((1, n_heads * width), 1)
    out = jnp.zeros((1, n_heads * width), F32)
    for h in range(n_heads):
        out = jnp.where((lane >= h * width) & (lane < (h + 1) * width), vals[:, h:h + 1], out)
    return out


def _pad_rows(rows):
    n = rows[0].shape[-1]
    r = _iota((SUBLANES, n), 0)
    out = jnp.zeros((SUBLANES, n), F32)
    for i, v in enumerate(rows):
        out = jnp.where(r == i, v, out)
    return out


def _sample_step_kernel(*refs, n_data, n_params, block_b):
    data, params, outs = refs[:n_data], refs[n_data:n_data + n_params], refs[n_data + n_params:]
    for bi in range(block_b):
        _sample_step_one(*[r.at[bi] for r in data], *params, *[r.at[bi] for r in outs])


def _sample_step_one(xbc_ref, qkv_ref, sm_ref, z_ref, cst_b_ref, cst_c_ref, h0_ref, s0_ref,
                     cwb_ref, cbb_ref, cwc_ref, dtb_s_ref, alog_s_ref, dskip_ref, ngb_ref, ngc_ref,
                     ob_ref, oc_ref, cnew_b_ref, cnew_c_ref, h1_ref, s1_ref):
    def conv(x, st_ref, w_ref, new_ref):
        y = w_ref[CONV_W - 1:CONV_W, :] * x
        for i in range(CONV_W - 1):
            y = y + w_ref[i:i + 1, :] * st_ref[i:i + 1, :]
        for i in range(CONV_W - 2):
            new_ref[i:i + 1, :] = st_ref[i + 1:i + 2, :]
        new_ref[CONV_W - 2:CONV_W - 1, :] = x
        return y

    sm = sm_ref[...]
    dt_or_sp = _softplus(sm + dtb_s_ref[...])
    a_s = dt_or_sp * (-jnp.exp(alog_s_ref[...]))
    z = z_ref[...]

    xa = _silu(conv(xbc_ref[...], cst_b_ref, cwb_ref, cnew_b_ref) + cbb_ref[...])
    xs = xa[:, 0:D_B]
    dt_e = _expand_heads(dt_or_sp[:, DT_LANE:DT_LANE + H_B], H_B, P_B)
    a_e = _expand_heads(a_s[:, DT_LANE:DT_LANE + H_B], H_B, P_B)
    xc = xs * dt_e
    half = D_B // G_B
    lane_b = _iota((1, D_B), 1)
    bm = [xa[:, D_B + g * N_B:D_B + (g + 1) * N_B] for g in range(G_B)]
    cm = [xa[:, D_B + G_B * N_B + g * N_B:D_B + G_B * N_B + (g + 1) * N_B] for g in range(G_B)]
    h0 = h0_ref[...]
    coff = _dot_nt(_pad_rows(cm).astype(BF16), h0.astype(BF16))
    y_off = jnp.where(lane_b < half, coff[0:1, :], coff[1:2, :]) * jnp.exp(a_e)
    cb = [jnp.sum(cm[g] * bm[g], axis=-1, keepdims=True) for g in range(G_B)]
    y_diag = jnp.where(lane_b < half, cb[0], cb[1]) * xc
    y = (y_diag + y_off + xs * dskip_ref[...]) * _silu(z[:, D_A:D_A + D_B])
    y = jnp.concatenate([_rms(y[:, g * half:(g + 1) * half]) for g in range(G_B)], axis=-1)
    ob_ref[...] = y * ngb_ref[...]
    xc_rows = _pad_rows([jnp.where(lane_b < half, xc, 0.0), jnp.where(lane_b >= half, xc, 0.0)])
    outer = _dot_tn(xc_rows.astype(BF16), _pad_rows(bm).astype(BF16))
    state_row = _iota((D_B, N_B), 0)
    dec = jnp.exp(a_s)
    dec_col = jnp.zeros((D_B, N_B), F32)
    for h in range(H_B):
        in_head = (state_row >= h * P_B) & (state_row < (h + 1) * P_B)
        dec_col = jnp.where(in_head, dec[:, DT_LANE + h:DT_LANE + h + 1], dec_col)
    h1_ref[...] = h0 * dec_col + outer

    qkv = _silu(conv(qkv_ref[...], cst_c_ref, cwc_ref, cnew_c_ref))
    beta_s = jax.nn.sigmoid(sm)
    outs = []
    for h in range(H_C):
        q = qkv[:, h * DK_C:(h + 1) * DK_C]
        k = qkv[:, D_C + h * DK_C:D_C + (h + 1) * DK_C]
        v = qkv[:, 2 * D_C + h * DV_C:2 * D_C + (h + 1) * DV_C]
        q = q * lax.rsqrt(jnp.sum(q * q, axis=-1, keepdims=True) + EPS) * (DK_C ** -0.5)
        k = k * lax.rsqrt(jnp.sum(k * k, axis=-1, keepdims=True) + EPS)
        beta = beta_s[:, BETA_LANE + h:BETA_LANE + h + 1]
        eg = jnp.exp(a_s[:, AG_LANE + h:AG_LANE + h + 1])
        s_old = s0_ref[h * DK_C:(h + 1) * DK_C, :]
        ws = _dot(_pad_rows([k * (beta * eg), q * eg]).astype(BF16), s_old.astype(BF16))
        v_new = v * beta - ws[0:1, :]
        o = ws[1:2, :] + jnp.sum(q * k, axis=-1, keepdims=True) * v_new
        upd = _dot_tn(_pad_rows([k]).astype(BF16), _pad_rows([v_new]).astype(BF16))
        s1_ref[h * DK_C:(h + 1) * DK_C, :] = s_old * eg + upd
        outs.append(_rms(o) * ngc_ref[...])
    oc_ref[...] = jnp.concatenate(outs, axis=-1) * _silu(z[:, D_A + D_B:D_MIX])


def _sample_step(xbc, qkv, sm, z, cst_b, cst_c, h0, s0, p):
    bs = xbc.shape[0]
    block_b = _pick(bs, (4, 2, 1))
    per_b = lambda a: pl.BlockSpec((block_b,) + tuple(a.shape[1:]), lambda i: (i,) + (0,) * (len(a.shape) - 1))
    full = lambda a: pl.BlockSpec(a.shape, lambda i: (0,) * a.ndim)
    data = [xbc, qkv, sm, z, cst_b, cst_c, h0, s0]
    params = [p["ssm_conv_w"], p["ssm_conv_b"], p["gdn_conv_w"], p["dtb_small"], p["alog_small"],
              p["ssm_d_e"], p["ssm_norm_g"], p["gdn_norm_g"]]
    outs = [jax.ShapeDtypeStruct((bs, 1, D_B), F32), jax.ShapeDtypeStruct((bs, 1, D_C), F32),
            jax.ShapeDtypeStruct(cst_b.shape, F32), jax.ShapeDtypeStruct(cst_c.shape, F32),
            jax.ShapeDtypeStruct(h0.shape, F32), jax.ShapeDtypeStruct(s0.shape, F32)]
    return pl.pallas_call(
        functools.partial(_sample_step_kernel, n_data=len(data), n_params=len(params), block_b=block_b),
        grid=(bs // block_b,),
        in_specs=[per_b(a) for a in data] + [full(a) for a in params],
        out_specs=[per_b(a) for a in outs],
        out_shape=outs,
        compiler_params=_cparams("arbitrary"),
    )(*data, *params)


def _split_w_in(w):
    sizes = (2 * H_A * DH_A, 2 * H_A * DH_A, D_A, CONV_B, H_B, CONV_C, H_C, H_C, D_MIX)
    offs = [0]
    for s in sizes:
        offs.append(offs[-1] + s)
    col = lambda i, j=None: w[:, offs[i]:offs[(i if j is None else j) + 1]].astype(BF16)
    small = jnp.concatenate([col(4), col(6), col(7),
                             jnp.zeros((w.shape[0], SMALL_W - H_B - 2 * H_C), BF16)], axis=1)
    return [col(0, 3), col(5), col(8), small]


def _layer_params(l, w):
    pad_small = lambda a_ssm, a_gdn: jnp.concatenate(
        [a_ssm, jnp.zeros((H_C,), F32), a_gdn, jnp.zeros((SMALL_W - H_B - 2 * H_C,), F32)]).reshape(1, SMALL_W)
    rep = lambda a: jnp.repeat(a, P_B).reshape(1, D_B)
    expand = (jnp.arange(SMALL_W)[:, None] == (jnp.arange(D_B)[None, :] // P_B)).astype(BF16)
    return {
        "ssm_conv_w": w["ssm_conv_w"][l], "ssm_conv_b": w["ssm_conv_b"][l].reshape(1, CONV_B),
        "gdn_conv_w": w["gdn_conv_w"][l],
        "dtb_small": pad_small(w["ssm_dt_bias"][l], w["gdn_dt_bias"][l]),
        "alog_small": pad_small(w["ssm_A_log"][l], w["gdn_A_log"][l]),
        "ssm_d_e": rep(w["ssm_D"][l]),
        "ssm_norm_g": w["ssm_norm_g"][l].reshape(1, D_B), "gdn_norm_g": w["gdn_norm_g"][l].reshape(1, DV_C),
        "expand": expand,
    }


def _pick(n, prefs):
    for t in prefs:
        if n % t == 0:
            return t
    return n


def kernel(x_prompt, x_sample, c_prompt, c_sample, cache_k, cache_v, state_ssm, state_ssm_conv, state_gdn, state_gdn_conv, page_table, w_ada, b_ada, w_in, w_out, rel_bias, attn_lambda, attn_subln_g, ssm_conv_w, ssm_conv_b, ssm_dt_bias, ssm_A_log, ssm_D, ssm_norm_g, gdn_conv_w, gdn_dt_bias, gdn_A_log, gdn_norm_g, final_norm_g):
    depth = w_in.shape[0]
    bp, L, _ = x_prompt.shape
    bs = x_sample.shape[0]
    n_pages = page_table.shape[1]
    past = n_pages * PAGE_SIZE
    n_pool = cache_k.shape[1]
    wts = dict(ssm_conv_w=ssm_conv_w, ssm_conv_b=ssm_conv_b, ssm_dt_bias=ssm_dt_bias, ssm_A_log=ssm_A_log,
               ssm_D=ssm_D, ssm_norm_g=ssm_norm_g, gdn_conv_w=gdn_conv_w, gdn_dt_bias=gdn_dt_bias,
               gdn_A_log=gdn_A_log, gdn_norm_g=gdn_norm_g)

    tm = _pick(L, (512, 256, 128, 64))
    tm_out = _pick(L, (512, 256, 128, 64))
    T = _pick(L, (256, 128))
    Q = _pick(L, (256, 128, 64))
    R = _pick(L, (256, 128, 64))
    G = _pick(n_pages, (32, 16, 8, 4, 2, 1))
    gdn_steps = bp * (L // R)
    G_ride = (bs * n_pages) // gdn_steps if (bs * n_pages) % gdn_steps == 0 else 0
    if not (1 <= G_ride <= MAX_PAGES_PER_STEP and n_pages % G_ride == 0):
        G_ride = 0

    mod = _modulation(jnp.concatenate([c_prompt, c_sample], axis=0), w_ada, b_ada)
    bias_near = _near_bias_tiles(rel_bias, T)
    bias_pages = _decode_bias(rel_bias, past)
    bias_new = jnp.broadcast_to(jnp.tile(rel_bias[0].astype(F32), 2)[:, None], (2 * H_A, DV_A))
    ck = cache_k.reshape(depth, n_pool, PAGE_SIZE * H_A, 2 * DH_A)
    cv = cache_v.reshape(depth, n_pool, PAGE_SIZE * H_A, DV_A)
    lane_half = jnp.arange(DV_A) < DH_A

    xp = x_prompt
    xs = x_sample.reshape(1, bs, D_MODEL)
    new_p, new_s = [], []
    kv_rows = None
    for l in range(depth):
        lam_init = 0.8 - 0.6 * math.exp(-0.3 * l)
        p = _layer_params(l, wts)
        w_perm = _split_w_in(w_in[l])
        w_o = w_out[l].astype(BF16)
        shift, scale, gate = jnp.split(mod[l], 3, axis=-1)
        mp = lambda a: a[:bp].reshape(bp, 1, D_MODEL)
        ms = lambda a: a[bp:].reshape(1, bs, D_MODEL)

        final = final_norm_g if l == depth - 1 else None
        q, k, v, xbc, qkv, z, sm, *kv_rows = _in_projection(xp, mp(scale), mp(shift), w_perm, tm,
                                                            (l, depth, kv_rows))
        q_s, k_s, v_s, xbc_s, qkv_s, z_s, sm_s = _in_projection(xs, ms(scale), ms(shift), w_perm, bs)
        q4 = q_s.reshape(bs, H_A, DV_A) * (DH_A ** -0.5)
        qpat = jnp.concatenate([jnp.where(lane_half, q4, 0.0), jnp.where(lane_half, 0.0, q4)], axis=1)
        k4 = k_s.reshape(bs, H_A, DV_A)
        v4 = v_s.reshape(bs, H_A, DV_A)
        decode_args = (ck, cv, qpat, jnp.concatenate([k4, k4], axis=1), jnp.concatenate([v4, v4], axis=1),
                       bias_pages, bias_new, z_s[..., :D_A].reshape(bs, H_A, DV_A), attn_lambda[l],
                       attn_subln_g[l])

        o_a = _prompt_attention(q, k, v, z, bias_near, attn_lambda[l], attn_subln_g[l], lam_init, T)
        o_b, ssm_h = _prompt_ssd(xbc, sm, z, p, Q)
        if G_ride:
            o_c, gdn_s, o_a_s = _prompt_gdn(qkv, sm, z, p, R, (l, page_table, lam_init, G_ride, *decode_args))
        else:
            o_c, gdn_s = _prompt_gdn(qkv, sm, z, p, R)
            o_a_s = _decode_attention(l, page_table, lam_init, G, *decode_args)
        xp = _out_projection(xp, mp(gate), o_a, o_b, o_c, w_o, final, tm_out)
        new_p.append((ssm_h.reshape(bp, H_B, P_B, N_B), xbc[:, L - (CONV_W - 1):, :],
                      gdn_s.reshape(bp, H_C, DK_C, DV_C), qkv[:, L - (CONV_W - 1):, :]))

        row = lambda a: a.reshape(bs, 1, a.shape[-1])
        o_b_s, o_c_s, conv_b1, conv_c1, ssm_h_s, gdn_s_s = _sample_step(
            row(xbc_s), row(qkv_s), row(sm_s), row(z_s), state_ssm_conv[l], state_gdn_conv[l],
            state_ssm[l].reshape(bs, D_B, N_B), state_gdn[l].reshape(bs, H_C * DK_C, DV_C), p)
        xs = _out_projection(xs, ms(gate), o_a_s.reshape(1, bs, D_A).astype(BF16),
                             o_b_s.reshape(1, bs, D_B).astype(BF16), o_c_s.reshape(1, bs, D_C).astype(BF16),
                             w_o, final, bs)
        new_s.append((k_s.reshape(bs, 1, H_A, 2 * DH_A), v_s.reshape(bs, 1, H_A, DV_A),
                      ssm_h_s.reshape(bs, H_B, P_B, N_B), conv_b1,
                      gdn_s_s.reshape(bs, H_C, DK_C, DV_C), conv_c1))

    k_p, v_p = kv_rows
    ssm_p, ssm_conv_p, gdn_p, gdn_conv_p = [jnp.stack(a) for a in zip(*new_p)]
    k_s, v_s, ssm_s, ssm_conv_s, gdn_s, gdn_conv_s = [jnp.stack(a) for a in zip(*new_s)]
    return (xp, xs.reshape(bs, 1, D_MODEL), k_p, v_p, ssm_p, ssm_conv_p, gdn_p, gdn_conv_p,
            k_s, v_s, ssm_s, ssm_conv_s, gdn_s, gdn_conv_s)
```

```python
import functools
import math

import jax
import jax.numpy as jnp
from jax import lax
from jax.experimental import pallas as pl
from jax.experimental.pallas import tpu as pltpu

F32 = jnp.float32
BF16 = jnp.bfloat16

D_MODEL = 1024
H_A, DH_A, DV_A = 4, 64, 128
D_A = H_A * DV_A
H_B, P_B, G_B, N_B = 8, 64, 2, 128
D_B = H_B * P_B
CONV_B = D_B + 2 * G_B * N_B
H_C, DK_C, DV_C = 4, 128, 128
D_C = H_C * DV_C
CONV_C = 2 * H_C * DK_C + D_C
CONV_W = 4
D_MIX = D_A + D_B + D_C
REL_BUCKETS, REL_MAX_DIST = 32, 128
PAGE_SIZE = 128
EPS = 1e-6
GDN_CHUNK = 64

LANES = 128
SUBLANES = 8
SMALL_W = LANES
DT_LANE, BETA_LANE, AG_LANE = 0, H_B, H_B + H_C
NEG_BIG = -1e30
LOG2E = math.log2(math.e)
ATTN_Q_SCALE = DH_A ** -0.5 * LOG2E
ATTN_STRIP = 64
MAX_PAGES_PER_STEP = 32
VMEM_LIMIT = 56 * 1024 * 1024

SEGMENTS = (("q", 2 * H_A * DH_A), ("k", 2 * H_A * DH_A), ("v", D_A), ("xbc", CONV_B),
            ("qkv", CONV_C), ("z", D_MIX), ("small", SMALL_W))
SEGMENT_GROUPS = (("q", "k", "v", "xbc"), ("qkv",), ("z",), ("small",))


def _cparams(*sem):
    return pltpu.CompilerParams(dimension_semantics=sem, vmem_limit_bytes=VMEM_LIMIT)


def _silu(x):
    return x * jax.nn.sigmoid(x)


def _softplus(x):
    return jnp.maximum(x, 0.0) + jnp.log1p(jnp.exp(-jnp.abs(x)))


def _rms(x):
    return x * lax.rsqrt(jnp.mean(x * x, axis=-1, keepdims=True) + EPS)


def _dot_nt(a, b):
    return lax.dot_general(a, b, (((1,), (1,)), ((), ())), preferred_element_type=F32)


def _dot_tn(a, b):
    return lax.dot_general(a, b, (((0,), (0,)), ((), ())), preferred_element_type=F32)


def _dot(a, b):
    return jnp.dot(a, b, preferred_element_type=F32)


def _iota(shape, dim):
    return lax.broadcasted_iota(jnp.int32, shape, dim)


def _mod_kernel(c_ref, w_ref, b_ref, o_ref):
    sc = _silu(c_ref[...]).astype(BF16)
    o_ref[...] = _dot(sc, w_ref[...].astype(BF16)) + b_ref[...]


def _modulation(c_all, w_ada, b_ada):
    depth = w_ada.shape[0]
    n = c_all.shape[0]
    return pl.pallas_call(
        _mod_kernel,
        grid=(depth, 3),
        in_specs=[pl.BlockSpec((n, D_MODEL), lambda l, j: (0, 0)),
                  pl.BlockSpec((None, D_MODEL, D_MODEL), lambda l, j: (l, 0, j)),
                  pl.BlockSpec((None, 1, D_MODEL), lambda l, j: (l, 0, j))],
        out_specs=pl.BlockSpec((None, n, D_MODEL), lambda l, j: (l, 0, j)),
        out_shape=jax.ShapeDtypeStruct((depth, n, 3 * D_MODEL), F32),
        compiler_params=_cparams("arbitrary", "arbitrary"),
    )(c_all, w_ada, b_ada.reshape(depth, 1, 3 * D_MODEL))


def _inproj_kernel(x_ref, scale_ref, shift_ref, *rest, heads_out, n_carried):
    n_groups = len(SEGMENT_GROUPS)
    w_refs = rest[:n_groups]
    out_refs = rest[n_groups + n_carried:]
    h = _rms(x_ref[...]) * (1.0 + scale_ref[...]) + shift_ref[...]
    hb = h.astype(BF16)
    seg_refs = dict(zip((name for name, _ in SEGMENTS), out_refs))
    head_refs = dict(zip(("k", "v"), out_refs[len(SEGMENTS):])) if heads_out else {}
    widths = dict(SEGMENTS)
    for w_ref, names in zip(w_refs, SEGMENT_GROUPS):
        off = 0
        for name in names:
            n = widths[name]
            res = _dot(hb, w_ref[:, off:off + n])
            scaled = res * ATTN_Q_SCALE if heads_out and name == "q" else res
            seg_refs[name][...] = scaled.astype(seg_refs[name].dtype)
            if name in head_refs:
                ref = head_refs[name]
                for hd in range(H_A):
                    if len(ref.shape) == 3:
                        ref[:, hd, :] = res[:, hd * DV_A:(hd + 1) * DV_A]
                    else:
                        for d in range(ref.shape[0]):
                            ref[d, :, hd, :] = res[:, hd * DV_A:(hd + 1) * DV_A]
            off += n


def _in_projection(x, scale, shift, w_groups, tm, cache_rows=None):
    b, L, _ = x.shape
    ms = scale.shape[1]
    mod_rows = tm if ms == L else 1
    mod_map = (lambda i, j: (i, j, 0)) if ms == L else (lambda i, j: (i, 0, 0))
    heads_out = cache_rows is not None
    dt = lambda name: BF16 if heads_out and name in ("q", "k", "v") else F32
    out_specs = [pl.BlockSpec((None, tm, w), lambda i, j: (i, j, 0)) for _, w in SEGMENTS]
    out_shape = [jax.ShapeDtypeStruct((b, L, w), dt(name)) for name, w in SEGMENTS]
    carried, aliases = (), {}
    if heads_out:
        layer, depth, prev = cache_rows
        if prev is None:
            out_specs += [pl.BlockSpec((depth, None, tm, H_A, DV_A), lambda i, j: (0, i, j, 0, 0))] * 2
        else:
            out_specs += [pl.BlockSpec((None, None, tm, H_A, DV_A), lambda i, j: (layer, i, j, 0, 0))] * 2
        out_shape += [jax.ShapeDtypeStruct((depth, b, L, H_A, DV_A), F32)] * 2
        if prev is not None:
            carried = tuple(prev)
            aliases = {3 + len(w_groups) + n: len(SEGMENTS) + n for n in range(len(carried))}
    return pl.pallas_call(
        functools.partial(_inproj_kernel, heads_out=heads_out, n_carried=len(carried)),
        grid=(b, L // tm),
        in_specs=[pl.BlockSpec((None, tm, D_MODEL), lambda i, j: (i, j, 0)),
                  pl.BlockSpec((None, mod_rows, D_MODEL), mod_map),
                  pl.BlockSpec((None, mod_rows, D_MODEL), mod_map)]
                 + [pl.BlockSpec(w.shape, lambda i, j: (0, 0), pipeline_mode=pl.Buffered(1)) for w in w_groups]
                 + [pl.BlockSpec(memory_space=pl.ANY)] * len(carried),
        out_specs=out_specs,
        out_shape=out_shape,
        input_output_aliases=aliases,
        compiler_params=_cparams("arbitrary", "arbitrary"),
    )(x, scale, shift, *w_groups, *carried)


def _outproj_kernel(x_ref, gate_ref, oa_ref, ob_ref, oc_ref, w_ref, *rest, final):
    acc = _dot(oa_ref[...], w_ref[0:D_A, :])
    acc += _dot(ob_ref[...], w_ref[D_A:D_A + D_B, :])
    acc += _dot(oc_ref[...], w_ref[D_A + D_B:D_MIX, :])
    y = x_ref[...] + gate_ref[...] * acc
    if final:
        g_ref, o_ref = rest
        o_ref[...] = _rms(y) * g_ref[...]
    else:
        (o_ref,) = rest
        o_ref[...] = y


def _out_projection(x, gate, oa, ob, oc, w_out, final_g, tm):
    b, L, _ = x.shape
    ms = gate.shape[1]
    mod_rows = tm if ms == L else 1
    mod_map = (lambda i, j: (i, j, 0)) if ms == L else (lambda i, j: (i, 0, 0))
    row = lambda w: pl.BlockSpec((None, tm, w), lambda i, j: (i, j, 0))
    in_specs = [row(D_MODEL), pl.BlockSpec((None, mod_rows, D_MODEL), mod_map),
                row(D_A), row(D_B), row(D_C),
                pl.BlockSpec((D_MIX, D_MODEL), lambda i, j: (0, 0))]
    args = [x, gate, oa, ob, oc, w_out]
    final = final_g is not None
    if final:
        in_specs.append(pl.BlockSpec((1, D_MODEL), lambda i, j: (0, 0)))
        args.append(final_g.reshape(1, D_MODEL))
    return pl.pallas_call(
        functools.partial(_outproj_kernel, final=final),
        grid=(b, L // tm),
        in_specs=in_specs,
        out_specs=row(D_MODEL),
        out_shape=jax.ShapeDtypeStruct((b, L, D_MODEL), F32),
        compiler_params=_cparams("arbitrary", "arbitrary"),
    )(*args)


def _lambda_value(lp, lam_init):
    s01 = jnp.sum(lp[0:1, :] * lp[1:2, :], axis=-1, keepdims=True)
    s23 = jnp.sum(lp[2:3, :] * lp[3:4, :], axis=-1, keepdims=True)
    return jnp.exp(s01) - jnp.exp(s23) + lam_init


def _rel_bias_values(table, n):
    exact = REL_BUCKETS // 2
    large = exact + (jnp.log(jnp.maximum(n, 1).astype(F32) / exact)
                     / math.log(REL_MAX_DIST / exact) * (REL_BUCKETS - exact)).astype(jnp.int32)
    bucket = jnp.where(n < exact, n, jnp.minimum(large, REL_BUCKETS - 1))
    onehot = bucket[..., None, None] == jnp.arange(REL_BUCKETS)[:, None]
    return jnp.sum(jnp.where(onehot, table.astype(F32), 0.0), axis=-2)


def _attn_kernel(q_ref, kb_scr, vb_scr, z_ref, bias_ref, lamp_ref, subln_ref, o_ref, s_bufs, w_bufs,
                 *, T, lam_init):
    L = q_ref.shape[0]
    nq = L // T
    lam = _lambda_value(lamp_ref[...], lam_init)
    lane = _iota((T, DV_A), 1)
    half = LANES

    def logit_tasks(i):
        q = q_ref[i * T:(i + 1) * T, :]
        s_scr = s_bufs.at[i % 2]
        tasks = []
        for sub in range(2):
            qm = jnp.where((lane < DH_A) if sub == 0 else (lane >= DH_A), q, jnp.zeros_like(q))
            for j in range(i + 1):
                def task(qm=qm, sub=sub, j=j):
                    s = _dot_nt(qm, kb_scr[j * T:(j + 1) * T, :])
                    if j == i:
                        s = s + bias_ref[:, T:2 * T]
                    elif j == i - 1:
                        s = s + bias_ref[:, 0:T]
                    s_scr[sub, j] = s
                tasks.append(task)
        return tasks

    def softmax_strip(i, r0):
        s_scr, w_scr = s_bufs.at[i % 2], w_bufs.at[i % 2]
        strip = slice(r0, r0 + ATTN_STRIP)
        scale = []
        for sub in range(2):
            mx = None
            for j in range(i + 1):
                s = s_scr[sub, j, strip, :]
                for c0 in range(0, T, half):
                    t = s[:, c0:c0 + half]
                    mx = t if mx is None else jnp.maximum(mx, t)
            m_row = jnp.max(mx, axis=-1, keepdims=True)
            ls = None
            for j in range(i + 1):
                p = jnp.exp2(s_scr[sub, j, strip, :] - m_row)
                s_scr[sub, j, strip, :] = p
                for c0 in range(0, T, half):
                    t = p[:, c0:c0 + half]
                    ls = t if ls is None else ls + t
            scale.append(1.0 / jnp.sum(ls, axis=-1, keepdims=True))
        w0, w1 = scale[0], lam * scale[1]
        for j in range(i + 1):
            w_scr[strip, j * T:(j + 1) * T] = (s_scr[0, j, strip, :] * w0
                                               - s_scr[1, j, strip, :] * w1).astype(BF16)

    for task in logit_tasks(0):
        task()
    for i in range(nq):
        nxt = logit_tasks(i + 1) if i + 1 < nq else []
        strips = list(range(0, T, ATTN_STRIP))
        per = -(-len(nxt) // len(strips))
        for n, r0 in enumerate(strips):
            softmax_strip(i, r0)
            for task in nxt[n * per:(n + 1) * per]:
                task()
        rows = slice(i * T, (i + 1) * T)
        o = _dot(w_bufs[i % 2, :, 0:(i + 1) * T], vb_scr[0:(i + 1) * T, :])
        o = _rms(o) * subln_ref[...] * (1.0 - lam_init) * _silu(z_ref[rows, :])
        o_ref[rows, :] = o.astype(o_ref.dtype)


def _prompt_attention(q, k, v, z, bias_near, lam_p, subln, lam_init, T):
    b, L, _ = q.shape
    seq = lambda: pl.BlockSpec((None, L, DV_A), lambda i, h: (i, 0, h))
    return pl.pallas_call(
        functools.partial(_attn_kernel, T=T, lam_init=lam_init),
        grid=(b, H_A),
        in_specs=[seq(), seq(), seq(), seq(),
                  pl.BlockSpec((None, T, 2 * T), lambda i, h: (h, 0, 0)),
                  pl.BlockSpec((4, DH_A), lambda i, h: (0, 0)),
                  pl.BlockSpec((1, DV_A), lambda i, h: (0, 0))],
        out_specs=seq(),
        out_shape=jax.ShapeDtypeStruct((b, L, D_A), BF16),
        scratch_shapes=[pltpu.VMEM((2, 2, L // T, T, T), F32), pltpu.VMEM((2, T, L), BF16)],
        compiler_params=_cparams("arbitrary", "arbitrary"),
    )(q, k, v, z, bias_near, lam_p, subln.reshape(1, DV_A))


def _near_bias_tiles(table, T):
    period = 3 * T
    k = jnp.arange(period)
    n = jnp.where(k < 2 * T, T - k, T + period - k)
    vals = _rel_bias_values(table, jnp.maximum(n, 0)) - table[REL_BUCKETS - 1].astype(F32)
    w = jnp.where((n >= 0)[:, None], vals * LOG2E, NEG_BIG).T
    skew = jnp.tile(w, (1, T + 1))[:, :T * (period - 1)].reshape(H_A, T, period - 1)
    return skew[:, :, :2 * T]


N_DECODE_IN = 10


def _decode_stages(pt_ref, in_refs, o_ref, scratch, step, n_steps, first, G, layer, lam_init):
    (qpat_ref, knew_ref, vnew_ref, bias_ref, biasnew_ref, z_ref, lamp_ref, subln_ref,
     cache_k_ref, cache_v_ref) = in_refs
    m_scr, l_scr, acc_scr, kbuf, vbuf, sem = scratch
    slot = lax.rem(step, 2)

    def page_copy(cache_ref, buf, which, page, to_slot, g):
        return pltpu.make_async_copy(cache_ref.at[layer, page], buf.at[to_slot, g], sem.at[to_slot, g, which])

    def start_step(s, to_slot):
        for g in range(G):
            page = pt_ref[s, g]
            page_copy(cache_k_ref, kbuf, 0, page, to_slot, g).start()
            page_copy(cache_v_ref, vbuf, 1, page, to_slot, g).start()

    @pl.when(step == 0)
    def _():
        start_step(0, 0)

    @pl.when(step + 1 < n_steps)
    def _():
        start_step(step + 1, 1 - slot)

    @pl.when(first)
    def _():
        m_scr[...] = jnp.full(m_scr.shape, NEG_BIG, F32)
        l_scr[...] = jnp.zeros(l_scr.shape, F32)
        acc_scr[...] = jnp.zeros(acc_scr.shape, F32)

    for g in range(G):
        page = pt_ref[step, g]
        page_copy(cache_k_ref, kbuf, 0, page, slot, g).wait()
        page_copy(cache_v_ref, vbuf, 1, page, slot, g).wait()

    qf = qpat_ref[...]
    qb = qf.astype(BF16)
    st = {"s": [None] * G, "pv": None}

    def logits(g):
        st["s"][g] = _dot_nt(qb, kbuf[slot, g].astype(BF16)) + bias_ref[g]

    def softmax_stats():
        s = st["s"]
        m_old, l_old = m_scr[...], l_scr[...]
        m_tile = s[0]
        for g in range(1, G):
            m_tile = jnp.maximum(m_tile, s[g])
        m = jnp.maximum(m_old, jnp.max(m_tile, axis=-1, keepdims=True))
        st["alpha"] = jnp.exp(m_old - m)
        p = [jnp.exp(s[g] - m) for g in range(G)]
        p_sum = p[0]
        for g in range(1, G):
            p_sum = p_sum + p[g]
        st["p"] = [x.astype(BF16) for x in p]
        st["m"] = m
        st["l"] = st["alpha"] * l_old + jnp.sum(p_sum, axis=-1, keepdims=True)

    def values(g):
        d =_dot(st["p"][g], vbuf[slot, g].astype(BF16))
        st["pv"] = d if st["pv"] is None else st["pv"] + d

    def finish():
        m, l = st["m"], st["l"]
        acc = st["alpha"] * acc_scr[...] + st["pv"]
        m_scr[...] = m
        l_scr[...] = l
        acc_scr[...] = acc
        s_new = jnp.sum(qf * knew_ref[...], axis=-1, keepdims=True) + biasnew_ref[:, 0:1]
        m_new = jnp.maximum(m, s_new)
        alpha = jnp.exp(m - m_new)
        p_new = jnp.exp(s_new - m_new)
        l_fin = alpha * l + p_new
        out = (alpha * acc + p_new * vnew_ref[...]) / l_fin
        lam = _lambda_value(lamp_ref[...], lam_init)
        o = out[0:H_A, :] - lam * out[H_A:2 * H_A, :]
        o_ref[...] = _rms(o) * subln_ref[...] * (1.0 - lam_init) * _silu(z_ref[...])

    return ([functools.partial(logits, g) for g in range(G)] + [softmax_stats]
            + [functools.partial(values, g) for g in range(G)] + [finish])


class _Interleaver:
    def __init__(self, thunks, n_ticks):
        self.thunks, self.n_ticks, self.ticks, self.done = thunks, n_ticks, 0, 0

    def tick(self):
        self.ticks += 1
        target = min(len(self.thunks), -(-len(self.thunks) * self.ticks // self.n_ticks))
        while self.done < target:
            self.thunks[self.done]()
            self.done += 1

    def finish(self):
        self.ticks = self.n_ticks - 1
        self.tick()


def _decode_operands(layer, G, flat_step, page_table, cache_k, cache_v, qpat, knew, vnew, bias_pages,
                     bias_new, z3, lam_p, subln):
    bs, n_pages = page_table.shape
    spb = n_pages // G
    rows = PAGE_SIZE * H_A
    if spb & (spb - 1) == 0:
        seq = lambda ids: lax.shift_right_logical(flat_step(*ids), spb.bit_length() - 1)
        grp = lambda ids: flat_step(*ids) & (spb - 1)
    else:
        seq = lambda ids: flat_step(*ids) // spb
        grp = lambda ids: flat_step(*ids) % spb

    per_b = lambda r: pl.BlockSpec((None, r, DV_A), lambda *a: (seq(a[:-1]), 0, 0))
    const = lambda shape: pl.BlockSpec(shape, lambda *a: (0,) * len(shape))
    in_specs = [per_b(2 * H_A), per_b(2 * H_A), per_b(2 * H_A),
                pl.BlockSpec((G, 2 * H_A, rows), lambda *a: (grp(a[:-1]), 0, 0)),
                const((2 * H_A, DV_A)), per_b(H_A), const((4, DH_A)), const((1, DV_A)),
                pl.BlockSpec(memory_space=pl.ANY), pl.BlockSpec(memory_space=pl.ANY)]
    args = [qpat, knew, vnew, bias_pages, bias_new, z3, lam_p, subln.reshape(1, DV_A), cache_k, cache_v]
    scratch = [pltpu.VMEM((2 * H_A, 1), F32), pltpu.VMEM((2 * H_A, 1), F32), pltpu.VMEM((2 * H_A, DV_A), F32),
               pltpu.VMEM((2, G, rows, DV_A), F32), pltpu.VMEM((2, G, rows, DV_A), F32),
               pltpu.SemaphoreType.DMA((2, G, 2))]
    steps_table = page_table.reshape(bs * spb, G)
    return steps_table, in_specs, args, per_b(H_A), jax.ShapeDtypeStruct((bs, H_A, DV_A), F32), scratch, spb


def _decode_kernel(pt_ref, *refs, G, layer, lam_init):
    step = pl.program_id(0) * pl.num_programs(1) + pl.program_id(1)
    n_steps = pl.num_programs(0) * pl.num_programs(1)
    for stage in _decode_stages(pt_ref, refs[:N_DECODE_IN], refs[N_DECODE_IN], refs[N_DECODE_IN + 1:], step,
                                n_steps, pl.program_id(1) == 0, G, layer, lam_init):
        stage()


def _decode_attention(layer, page_table, lam_init, G, *decode_args):
    bs, n_pages = page_table.shape
    steps_table, in_specs, args, out_spec, out_shape, scratch, spb = _decode_operands(
        layer, G, lambda i, j: i * (n_pages // G) + j, page_table, *decode_args)
    return pl.pallas_call(
        functools.partial(_decode_kernel, G=G, layer=layer, lam_init=lam_init),
        grid_spec=pltpu.PrefetchScalarGridSpec(num_scalar_prefetch=1, grid=(bs, spb), in_specs=in_specs,
                                               out_specs=out_spec, scratch_shapes=scratch),
        out_shape=out_shape,
        compiler_params=_cparams("arbitrary", "arbitrary"),
    )(steps_table, *args)


def _decode_bias(table, past):
    n = past - jnp.arange(past)
    vals = _rel_bias_values(table, n).reshape(past // PAGE_SIZE, PAGE_SIZE, H_A)
    vals = jnp.moveaxis(vals, -1, 1)[..., None]
    same = (jnp.arange(H_A)[:, None, None] == jnp.arange(H_A)[None, None, :])
    full = jnp.where(same[None], vals, NEG_BIG).reshape(past // PAGE_SIZE, H_A, PAGE_SIZE * H_A)
    return jnp.concatenate([full, full], axis=1)


def _conv_taps(xp_ref, x, w_ref, rows):
    xp_ref[SUBLANES:SUBLANES + rows, :] = x
    xe = xp_ref[...]
    t = w_ref[0:1, :] * xe
    for i in range(1, CONV_W):
        t = w_ref[i:i + 1, :] * xe + pltpu.roll(t, 1, axis=0)
    xp_ref[0:SUBLANES, :] = x[rows - SUBLANES:rows, :]
    return t[SUBLANES:SUBLANES + rows, :]


def _ssd_kernel(xbc_ref, sm_ref, z_ref, cw_ref, cb_ref, dtb_s_ref, alog_s_ref,
                dskip_ref, ng_ref, exp_ref, y_ref, hout_ref, xp_scr, h_scr, *, Q):
    c = pl.program_id(1)

    @pl.when(c == 0)
    def _():
        xp_scr[0:SUBLANES, :] = jnp.zeros((SUBLANES, CONV_B), F32)
        h_scr[...] = jnp.zeros(h_scr.shape, F32)

    xa = _silu(_conv_taps(xp_scr, xbc_ref[...], cw_ref, Q) + cb_ref[...])
    xs = xa[:, 0:D_B]
    sm = sm_ref[...]
    row = _iota((Q, Q), 0)
    col = _iota((Q, Q), 1)
    tril = row >= col
    expand = exp_ref[...]

    dt_s = _softplus(sm + dtb_s_ref[...])
    a_s = dt_s * (-jnp.exp(alog_s_ref[...]))
    cum_s = _dot_exact_rhs(tril.astype(BF16), a_s)
    cum_t = cum_s.T
    dt_e = _dot_exact_lhs(dt_s, expand)
    cum_e = _dot_exact_lhs(cum_s, expand)
    xc = xs * dt_e
    dec_end = jnp.exp(cum_e[Q - 1:Q, :] - cum_e)
    dec_in = jnp.exp(cum_e)
    xc_b = xc.astype(BF16)
    xe_b = (xc * dec_end).astype(BF16)
    lane = _iota((Q, LANES), 1)
    prow = _iota((LANES, LANES), 0)
    heads_per_group = H_B // G_B
    pairs = []
    for g in range(G_B):
        bm = xa[:, D_B + g * N_B:D_B + (g + 1) * N_B].astype(BF16)
        cm = xa[:, D_B + G_B * N_B + g * N_B:D_B + G_B * N_B + (g + 1) * N_B].astype(BF16)
        cb = _dot_nt(cm, bm)
        for pr in range(g * heads_per_group // 2, (g + 1) * heads_per_group // 2):
            lo, hi = pr * LANES, (pr + 1) * LANES
            yd = []
            for hh in (2 * pr, 2 * pr + 1):
                seg = cum_s[:, hh:hh + 1] - cum_t[hh:hh + 1, :]
                mat = cb * jnp.exp(jnp.where(tril, seg, -jnp.inf))
                yd.append(_dot(mat.astype(BF16), xc_b[:, lo:hi]))
            y_diag = jnp.where(lane < P_B, yd[0], yd[1])
            hp = h_scr[lo:hi, :]
            y_off = _dot_nt(cm, hp.astype(BF16)) * dec_in[:, lo:hi]
            st = _dot_tn(xe_b[:, lo:hi], bm)
            a_last = jnp.where(prow < P_B, cum_t[2 * pr:2 * pr + 1, Q - 1:Q],
                               cum_t[2 * pr + 1:2 * pr + 2, Q - 1:Q])
            h_scr[lo:hi, :] = hp * jnp.exp(a_last) + st
            pairs.append(y_diag + y_off + xs[:, lo:hi] * dskip_ref[:, lo:hi])
    y = jnp.concatenate(pairs, axis=-1) * _silu(z_ref[...])
    gw = D_B // G_B
    y = jnp.concatenate([_rms(y[:, g * gw:(g + 1) * gw]) for g in range(G_B)], axis=-1) * ng_ref[...]
    y_ref[...] = y.astype(y_ref.dtype)

    @pl.when(c == pl.num_programs(1) - 1)
    def _():
        hout_ref[...] = h_scr[...]


def _prompt_ssd(xbc, sm, z, p, Q):
    b, L, _ = xbc.shape
    full = lambda a: pl.BlockSpec(a.shape, lambda i, j: (0,) * a.ndim)
    params = [p["ssm_conv_w"], p["ssm_conv_b"], p["dtb_small"], p["alog_small"],
              p["ssm_d_e"], p["ssm_norm_g"], p["expand"]]
    return pl.pallas_call(
        functools.partial(_ssd_kernel, Q=Q),
        grid=(b, L // Q),
        in_specs=[pl.BlockSpec((None, Q, CONV_B), lambda i, j: (i, j, 0)),
                  pl.BlockSpec((None, Q, SMALL_W), lambda i, j: (i, j, 0)),
                  pl.BlockSpec((None, Q, D_B), lambda i, j: (i, j, D_A // D_B))]
                 + [full(a) for a in params],
        out_specs=[pl.BlockSpec((None, Q, D_B), lambda i, j: (i, j, 0)),
                   pl.BlockSpec((None, D_B, N_B), lambda i, j: (i, 0, 0))],
        out_shape=[jax.ShapeDtypeStruct((b, L, D_B), BF16),
                   jax.ShapeDtypeStruct((b, D_B, N_B), F32)],
        scratch_shapes=[pltpu.VMEM((Q + SUBLANES, CONV_B), F32), pltpu.VMEM((D_B, N_B), F32)],
        compiler_params=_cparams("arbitrary", "arbitrary"),
    )(xbc, sm, z, *params)


def _split2(a):
    hi = a.astype(BF16)
    return hi, (a - hi.astype(F32)).astype(BF16)


def _split3(x):
    x1 = x.astype(BF16)
    r1 = x - x1.astype(F32)
    x2 = r1.astype(BF16)
    return x1, x2, (r1 - x2.astype(F32)).astype(BF16)


def _dot_exact_rhs(a_bf16, x):
    x1, x2, x3 = _split3(x)
    return _dot(a_bf16, x1) + (_dot(a_bf16, x2) + _dot(a_bf16, x3))


def _dot_exact_lhs(x, a_bf16):
    x1, x2, x3 = _split3(x)
    return _dot(x1, a_bf16) + (_dot(x2, a_bf16) + _dot(x3, a_bf16))


def _gdn_head_prepare(act, gcum, gcum_t, beta_s, r0, h):
    cs = GDN_CHUNK
    rows = slice(r0, r0 + cs)
    q = act[rows, h * DK_C:(h + 1) * DK_C]
    k = act[rows, D_C + h * DK_C:D_C + (h + 1) * DK_C]
    v = act[rows, 2 * D_C + h * DV_C:2 * D_C + (h + 1) * DV_C]
    q = q * lax.rsqrt(jnp.sum(q * q, axis=-1, keepdims=True) + EPS) * (DK_C ** -0.5)
    k = k * lax.rsqrt(jnp.sum(k * k, axis=-1, keepdims=True) + EPS)
    beta = beta_s[rows, BETA_LANE + h:BETA_LANE + h + 1]
    gc = gcum[rows, AG_LANE + h:AG_LANE + h + 1]
    gr = gcum_t[AG_LANE + h:AG_LANE + h + 1, rows]
    g_last = gcum[r0 + cs - 1:r0 + cs, AG_LANE + h:AG_LANE + h + 1]
    ri = _iota((cs, 2 * cs), 0)
    ci = _iota((cs, 2 * cs), 1) & (cs - 1)
    decay = jnp.exp(jnp.where(ri >= ci, gc - jnp.concatenate([gr, gr], axis=1), -jnp.inf))
    kb = k.astype(BF16)
    kk = _dot_nt(kb, jnp.concatenate([kb, kb], axis=0))
    p = jnp.where(ri > ci, -(beta * kk * decay), 0.0)
    rhs = jnp.concatenate([v * beta, k * (beta * jnp.exp(gc))], axis=-1)
    attn = (_dot_nt(q.astype(BF16), kb) * decay[:, 0:cs]).astype(BF16)
    qg = q * jnp.exp(gc)
    kd = (k * jnp.exp(g_last - gc)).astype(BF16)
    return p, rhs, attn, qg, kd, jnp.exp(g_last)


def _gdn_head_finish(sol, attn, qg, kd, e_last):
    x_b = sol.astype(BF16)
    ax = _dot(attn, x_b)
    kx = _dot_tn(kd, x_b)
    lhs = jnp.concatenate([kx[:, DV_C:2 * DV_C], qg - ax[:, DV_C:2 * DV_C]], axis=0).astype(BF16)
    return lhs, kx[:, 0:DV_C], ax[:, 0:DV_C], e_last


def _split_lhs(p):
    hi, lo = _split2(p)
    half = jnp.where(_iota(p.shape, 1) < GDN_CHUNK, hi, lo)
    return jnp.concatenate([half, half], axis=1)


def _split_rhs(x):
    hi, lo = _split2(x)
    return jnp.concatenate([hi, hi, lo, lo], axis=0)


def _neumann_solve(ps, xs, tick):
    n_levels = GDN_CHUNK.bit_length() - 1
    for lvl in range(n_levels):
        lhs = [_split_lhs(p) for p in ps]
        xs = [x + _dot(a, _split_rhs(x)) for a, x in zip(lhs, xs)]
        if lvl + 1 < n_levels:
            ps = [_dot(a, _split_rhs(p)) for a, p in zip(lhs, ps)]
        tick()
    return xs


N_GDN_IN, N_GDN_OUT, N_GDN_SCRATCH = 7, 2, 3


def _gdn_kernel(*refs, R, rider=None):
    if rider is None:
        ins, refs = refs[:N_GDN_IN], refs[N_GDN_IN:]
        outs, scratch = refs[:N_GDN_OUT], refs[N_GDN_OUT:]
    else:
        G, spb, layer, lam_init = rider
        n_dec = N_DECODE_IN
        pt_ref, refs = refs[0], refs[1:]
        ins, dec_ins, refs = refs[:N_GDN_IN], refs[N_GDN_IN:N_GDN_IN + n_dec], refs[N_GDN_IN + n_dec:]
        outs, dec_out, refs = refs[:N_GDN_OUT], refs[N_GDN_OUT], refs[N_GDN_OUT + 1:]
        scratch, dec_scratch = refs[:N_GDN_SCRATCH], refs[N_GDN_SCRATCH:]
    qkv_ref, sm_ref, z_ref, cw_ref, dtb_s_ref, alog_s_ref, ng_ref = ins
    o_ref, sout_ref = outs
    xp_scr, s_scr, act_scr = scratch
    c = pl.program_id(1)

    @pl.when(c == 0)
    def _():
        xp_scr[0:SUBLANES, :] = jnp.zeros((SUBLANES, CONV_C), F32)
        s_scr[...] = jnp.zeros(s_scr.shape, F32)

    cs = GDN_CHUNK
    n_chunks = R // cs
    if rider is not None:
        step = pl.program_id(0) * pl.num_programs(1) + c
        n_steps = pl.num_programs(0) * pl.num_programs(1)
        stages = _decode_stages(pt_ref, dec_ins, dec_out, dec_scratch, step, n_steps,
                                lax.rem(step, spb) == 0, G, layer, lam_init)
        n_ticks = n_chunks * (H_C + 1) + 2 * (GDN_CHUNK.bit_length() - 1)
        rider_stages = _Interleaver(stages, n_ticks)
    else:
        rider_stages = _Interleaver([], 1)
    tick = rider_stages.tick

    act_scr[...] = _silu(_conv_taps(xp_scr, qkv_ref[...], cw_ref, R))
    sm = sm_ref[...]
    g_s = _softplus(sm + dtb_s_ref[...]) * (-jnp.exp(alog_s_ref[...]))
    beta_s = jax.nn.sigmoid(sm)
    ri = _iota((R, R), 0)
    ci = _iota((R, R), 1)
    chunk_tril = (lax.shift_right_logical(ri, 6) == lax.shift_right_logical(ci, 6)) & (ri >= ci)
    gcum = _dot_exact_rhs(chunk_tril.astype(BF16), g_s)
    gcum_t = gcum.T

    def state_free_part(chunks):
        prep = []
        for ch in chunks:
            for h in range(H_C):
                prep.append(_gdn_head_prepare(act_scr, gcum, gcum_t, beta_s, ch * cs, h))
                tick()
        sols = _neumann_solve([pr[0] for pr in prep], [pr[1] for pr in prep], tick)
        return [_gdn_head_finish(sol, *pr[2:]) for sol, pr in zip(sols, prep)]

    def state_step(ch, heads):
        tick()
        outs = []
        for h, (lhs, c_add, o_add, e_last) in enumerate(heads):
            s_old = s_scr[h * DK_C:(h + 1) * DK_C, :]
            r = _dot(lhs, s_old.astype(BF16))
            s_scr[h * DK_C:(h + 1) * DK_C, :] = s_old * e_last + (c_add - r[0:DK_C, :])
            outs.append(_rms(r[DK_C:DK_C + cs, :] + o_add) * ng_ref[...])
        rows = slice(ch * cs, (ch + 1) * cs)
        o_ref[rows, :] = (jnp.concatenate(outs, axis=-1) * _silu(z_ref[rows, :])).astype(o_ref.dtype)

    half = max(n_chunks // 2, 1)
    for chunks in (range(0, half), range(half, n_chunks)):
        parts = state_free_part(chunks)
        for n, ch in enumerate(chunks):
            state_step(ch, parts[n * H_C:(n + 1) * H_C])
    rider_stages.finish()

    @pl.when(c == pl.num_programs(1) - 1)
    def _():
        sout_ref[...] = s_scr[...]


def _prompt_gdn(qkv, sm, z, p, R, decode=None):
    b, L, _ = qkv.shape
    nj = L // R
    full = lambda a: pl.BlockSpec(a.shape, lambda *_: (0,) * a.ndim)
    params = [p["gdn_conv_w"], p["dtb_small"], p["alog_small"], p["gdn_norm_g"]]
    in_specs = [pl.BlockSpec((None, R, CONV_C), lambda i, j, *_: (i, j, 0)),
                pl.BlockSpec((None, R, SMALL_W), lambda i, j, *_: (i, j, 0)),
                pl.BlockSpec((None, R, D_C), lambda i, j, *_: (i, j, (D_A + D_B) // D_C))] + [full(a) for a in params]
    out_specs = [pl.BlockSpec((None, R, D_C), lambda i, j, *_: (i, j, 0)),
                 pl.BlockSpec((None, H_C * DK_C, DV_C), lambda i, j, *_: (i, 0, 0))]
    out_shape = [jax.ShapeDtypeStruct((b, L, D_C), BF16), jax.ShapeDtypeStruct((b, H_C * DK_C, DV_C), F32)]
    scratch = [pltpu.VMEM((R + SUBLANES, CONV_C), F32), pltpu.VMEM((H_C * DK_C, DV_C), F32),
               pltpu.VMEM((R, CONV_C), F32)]
    args = [qkv, sm, z, *params]
    if decode is None:
        return pl.pallas_call(
            functools.partial(_gdn_kernel, R=R), grid=(b, nj), in_specs=in_specs, out_specs=out_specs,
            out_shape=out_shape, scratch_shapes=scratch, compiler_params=_cparams("arbitrary", "arbitrary"),
        )(*args)
    layer, page_table, lam_init, G, *decode_args = decode
    steps_table, d_in, d_args, d_out, d_shape, d_scratch, spb = _decode_operands(
        layer, G, lambda i, j: i * nj + j, page_table, *decode_args)
    return pl.pallas_call(
        functools.partial(_gdn_kernel, R=R, rider=(G, spb, layer, lam_init)),
        grid_spec=pltpu.PrefetchScalarGridSpec(
            num_scalar_prefetch=1, grid=(b, nj), in_specs=in_specs + d_in, out_specs=out_specs + [d_out],
            scratch_shapes=scratch + d_scratch),
        out_shape=out_shape + [d_shape],
        compiler_params=_cparams("arbitrary", "arbitrary"),
    )(steps_table, *args, *d_args)


def _expand_heads(vals, n_heads, width):
    lane = _iota((1, n_heads * width), 1)
    out = jnp.zeros((1, n_heads * width), F32)
    for h in range(n_heads):
        out = jnp.where((lane >= h * width) & (lane < (h + 1) * width), vals[:, h:h + 1], out)
    return out


def _pad_rows(rows):
    n = rows[0].shape[-1]
    r = _iota((SUBLANES, n), 0)
    out = jnp.zeros((SUBLANES, n), F32)
    for i, v in enumerate(rows):
        out = jnp.where(r == i, v, out)
    return out


def _sample_step_kernel(*refs, n_data, n_params, block_b):
    data, params, outs = refs[:n_data], refs[n_data:n_data + n_params], refs[n_data + n_params:]
    for bi in range(block_b):
        _sample_step_one(*[r.at[bi] for r in data], *params, *[r.at[bi] for r in outs])


def _sample_step_one(xbc_ref, qkv_ref, sm_ref, z_ref, cst_b_ref, cst_c_ref, h0_ref, s0_ref,
                     cwb_ref, cbb_ref, cwc_ref, dtb_s_ref, alog_s_ref, dskip_ref, ngb_ref, ngc_ref,
                     ob_ref, oc_ref, cnew_b_ref, cnew_c_ref, h1_ref, s1_ref):
    def conv(x, st_ref, w_ref, new_ref):
        y = w_ref[CONV_W - 1:CONV_W, :] * x
        for i in range(CONV_W - 1):
            y = y + w_ref[i:i + 1, :] * st_ref[i:i + 1, :]
        for i in range(CONV_W - 2):
            new_ref[i:i + 1, :] = st_ref[i + 1:i + 2, :]
        new_ref[CONV_W - 2:CONV_W - 1, :] = x
        return y

    sm = sm_ref[...]
    dt_or_sp = _softplus(sm + dtb_s_ref[...])
    a_s = dt_or_sp * (-jnp.exp(alog_s_ref[...]))
    z = z_ref[...]

    xa = _silu(conv(xbc_ref[...], cst_b_ref, cwb_ref, cnew_b_ref) + cbb_ref[...])
    xs = xa[:, 0:D_B]
    dt_e = _expand_heads(dt_or_sp[:, DT_LANE:DT_LANE + H_B], H_B, P_B)
    a_e = _expand_heads(a_s[:, DT_LANE:DT_LANE + H_B], H_B, P_B)
    xc = xs * dt_e
    half = D_B // G_B
    lane_b = _iota((1, D_B), 1)
    bm = [xa[:, D_B + g * N_B:D_B + (g + 1) * N_B] for g in range(G_B)]
    cm = [xa[:, D_B + G_B * N_B + g * N_B:D_B + G_B * N_B + (g + 1) * N_B] for g in range(G_B)]
    h0 = h0_ref[...]
    coff = _dot_nt(_pad_rows(cm).astype(BF16), h0.astype(BF16))
    y_off = jnp.where(lane_b < half, coff[0:1, :], coff[1:2, :]) * jnp.exp(a_e)
    cb = [jnp.sum(cm[g] * bm[g], axis=-1, keepdims=True) for g in range(G_B)]
    y_diag = jnp.where(lane_b < half, cb[0], cb[1]) * xc
    y = (y_diag + y_off + xs * dskip_ref[...]) * _silu(z[:, D_A:D_A + D_B])
    y = jnp.concatenate([_rms(y[:, g * half:(g + 1) * half]) for g in range(G_B)], axis=-1)
    ob_ref[...] = y * ngb_ref[...]
    xc_rows = _pad_rows([jnp.where(lane_b < half, xc, 0.0), jnp.where(lane_b >= half, xc, 0.0)])
    outer = _dot_tn(xc_rows.astype(BF16), _pad_rows(bm).astype(BF16))
    state_row = _iota((D_B, N_B), 0)
    dec = jnp.exp(a_s)
    dec_col = jnp.zeros((D_B, N_B), F32)
    for h in range(H_B):
        in_head = (state_row >= h * P_B) & (state_row < (h + 1) * P_B)
        dec_col = jnp.where(in_head, dec[:, DT_LANE + h:DT_LANE + h + 1], dec_col)
    h1_ref[...] = h0 * dec_col + outer

    qkv = _silu(conv(qkv_ref[...], cst_c_ref, cwc_ref, cnew_c_ref))
    beta_s = jax.nn.sigmoid(sm)
    outs = []
    for h in range(H_C):
        q = qkv[:, h * DK_C:(h + 1) * DK_C]
        k = qkv[:, D_C + h * DK_C:D_C + (h + 1) * DK_C]
        v = qkv[:, 2 * D_C + h * DV_C:2 * D_C + (h + 1) * DV_C]
        q = q * lax.rsqrt(jnp.sum(q * q, axis=-1, keepdims=True) + EPS) * (DK_C ** -0.5)
        k = k * lax.rsqrt(jnp.sum(k * k, axis=-1, keepdims=True) + EPS)
        beta = beta_s[:, BETA_LANE + h:BETA_LANE + h + 1]
        eg = jnp.exp(a_s[:, AG_LANE + h:AG_LANE + h + 1])
        s_old = s0_ref[h * DK_C:(h + 1) * DK_C, :]
        ws = _dot(_pad_rows([k * (beta * eg), q * eg]).astype(BF16), s_old.astype(BF16))
        v_new = v * beta - ws[0:1, :]
        o = ws[1:2, :] + jnp.sum(q * k, axis=-1, keepdims=True) * v_new
        upd = _dot_tn(_pad_rows([k]).astype(BF16), _pad_rows([v_new]).astype(BF16))
        s1_ref[h * DK_C:(h + 1) * DK_C, :] = s_old * eg + upd
        outs.append(_rms(o) * ngc_ref[...])
    oc_ref[...] = jnp.concatenate(outs, axis=-1) * _silu(z[:, D_A + D_B:D_MIX])


def _sample_step(xbc, qkv, sm, z, cst_b, cst_c, h0, s0, p):
    bs = xbc.shape[0]
    block_b = _pick(bs, (4, 2, 1))
    per_b = lambda a: pl.BlockSpec((block_b,) + tuple(a.shape[1:]), lambda i: (i,) + (0,) * (len(a.shape) - 1))
    full = lambda a: pl.BlockSpec(a.shape, lambda i: (0,) * a.ndim)
    data = [xbc, qkv, sm, z, cst_b, cst_c, h0, s0]
    params = [p["ssm_conv_w"], p["ssm_conv_b"], p["gdn_conv_w"], p["dtb_small"], p["alog_small"],
              p["ssm_d_e"], p["ssm_norm_g"], p["gdn_norm_g"]]
    outs = [jax.ShapeDtypeStruct((bs, 1, D_B), F32), jax.ShapeDtypeStruct((bs, 1, D_C), F32),
            jax.ShapeDtypeStruct(cst_b.shape, F32), jax.ShapeDtypeStruct(cst_c.shape, F32),
            jax.ShapeDtypeStruct(h0.shape, F32), jax.ShapeDtypeStruct(s0.shape, F32)]
    return pl.pallas_call(
        functools.partial(_sample_step_kernel, n_data=len(data), n_params=len(params), block_b=block_b),
        grid=(bs // block_b,),
        in_specs=[per_b(a) for a in data] + [full(a) for a in params],
        out_specs=[per_b(a) for a in outs],
        out_shape=outs,
        compiler_params=_cparams("arbitrary"),
    )(*data, *params)


def _split_w_in(w):
    sizes = (2 * H_A * DH_A, 2 * H_A * DH_A, D_A, CONV_B, H_B, CONV_C, H_C, H_C, D_MIX)
    offs = [0]
    for s in sizes:
        offs.append(offs[-1] + s)
    col = lambda i, j=None: w[:, offs[i]:offs[(i if j is None else j) + 1]].astype(BF16)
    small = jnp.concatenate([col(4), col(6), col(7),
                             jnp.zeros((w.shape[0], SMALL_W - H_B - 2 * H_C), BF16)], axis=1)
    return [col(0, 3), col(5), col(8), small]


def _layer_params(l, w):
    pad_small = lambda a_ssm, a_gdn: jnp.concatenate(
        [a_ssm, jnp.zeros((H_C,), F32), a_gdn, jnp.zeros((SMALL_W - H_B - 2 * H_C,), F32)]).reshape(1, SMALL_W)
    rep = lambda a: jnp.repeat(a, P_B).reshape(1, D_B)
    expand = (jnp.arange(SMALL_W)[:, None] == (jnp.arange(D_B)[None, :] // P_B)).astype(BF16)
    return {
        "ssm_conv_w": w["ssm_conv_w"][l], "ssm_conv_b": w["ssm_conv_b"][l].reshape(1, CONV_B),
        "gdn_conv_w": w["gdn_conv_w"][l],
        "dtb_small": pad_small(w["ssm_dt_bias"][l], w["gdn_dt_bias"][l]),
        "alog_small": pad_small(w["ssm_A_log"][l], w["gdn_A_log"][l]),
        "ssm_d_e": rep(w["ssm_D"][l]),
        "ssm_norm_g": w["ssm_norm_g"][l].reshape(1, D_B), "gdn_norm_g": w["gdn_norm_g"][l].reshape(1, DV_C),
        "expand": expand,
    }


def _pick(n, prefs):
    for t in prefs:
        if n % t == 0:
            return t
    return n


def kernel(x_prompt, x_sample, c_prompt, c_sample, cache_k, cache_v, state_ssm, state_ssm_conv, state_gdn, state_gdn_conv, page_table, w_ada, b_ada, w_in, w_out, rel_bias, attn_lambda, attn_subln_g, ssm_conv_w, ssm_conv_b, ssm_dt_bias, ssm_A_log, ssm_D, ssm_norm_g, gdn_conv_w, gdn_dt_bias, gdn_A_log, gdn_norm_g, final_norm_g):
    depth = w_in.shape[0]
    bp, L, _ = x_prompt.shape
    bs = x_sample.shape[0]
    n_pages = page_table.shape[1]
    past = n_pages * PAGE_SIZE
    n_pool = cache_k.shape[1]
    wts = dict(ssm_conv_w=ssm_conv_w, ssm_conv_b=ssm_conv_b, ssm_dt_bias=ssm_dt_bias, ssm_A_log=ssm_A_log,
               ssm_D=ssm_D, ssm_norm_g=ssm_norm_g, gdn_conv_w=gdn_conv_w, gdn_dt_bias=gdn_dt_bias,
               gdn_A_log=gdn_A_log, gdn_norm_g=gdn_norm_g)

    tm = _pick(L, (512, 256, 128, 64))
    tm_out = _pick(L, (512, 256, 128, 64))
    T = _pick(L, (256, 128))
    Q = _pick(L, (256, 128, 64))
    R = _pick(L, (256, 128, 64))
    G = _pick(n_pages, (32, 16, 8, 4, 2, 1))
    gdn_steps = bp * (L // R)
    G_ride = (bs * n_pages) // gdn_steps if (bs * n_pages) % gdn_steps == 0 else 0
    if not (1 <= G_ride <= MAX_PAGES_PER_STEP and n_pages % G_ride == 0):
        G_ride = 0

    mod = _modulation(jnp.concatenate([c_prompt, c_sample], axis=0), w_ada, b_ada)
    bias_near = _near_bias_tiles(rel_bias, T)
    bias_pages = _decode_bias(rel_bias, past)
    bias_new = jnp.broadcast_to(jnp.tile(rel_bias[0].astype(F32), 2)[:, None], (2 * H_A, DV_A))
    ck = cache_k.reshape(depth, n_pool, PAGE_SIZE * H_A, 2 * DH_A)
    cv = cache_v.reshape(depth, n_pool, PAGE_SIZE * H_A, DV_A)
    lane_half = jnp.arange(DV_A) < DH_A

    xp = x_prompt
    xs = x_sample.reshape(1, bs, D_MODEL)
    new_p, new_s = [], []
    kv_rows = None
    for l in range(depth):
        lam_init = 0.8 - 0.6 * math.exp(-0.3 * l)
        p = _layer_params(l, wts)
        w_perm = _split_w_in(w_in[l])
        w_o = w_out[l].astype(BF16)
        shift, scale, gate = jnp.split(mod[l], 3, axis=-1)
        mp = lambda a: a[:bp].reshape(bp, 1, D_MODEL)
        ms = lambda a: a[bp:].reshape(1, bs, D_MODEL)

        final = final_norm_g if l == depth - 1 else None
        q, k, v, xbc, qkv, z, sm, *kv_rows = _in_projection(xp, mp(scale), mp(shift), w_perm, tm,
                                                            (l, depth, kv_rows))
        q_s, k_s, v_s, xbc_s, qkv_s, z_s, sm_s = _in_projection(xs, ms(scale), ms(shift), w_perm, bs)
        q4 = q_s.reshape(bs, H_A, DV_A) * (DH_A ** -0.5)
        qpat = jnp.concatenate([jnp.where(lane_half, q4, 0.0), jnp.where(lane_half, 0.0, q4)], axis=1)
        k4 = k_s.reshape(bs, H_A, DV_A)
        v4 = v_s.reshape(bs, H_A, DV_A)
        decode_args = (ck, cv, qpat, jnp.concatenate([k4, k4], axis=1), jnp.concatenate([v4, v4], axis=1),
                       bias_pages, bias_new, z_s[..., :D_A].reshape(bs, H_A, DV_A), attn_lambda[l],
                       attn_subln_g[l])

        o_a = _prompt_attention(q, k, v, z, bias_near, attn_lambda[l], attn_subln_g[l], lam_init, T)
        o_b, ssm_h = _prompt_ssd(xbc, sm, z, p, Q)
        if G_ride:
            o_c, gdn_s, o_a_s = _prompt_gdn(qkv, sm, z, p, R, (l, page_table, lam_init, G_ride, *decode_args))
        else:
            o_c, gdn_s = _prompt_gdn(qkv, sm, z, p, R)
            o_a_s = _decode_attention(l, page_table, lam_init, G, *decode_args)
        xp = _out_projection(xp, mp(gate), o_a, o_b, o_c, w_o, final, tm_out)
        new_p.append((ssm_h.reshape(bp, H_B, P_B, N_B), xbc[:, L - (CONV_W - 1):, :],
                      gdn_s.reshape(bp, H_C, DK_C, DV_C), qkv[:, L - (CONV_W - 1):, :]))

        row = lambda a: a.reshape(bs, 1, a.shape[-1])
        o_b_s, o_c_s, conv_b1, conv_c1, ssm_h_s, gdn_s_s = _sample_step(
            row(xbc_s), row(qkv_s), row(sm_s), row(z_s), state_ssm_conv[l], state_gdn_conv[l],
            state_ssm[l].reshape(bs, D_B, N_B), state_gdn[l].reshape(bs, H_C * DK_C, DV_C), p)
        xs = _out_projection(xs, ms(gate), o_a_s.reshape(1, bs, D_A).astype(BF16),
                             o_b_s.reshape(1, bs, D_B).astype(BF16), o_c_s.reshape(1, bs, D_C).astype(BF16),
                             w_o, final, bs)
        new_s.append((k_s.reshape(bs, 1, H_A, 2 * DH_A), v_s.reshape(bs, 1, H_A, DV_A),
                      ssm_h_s.reshape(bs, H_B, P_B, N_B), conv_b1,
                      gdn_s_s.reshape(bs, H_C, DK_C, DV_C), conv_c1))

    k_p, v_p = kv_rows
    ssm_p, ssm_conv_p, gdn_p, gdn_conv_p = [jnp.stack(a) for a in zip(*new_p)]
    k_s, v_s, ssm_s, ssm_conv_s, gdn_s, gdn_conv_s = [jnp.stack(a) for a in zip(*new_s)]
    return (xp, xs.reshape(bs, 1, D_MODEL), k_p, v_p, ssm_p, ssm_conv_p, gdn_p, gdn_conv_p,
            k_s, v_s, ssm_s, ssm_conv_s, gdn_s, gdn_conv_s)
```

```python
import functools
import math

import jax
import jax.numpy as jnp
from jax import lax
from jax.experimental import pallas as pl
from jax.experimental.pallas import tpu as pltpu

F32 = jnp.float32
BF16 = jnp.bfloat16

D_MODEL = 1024
H_A, DH_A, DV_A = 4, 64, 128
D_A = H_A * DV_A
H_B, P_B, G_B, N_B = 8, 64, 2, 128
D_B = H_B * P_B
CONV_B = D_B + 2 * G_B * N_B
H_C, DK_C, DV_C = 4, 128, 128
D_C = H_C * DV_C
CONV_C = 2 * H_C * DK_C + D_C
CONV_W = 4
D_MIX = D_A + D_B + D_C
REL_BUCKETS, REL_MAX_DIST = 32, 128
PAGE_SIZE = 128
EPS = 1e-6
GDN_CHUNK = 64

LANES = 128
SUBLANES = 8
SMALL_W = LANES
DT_LANE, BETA_LANE, AG_LANE = 0, H_B, H_B + H_C
NEG_BIG = -1e30
LOG2E = math.log2(math.e)
ATTN_Q_SCALE = DH_A ** -0.5 * LOG2E
ATTN_STRIP = 64
MAX_PAGES_PER_STEP = 32
VMEM_LIMIT = 56 * 1024 * 1024

SEGMENTS = (("q", 2 * H_A * DH_A), ("k", 2 * H_A * DH_A), ("v", D_A), ("xbc", CONV_B),
            ("qkv", CONV_C), ("z", D_MIX), ("small", SMALL_W))
SEGMENT_GROUPS = (("q", "k", "v", "xbc"), ("qkv",), ("z",), ("small",))


def _cparams(*sem):
    return pltpu.CompilerParams(dimension_semantics=sem, vmem_limit_bytes=VMEM_LIMIT)


def _silu(x):
    return x * jax.nn.sigmoid(x)


def _softplus(x):
    return jnp.maximum(x, 0.0) + jnp.log1p(jnp.exp(-jnp.abs(x)))


def _rms(x):
    return x * lax.rsqrt(jnp.mean(x * x, axis=-1, keepdims=True) + EPS)


def _dot_nt(a, b):
    return lax.dot_general(a, b, (((1,), (1,)), ((), ())), preferred_element_type=F32)


def _dot_tn(a, b):
    return lax.dot_general(a, b, (((0,), (0,)), ((), ())), preferred_element_type=F32)


def _dot(a, b):
    return jnp.dot(a, b, preferred_element_type=F32)


def _iota(shape, dim):
    return lax.broadcasted_iota(jnp.int32, shape, dim)


def _mod_kernel(c_ref, w_ref, b_ref, o_ref):
    sc = _silu(c_ref[...]).astype(BF16)
    o_ref[...] = _dot(sc, w_ref[...].astype(BF16)) + b_ref[...]


def _modulation(c_all, w_ada, b_ada):
    depth = w_ada.shape[0]
    n = c_all.shape[0]
    return pl.pallas_call(
        _mod_kernel,
        grid=(depth, 3),
        in_specs=[pl.BlockSpec((n, D_MODEL), lambda l, j: (0, 0)),
                  pl.BlockSpec((None, D_MODEL, D_MODEL), lambda l, j: (l, 0, j)),
                  pl.BlockSpec((None, 1, D_MODEL), lambda l, j: (l, 0, j))],
        out_specs=pl.BlockSpec((None, n, D_MODEL), lambda l, j: (l, 0, j)),
        out_shape=jax.ShapeDtypeStruct((depth, n, 3 * D_MODEL), F32),
        compiler_params=_cparams("arbitrary", "arbitrary"),
    )(c_all, w_ada, b_ada.reshape(depth, 1, 3 * D_MODEL))


def _inproj_kernel(x_ref, scale_ref, shift_ref, *rest, heads_out, n_carried):
    n_groups = len(SEGMENT_GROUPS)
    w_refs = rest[:n_groups]
    out_refs = rest[n_groups + n_carried:]
    h = _rms(x_ref[...]) * (1.0 + scale_ref[...]) + shift_ref[...]
    hb = h.astype(BF16)
    seg_refs = dict(zip((name for name, _ in SEGMENTS), out_refs))
    head_refs = dict(zip(("k", "v"), out_refs[len(SEGMENTS):])) if heads_out else {}
    widths = dict(SEGMENTS)
    for w_ref, names in zip(w_refs, SEGMENT_GROUPS):
        off = 0
        for name in names:
            n = widths[name]
            res = _dot(hb, w_ref[:, off:off + n])
            scaled = res * ATTN_Q_SCALE if heads_out and name == "q" else res
            seg_refs[name][...] = scaled.astype(seg_refs[name].dtype)
            if name in head_refs:
                ref = head_refs[name]
                for hd in range(H_A):
                    if len(ref.shape) == 3:
                        ref[:, hd, :] = res[:, hd * DV_A:(hd + 1) * DV_A]
                    else:
                        for d in range(ref.shape[0]):
                            ref[d, :, hd, :] = res[:, hd * DV_A:(hd + 1) * DV_A]
            off += n


def _in_projection(x, scale, shift, w_groups, tm, cache_rows=None):
    b, L, _ = x.shape
    ms = scale.shape[1]
    mod_rows = tm if ms == L else 1
    mod_map = (lambda i, j: (i, j, 0)) if ms == L else (lambda i, j: (i, 0, 0))
    heads_out = cache_rows is not None
    dt = lambda name: BF16 if heads_out and name in ("q", "k", "v") else F32
    out_specs = [pl.BlockSpec((None, tm, w), lambda i, j: (i, j, 0)) for _, w in SEGMENTS]
    out_shape = [jax.ShapeDtypeStruct((b, L, w), dt(name)) for name, w in SEGMENTS]
    carried, aliases = (), {}
    if heads_out:
        layer, depth, prev = cache_rows
        if prev is None:
            out_specs += [pl.BlockSpec((depth, None, tm, H_A, DV_A), lambda i, j: (0, i, j, 0, 0))] * 2
        else:
            out_specs += [pl.BlockSpec((None, None, tm, H_A, DV_A), lambda i, j: (layer, i, j, 0, 0))] * 2
        out_shape += [jax.ShapeDtypeStruct((depth, b, L, H_A, DV_A), F32)] * 2
        if prev is not None:
            carried = tuple(prev)
            aliases = {3 + len(w_groups) + n: len(SEGMENTS) + n for n in range(len(carried))}
    return pl.pallas_call(
        functools.partial(_inproj_kernel, heads_out=heads_out, n_carried=len(carried)),
        grid=(b, L // tm),
        in_specs=[pl.BlockSpec((None, tm, D_MODEL), lambda i, j: (i, j, 0)),
                  pl.BlockSpec((None, mod_rows, D_MODEL), mod_map),
                  pl.BlockSpec((None, mod_rows, D_MODEL), mod_map)]
                 + [pl.BlockSpec(w.shape, lambda i, j: (0, 0), pipeline_mode=pl.Buffered(1)) for w in w_groups]
                 + [pl.BlockSpec(memory_space=pl.ANY)] * len(carried),
        out_specs=out_specs,
        out_shape=out_shape,
        input_output_aliases=aliases,
        compiler_params=_cparams("arbitrary", "arbitrary"),
    )(x, scale, shift, *w_groups, *carried)


def _outproj_kernel(x_ref, gate_ref, oa_ref, ob_ref, oc_ref, w_ref, *rest, final):
    acc = _dot(oa_ref[...], w_ref[0:D_A, :])
    acc += _dot(ob_ref[...], w_ref[D_A:D_A + D_B, :])
    acc += _dot(oc_ref[...], w_ref[D_A + D_B:D_MIX, :])
    y = x_ref[...] + gate_ref[...] * acc
    if final:
        g_ref, o_ref = rest
        o_ref[...] = _rms(y) * g_ref[...]
    else:
        (o_ref,) = rest
        o_ref[...] = y


def _out_projection(x, gate, oa, ob, oc, w_out, final_g, tm):
    b, L, _ = x.shape
    ms = gate.shape[1]
    mod_rows = tm if ms == L else 1
    mod_map = (lambda i, j: (i, j, 0)) if ms == L else (lambda i, j: (i, 0, 0))
    row = lambda w: pl.BlockSpec((None, tm, w), lambda i, j: (i, j, 0))
    in_specs = [row(D_MODEL), pl.BlockSpec((None, mod_rows, D_MODEL), mod_map),
                row(D_A), row(D_B), row(D_C),
                pl.BlockSpec((D_MIX, D_MODEL), lambda i, j: (0, 0))]
    args = [x, gate, oa, ob, oc, w_out]
    final = final_g is not None
    if final:
        in_specs.append(pl.BlockSpec((1, D_MODEL), lambda i, j: (0, 0)))
        args.append(final_g.reshape(1, D_MODEL))
    return pl.pallas_call(
        functools.partial(_outproj_kernel, final=final),
        grid=(b, L // tm),
        in_specs=in_specs,
        out_specs=row(D_MODEL),
        out_shape=jax.ShapeDtypeStruct((b, L, D_MODEL), F32),
        compiler_params=_cparams("arbitrary", "arbitrary"),
    )(*args)


def _lambda_value(lp, lam_init):
    s01 = jnp.sum(lp[0:1, :] * lp[1:2, :], axis=-1, keepdims=True)
    s23 = jnp.sum(lp[2:3, :] * lp[3:4, :], axis=-1, keepdims=True)
    return jnp.exp(s01) - jnp.exp(s23) + lam_init


def _rel_bias_values(table, n):
    exact = REL_BUCKETS // 2
    large = exact + (jnp.log(jnp.maximum(n, 1).astype(F32) / exact)
                     / math.log(REL_MAX_DIST / exact) * (REL_BUCKETS - exact)).astype(jnp.int32)
    bucket = jnp.where(n < exact, n, jnp.minimum(large, REL_BUCKETS - 1))
    onehot = bucket[..., None, None] == jnp.arange(REL_BUCKETS)[:, None]
    return jnp.sum(jnp.where(onehot, table.astype(F32), 0.0), axis=-2)


def _attn_kernel(q_ref, kb_scr, vb_scr, z_ref, bias_ref, lamp_ref, subln_ref, o_ref, s_bufs, w_bufs,
                 vaug_scr, *, T, lam_init):
    L = q_ref.shape[0]
    nq = L // T
    lam = _lambda_value(lamp_ref[...], lam_init)
    lane = _iota((T, DV_A), 1)
    half = LANES

    def logit_tasks(i):
        q = q_ref[i * T:(i + 1) * T, :]
        s_scr = s_bufs.at[i % 2]
        tasks = []
        for sub in range(2):
            qm = jnp.where((lane < DH_A) if sub == 0 else (lane >= DH_A), q, jnp.zeros_like(q))
            for j in range(i + 1):
                def task(qm=qm, sub=sub, j=j):
                    s = _dot_nt(qm, kb_scr[j * T:(j + 1) * T, :])
                    if j == i:
                        s = s + bias_ref[:, T:2 * T]
                    elif j == i - 1:
                        s = s + bias_ref[:, 0:T]
                    s_scr[sub, j] = s
                tasks.append(task)
        return tasks

    def softmax_strip(i, r0):
        s_scr, w_scr = s_bufs.at[i % 2], w_bufs.at[i % 2]
        strip = slice(r0, r0 + ATTN_STRIP)
        for sub in range(2):
            mx = None
            for j in range(i + 1):
                s = s_scr[sub, j, strip, :]
                for c0 in range(0, T, half):
                    t = s[:, c0:c0 + half]
                    mx = t if mx is None else jnp.maximum(mx, t)
            m_row = jnp.max(mx, axis=-1, keepdims=True)
            for j in range(i + 1):
                w_scr[sub, strip, j * T:(j + 1) * T] = jnp.exp2(s_scr[sub, j, strip, :] - m_row).astype(BF16)

    vaug_scr[:, 0:DV_A] = vb_scr[...]
    vaug_scr[:, DV_A:2 * DV_A] = (_iota((L, DV_A), 1) == 0).astype(BF16)

    for task in logit_tasks(0):
        task()
    for i in range(nq):
        nxt = logit_tasks(i + 1) if i + 1 < nq else []
        strips = list(range(0, T, ATTN_STRIP))
        per = -(-len(nxt) // len(strips))
        for n, r0 in enumerate(strips):
            softmax_strip(i, r0)
            for task in nxt[n * per:(n + 1) * per]:
                task()
        rows = slice(i * T, (i + 1) * T)
        nk = (i + 1) * T
        outs = []
        for sub in range(2):
            acc = _dot(w_bufs[i % 2, sub, :, 0:nk], vaug_scr[0:nk, :])
            outs.append(acc[:, 0:DV_A] / acc[:, DV_A:DV_A + 1])
        o = outs[0] - lam * outs[1]
        o = _rms(o) * subln_ref[...] * (1.0 - lam_init) * _silu(z_ref[rows, :])
        o_ref[rows, :] = o.astype(o_ref.dtype)


def _prompt_attention(q, k, v, z, bias_near, lam_p, subln, lam_init, T):
    b, L, _ = q.shape
    seq = lambda: pl.BlockSpec((None, L, DV_A), lambda i, h: (i, 0, h))
    return pl.pallas_call(
        functools.partial(_attn_kernel, T=T, lam_init=lam_init),
        grid=(b, H_A),
        in_specs=[seq(), seq(), seq(), seq(),
                  pl.BlockSpec((None, T, 2 * T), lambda i, h: (h, 0, 0)),
                  pl.BlockSpec((4, DH_A), lambda i, h: (0, 0)),
                  pl.BlockSpec((1, DV_A), lambda i, h: (0, 0))],
        out_specs=seq(),
        out_shape=jax.ShapeDtypeStruct((b, L, D_A), BF16),
        scratch_shapes=[pltpu.VMEM((2, 2, L // T, T, T), F32), pltpu.VMEM((2, 2, T, L), BF16),
                        pltpu.VMEM((L, 2 * DV_A), BF16)],
        compiler_params=_cparams("arbitrary", "arbitrary"),
    )(q, k, v, z, bias_near, lam_p, subln.reshape(1, DV_A))


def _near_bias_tiles(table, T):
    period = 3 * T
    k = jnp.arange(period)
    n = jnp.where(k < 2 * T, T - k, T + period - k)
    vals = _rel_bias_values(table, jnp.maximum(n, 0)) - table[REL_BUCKETS - 1].astype(F32)
    w = jnp.where((n >= 0)[:, None], vals * LOG2E, NEG_BIG).T
    skew = jnp.tile(w, (1, T + 1))[:, :T * (period - 1)].reshape(H_A, T, period - 1)
    return skew[:, :, :2 * T]


N_DECODE_IN = 10


def _decode_stages(pt_ref, in_refs, o_ref, scratch, step, n_steps, first, G, layer, lam_init):
    (qpat_ref, knew_ref, vnew_ref, bias_ref, biasnew_ref, z_ref, lamp_ref, subln_ref,
     cache_k_ref, cache_v_ref) = in_refs
    m_scr, l_scr, acc_scr, kbuf, vbuf, sem = scratch
    slot = lax.rem(step, 2)

    def page_copy(cache_ref, buf, which, page, to_slot, g):
        return pltpu.make_async_copy(cache_ref.at[layer, page], buf.at[to_slot, g], sem.at[to_slot, g, which])

    def start_step(s, to_slot):
        for g in range(G):
            page = pt_ref[s, g]
            page_copy(cache_k_ref, kbuf, 0, page, to_slot, g).start()
            page_copy(cache_v_ref, vbuf, 1, page, to_slot, g).start()

    @pl.when(step == 0)
    def _():
        start_step(0, 0)

    @pl.when(step + 1 < n_steps)
    def _():
        start_step(step + 1, 1 - slot)

    @pl.when(first)
    def _():
        m_scr[...] = jnp.full(m_scr.shape, NEG_BIG, F32)
        l_scr[...] = jnp.zeros(l_scr.shape, F32)
        acc_scr[...] = jnp.zeros(acc_scr.shape, F32)

    for g in range(G):
        page = pt_ref[step, g]
        page_copy(cache_k_ref, kbuf, 0, page, slot, g).wait()
        page_copy(cache_v_ref, vbuf, 1, page, slot, g).wait()

    qf = qpat_ref[...]
    qb = qf.astype(BF16)
    st = {"s": [None] * G, "pv": None}

    def logits(g):
        st["s"][g] = _dot_nt(qb, kbuf[slot, g].astype(BF16)) + bias_ref[g]

    def softmax_stats():
        s = st["s"]
        m_old, l_old = m_scr[...], l_scr[...]
        m_tile = s[0]
        for g in range(1, G):
            m_tile = jnp.maximum(m_tile, s[g])
        m = jnp.maximum(m_old, jnp.max(m_tile, axis=-1, keepdims=True))
        st["alpha"] = jnp.exp(m_old - m)
        p = [jnp.exp(s[g] - m) for g in range(G)]
        p_sum = p[0]
        for g in range(1, G):
            p_sum = p_sum + p[g]
        st["p"] = [x.astype(BF16) for x in p]
        st["m"] = m
        st["l"] = st["alpha"] * l_old + jnp.sum(p_sum, axis=-1, keepdims=True)

    def values(g):
        d =_dot(st["p"][g], vbuf[slot, g].astype(BF16))
        st["pv"] = d if st["pv"] is None else st["pv"] + d

    def finish():
        m, l = st["m"], st["l"]
        acc = st["alpha"] * acc_scr[...] + st["pv"]
        m_scr[...] = m
        l_scr[...] = l
        acc_scr[...] = acc
        s_new = jnp.sum(qf * knew_ref[...], axis=-1, keepdims=True) + biasnew_ref[:, 0:1]
        m_new = jnp.maximum(m, s_new)
        alpha = jnp.exp(m - m_new)
        p_new = jnp.exp(s_new - m_new)
        l_fin = alpha * l + p_new
        out = (alpha * acc + p_new * vnew_ref[...]) / l_fin
        lam = _lambda_value(lamp_ref[...], lam_init)
        o = out[0:H_A, :] - lam * out[H_A:2 * H_A, :]
        o_ref[...] = _rms(o) * subln_ref[...] * (1.0 - lam_init) * _silu(z_ref[...])

    return ([functools.partial(logits, g) for g in range(G)] + [softmax_stats]
            + [functools.partial(values, g) for g in range(G)] + [finish])


class _Interleaver:
    def __init__(self, thunks, n_ticks):
        self.thunks, self.n_ticks, self.ticks, self.done = thunks, n_ticks, 0, 0

    def tick(self):
        self.ticks += 1
        target = min(len(self.thunks), -(-len(self.thunks) * self.ticks // self.n_ticks))
        while self.done < target:
            self.thunks[self.done]()
            self.done += 1

    def finish(self):
        self.ticks = self.n_ticks - 1
        self.tick()


def _decode_operands(layer, G, flat_step, page_table, cache_k, cache_v, qpat, knew, vnew, bias_pages,
                     bias_new, z3, lam_p, subln):
    bs, n_pages = page_table.shape
    spb = n_pages // G
    rows = PAGE_SIZE * H_A
    if spb & (spb - 1) == 0:
        seq = lambda ids: lax.shift_right_logical(flat_step(*ids), spb.bit_length() - 1)
        grp = lambda ids: flat_step(*ids) & (spb - 1)
    else:
        seq = lambda ids: flat_step(*ids) // spb
        grp = lambda ids: flat_step(*ids) % spb

    per_b = lambda r: pl.BlockSpec((None, r, DV_A), lambda *a: (seq(a[:-1]), 0, 0))
    const = lambda shape: pl.BlockSpec(shape, lambda *a: (0,) * len(shape))
    in_specs = [per_b(2 * H_A), per_b(2 * H_A), per_b(2 * H_A),
                pl.BlockSpec((G, 2 * H_A, rows), lambda *a: (grp(a[:-1]), 0, 0)),
                const((2 * H_A, DV_A)), per_b(H_A), const((4, DH_A)), const((1, DV_A)),
                pl.BlockSpec(memory_space=pl.ANY), pl.BlockSpec(memory_space=pl.ANY)]
    args = [qpat, knew, vnew, bias_pages, bias_new, z3, lam_p, subln.reshape(1, DV_A), cache_k, cache_v]
    scratch = [pltpu.VMEM((2 * H_A, 1), F32), pltpu.VMEM((2 * H_A, 1), F32), pltpu.VMEM((2 * H_A, DV_A), F32),
               pltpu.VMEM((2, G, rows, DV_A), F32), pltpu.VMEM((2, G, rows, DV_A), F32),
               pltpu.SemaphoreType.DMA((2, G, 2))]
    steps_table = page_table.reshape(bs * spb, G)
    return steps_table, in_specs, args, per_b(H_A), jax.ShapeDtypeStruct((bs, H_A, DV_A), F32), scratch, spb


def _decode_kernel(pt_ref, *refs, G, layer, lam_init):
    step = pl.program_id(0) * pl.num_programs(1) + pl.program_id(1)
    n_steps = pl.num_programs(0) * pl.num_programs(1)
    for stage in _decode_stages(pt_ref, refs[:N_DECODE_IN], refs[N_DECODE_IN], refs[N_DECODE_IN + 1:], step,
                                n_steps, pl.program_id(1) == 0, G, layer, lam_init):
        stage()


def _decode_attention(layer, page_table, lam_init, G, *decode_args):
    bs, n_pages = page_table.shape
    steps_table, in_specs, args, out_spec, out_shape, scratch, spb = _decode_operands(
        layer, G, lambda i, j: i * (n_pages // G) + j, page_table, *decode_args)
    return pl.pallas_call(
        functools.partial(_decode_kernel, G=G, layer=layer, lam_init=lam_init),
        grid_spec=pltpu.PrefetchScalarGridSpec(num_scalar_prefetch=1, grid=(bs, spb), in_specs=in_specs,
                                               out_specs=out_spec, scratch_shapes=scratch),
        out_shape=out_shape,
        compiler_params=_cparams("arbitrary", "arbitrary"),
    )(steps_table, *args)


def _decode_bias(table, past):
    n = past - jnp.arange(past)
    vals = _rel_bias_values(table, n).reshape(past // PAGE_SIZE, PAGE_SIZE, H_A)
    vals = jnp.moveaxis(vals, -1, 1)[..., None]
    same = (jnp.arange(H_A)[:, None, None] == jnp.arange(H_A)[None, None, :])
    full = jnp.where(same[None], vals, NEG_BIG).reshape(past // PAGE_SIZE, H_A, PAGE_SIZE * H_A)
    return jnp.concatenate([full, full], axis=1)


def _conv_taps(xp_ref, x, w_ref, rows):
    xp_ref[SUBLANES:SUBLANES + rows, :] = x
    xe = xp_ref[...]
    t = w_ref[0:1, :] * xe
    for i in range(1, CONV_W):
        t = w_ref[i:i + 1, :] * xe + pltpu.roll(t, 1, axis=0)
    xp_ref[0:SUBLANES, :] = x[rows - SUBLANES:rows, :]
    return t[SUBLANES:SUBLANES + rows, :]


def _ssd_kernel(xbc_ref, sm_ref, z_ref, cw_ref, cb_ref, dtb_s_ref, alog_s_ref,
                dskip_ref, ng_ref, exp_ref, y_ref, hout_ref, xp_scr, h_scr, *, Q):
    c = pl.program_id(1)

    @pl.when(c == 0)
    def _():
        xp_scr[0:SUBLANES, :] = jnp.zeros((SUBLANES, CONV_B), F32)
        h_scr[...] = jnp.zeros(h_scr.shape, F32)

    xa = _silu(_conv_taps(xp_scr, xbc_ref[...], cw_ref, Q) + cb_ref[...])
    xs = xa[:, 0:D_B]
    sm = sm_ref[...]
    row = _iota((Q, Q), 0)
    col = _iota((Q, Q), 1)
    tril = row >= col
    expand = exp_ref[...]

    dt_s = _softplus(sm + dtb_s_ref[...])
    a_s = dt_s * (-jnp.exp(alog_s_ref[...]))
    cum_s = _dot_exact_rhs(tril.astype(BF16), a_s)
    cum_t = cum_s.T
    dt_e = _dot_exact_lhs(dt_s, expand)
    cum_e = _dot_exact_lhs(cum_s, expand)
    xc = xs * dt_e
    dec_end = jnp.exp(cum_e[Q - 1:Q, :] - cum_e)
    dec_in = jnp.exp(cum_e)
    xc_b = xc.astype(BF16)
    xe_b = (xc * dec_end).astype(BF16)
    lane = _iota((Q, LANES), 1)
    prow = _iota((LANES, LANES), 0)
    heads_per_group = H_B // G_B
    pairs = []
    for g in range(G_B):
        bm = xa[:, D_B + g * N_B:D_B + (g + 1) * N_B].astype(BF16)
        cm = xa[:, D_B + G_B * N_B + g * N_B:D_B + G_B * N_B + (g + 1) * N_B].astype(BF16)
        cb = _dot_nt(cm, bm)
        for pr in range(g * heads_per_group // 2, (g + 1) * heads_per_group // 2):
            lo, hi = pr * LANES, (pr + 1) * LANES
            yd = []
            for hh in (2 * pr, 2 * pr + 1):
                seg = cum_s[:, hh:hh + 1] - cum_t[hh:hh + 1, :]
                mat = cb * jnp.exp(jnp.where(tril, seg, -jnp.inf))
                yd.append(_dot(mat.astype(BF16), xc_b[:, lo:hi]))
            y_diag = jnp.where(lane < P_B, yd[0], yd[1])
            hp = h_scr[lo:hi, :]
            y_off = _dot_nt(cm, hp.astype(BF16)) * dec_in[:, lo:hi]
            st = _dot_tn(xe_b[:, lo:hi], bm)
            a_last = jnp.where(prow < P_B, cum_t[2 * pr:2 * pr + 1, Q - 1:Q],
                               cum_t[2 * pr + 1:2 * pr + 2, Q - 1:Q])
            h_scr[lo:hi, :] = hp * jnp.exp(a_last) + st
            pairs.append(y_diag + y_off + xs[:, lo:hi] * dskip_ref[:, lo:hi])
    y = jnp.concatenate(pairs, axis=-1) * _silu(z_ref[...])
    gw = D_B // G_B
    y = jnp.concatenate([_rms(y[:, g * gw:(g + 1) * gw]) for g in range(G_B)], axis=-1) * ng_ref[...]
    y_ref[...] = y.astype(y_ref.dtype)

    @pl.when(c == pl.num_programs(1) - 1)
    def _():
        hout_ref[...] = h_scr[...]


def _prompt_ssd(xbc, sm, z, p, Q):
    b, L, _ = xbc.shape
    full = lambda a: pl.BlockSpec(a.shape, lambda i, j: (0,) * a.ndim)
    params = [p["ssm_conv_w"], p["ssm_conv_b"], p["dtb_small"], p["alog_small"],
              p["ssm_d_e"], p["ssm_norm_g"], p["expand"]]
    return pl.pallas_call(
        functools.partial(_ssd_kernel, Q=Q),
        grid=(b, L // Q),
        in_specs=[pl.BlockSpec((None, Q, CONV_B), lambda i, j: (i, j, 0)),
                  pl.BlockSpec((None, Q, SMALL_W), lambda i, j: (i, j, 0)),
                  pl.BlockSpec((None, Q, D_B), lambda i, j: (i, j, D_A // D_B))]
                 + [full(a) for a in params],
        out_specs=[pl.BlockSpec((None, Q, D_B), lambda i, j: (i, j, 0)),
                   pl.BlockSpec((None, D_B, N_B), lambda i, j: (i, 0, 0))],
        out_shape=[jax.ShapeDtypeStruct((b, L, D_B), BF16),
                   jax.ShapeDtypeStruct((b, D_B, N_B), F32)],
        scratch_shapes=[pltpu.VMEM((Q + SUBLANES, CONV_B), F32), pltpu.VMEM((D_B, N_B), F32)],
        compiler_params=_cparams("arbitrary", "arbitrary"),
    )(xbc, sm, z, *params)


def _split2(a):
    hi = a.astype(BF16)
    return hi, (a - hi.astype(F32)).astype(BF16)


def _split3(x):
    x1 = x.astype(BF16)
    r1 = x - x1.astype(F32)
    x2 = r1.astype(BF16)
    return x1, x2, (r1 - x2.astype(F32)).astype(BF16)


def _dot_exact_rhs(a_bf16, x):
    x1, x2, x3 = _split3(x)
    return _dot(a_bf16, x1) + (_dot(a_bf16, x2) + _dot(a_bf16, x3))


def _dot_exact_lhs(x, a_bf16):
    x1, x2, x3 = _split3(x)
    return _dot(x1, a_bf16) + (_dot(x2, a_bf16) + _dot(x3, a_bf16))


def _gdn_head_prepare(act, gcum, gcum_t, beta_s, r0, h):
    cs = GDN_CHUNK
    rows = slice(r0, r0 + cs)
    q = act[rows, h * DK_C:(h + 1) * DK_C]
    k = act[rows, D_C + h * DK_C:D_C + (h + 1) * DK_C]
    v = act[rows, 2 * D_C + h * DV_C:2 * D_C + (h + 1) * DV_C]
    q = q * lax.rsqrt(jnp.sum(q * q, axis=-1, keepdims=True) + EPS) * (DK_C ** -0.5)
    k = k * lax.rsqrt(jnp.sum(k * k, axis=-1, keepdims=True) + EPS)
    beta = beta_s[rows, BETA_LANE + h:BETA_LANE + h + 1]
    gc = gcum[rows, AG_LANE + h:AG_LANE + h + 1]
    gr = gcum_t[AG_LANE + h:AG_LANE + h + 1, rows]
    g_last = gcum[r0 + cs - 1:r0 + cs, AG_LANE + h:AG_LANE + h + 1]
    ri = _iota((cs, 2 * cs), 0)
    ci = _iota((cs, 2 * cs), 1) & (cs - 1)
    decay = jnp.exp(jnp.where(ri >= ci, gc - jnp.concatenate([gr, gr], axis=1), -jnp.inf))
    kb = k.astype(BF16)
    kk = _dot_nt(kb, jnp.concatenate([kb, kb], axis=0))
    p = jnp.where(ri > ci, -(beta * kk * decay), 0.0)
    rhs = jnp.concatenate([v * beta, k * (beta * jnp.exp(gc))], axis=-1)
    attn = (_dot_nt(q.astype(BF16), kb) * decay[:, 0:cs]).astype(BF16)
    qg = q * jnp.exp(gc)
    kd = (k * jnp.exp(g_last - gc)).astype(BF16)
    return p, rhs, attn, qg, kd, jnp.exp(g_last)


def _gdn_head_finish(sol, attn, qg, kd, e_last):
    x_b = sol.astype(BF16)
    ax = _dot(attn, x_b)
    kx = _dot_tn(kd, x_b)
    lhs = jnp.concatenate([kx[:, DV_C:2 * DV_C], qg - ax[:, DV_C:2 * DV_C]], axis=0).astype(BF16)
    return lhs, kx[:, 0:DV_C], ax[:, 0:DV_C], e_last


def _split_lhs(p):
    hi, lo = _split2(p)
    half = jnp.where(_iota(p.shape, 1) < GDN_CHUNK, hi, lo)
    return jnp.concatenate([half, half], axis=1)


def _split_rhs(x):
    hi, lo = _split2(x)
    return jnp.concatenate([hi, hi, lo, lo], axis=0)


def _neumann_solve(ps, xs, tick):
    n_levels = GDN_CHUNK.bit_length() - 1
    for lvl in range(n_levels):
        lhs = [_split_lhs(p) for p in ps]
        xs = [x + _dot(a, _split_rhs(x)) for a, x in zip(lhs, xs)]
        if lvl + 1 < n_levels:
            ps = [_dot(a, _split_rhs(p)) for a, p in zip(lhs, ps)]
        tick()
    return xs


N_GDN_IN, N_GDN_OUT, N_GDN_SCRATCH = 7, 2, 3


def _gdn_kernel(*refs, R, rider=None):
    if rider is None:
        ins, refs = refs[:N_GDN_IN], refs[N_GDN_IN:]
        outs, scratch = refs[:N_GDN_OUT], refs[N_GDN_OUT:]
    else:
        G, spb, layer, lam_init = rider
        n_dec = N_DECODE_IN
        pt_ref, refs = refs[0], refs[1:]
        ins, dec_ins, refs = refs[:N_GDN_IN], refs[N_GDN_IN:N_GDN_IN + n_dec], refs[N_GDN_IN + n_dec:]
        outs, dec_out, refs = refs[:N_GDN_OUT], refs[N_GDN_OUT], refs[N_GDN_OUT + 1:]
        scratch, dec_scratch = refs[:N_GDN_SCRATCH], refs[N_GDN_SCRATCH:]
    qkv_ref, sm_ref, z_ref, cw_ref, dtb_s_ref, alog_s_ref, ng_ref = ins
    o_ref, sout_ref = outs
    xp_scr, s_scr, act_scr = scratch
    c = pl.program_id(1)

    @pl.when(c == 0)
    def _():
        xp_scr[0:SUBLANES, :] = jnp.zeros((SUBLANES, CONV_C), F32)
        s_scr[...] = jnp.zeros(s_scr.shape, F32)

    cs = GDN_CHUNK
    n_chunks = R // cs
    if rider is not None:
        step = pl.program_id(0) * pl.num_programs(1) + c
        n_steps = pl.num_programs(0) * pl.num_programs(1)
        stages = _decode_stages(pt_ref, dec_ins, dec_out, dec_scratch, step, n_steps,
                                lax.rem(step, spb) == 0, G, layer, lam_init)
        n_ticks = n_chunks * (H_C + 1) + 2 * (GDN_CHUNK.bit_length() - 1)
        rider_stages = _Interleaver(stages, n_ticks)
    else:
        rider_stages = _Interleaver([], 1)
    tick = rider_stages.tick

    act_scr[...] = _silu(_conv_taps(xp_scr, qkv_ref[...], cw_ref, R))
    sm = sm_ref[...]
    g_s = _softplus(sm + dtb_s_ref[...]) * (-jnp.exp(alog_s_ref[...]))
    beta_s = jax.nn.sigmoid(sm)
    ri = _iota((R, R), 0)
    ci = _iota((R, R), 1)
    chunk_tril = (lax.shift_right_logical(ri, 6) == lax.shift_right_logical(ci, 6)) & (ri >= ci)
    gcum = _dot_exact_rhs(chunk_tril.astype(BF16), g_s)
    gcum_t = gcum.T

    def state_free_part(chunks):
        prep = []
        for ch in chunks:
            for h in range(H_C):
                prep.append(_gdn_head_prepare(act_scr, gcum, gcum_t, beta_s, ch * cs, h))
                tick()
        sols = _neumann_solve([pr[0] for pr in prep], [pr[1] for pr in prep], tick)
        return [_gdn_head_finish(sol, *pr[2:]) for sol, pr in zip(sols, prep)]

    def state_step(ch, heads):
        tick()
        outs = []
        for h, (lhs, c_add, o_add, e_last) in enumerate(heads):
            s_old = s_scr[h * DK_C:(h + 1) * DK_C, :]
            r = _dot(lhs, s_old.astype(BF16))
            s_scr[h * DK_C:(h + 1) * DK_C, :] = s_old * e_last + (c_add - r[0:DK_C, :])
            outs.append(_rms(r[DK_C:DK_C + cs, :] + o_add) * ng_ref[...])
        rows = slice(ch * cs, (ch + 1) * cs)
        o_ref[rows, :] = (jnp.concatenate(outs, axis=-1) * _silu(z_ref[rows, :])).astype(o_ref.dtype)

    half = max(n_chunks // 2, 1)
    for chunks in (range(0, half), range(half, n_chunks)):
        parts = state_free_part(chunks)
        for n, ch in enumerate(chunks):
            state_step(ch, parts[n * H_C:(n + 1) * H_C])
    rider_stages.finish()

    @pl.when(c == pl.num_programs(1) - 1)
    def _():
        sout_ref[...] = s_scr[...]


def _prompt_gdn(qkv, sm, z, p, R, decode=None):
    b, L, _ = qkv.shape
    nj = L // R
    full = lambda a: pl.BlockSpec(a.shape, lambda *_: (0,) * a.ndim)
    params = [p["gdn_conv_w"], p["dtb_small"], p["alog_small"], p["gdn_norm_g"]]
    in_specs = [pl.BlockSpec((None, R, CONV_C), lambda i, j, *_: (i, j, 0)),
                pl.BlockSpec((None, R, SMALL_W), lambda i, j, *_: (i, j, 0)),
                pl.BlockSpec((None, R, D_C), lambda i, j, *_: (i, j, (D_A + D_B) // D_C))] + [full(a) for a in params]
    out_specs = [pl.BlockSpec((None, R, D_C), lambda i, j, *_: (i, j, 0)),
                 pl.BlockSpec((None, H_C * DK_C, DV_C), lambda i, j, *_: (i, 0, 0))]
    out_shape = [jax.ShapeDtypeStruct((b, L, D_C), BF16), jax.ShapeDtypeStruct((b, H_C * DK_C, DV_C), F32)]
    scratch = [pltpu.VMEM((R + SUBLANES, CONV_C), F32), pltpu.VMEM((H_C * DK_C, DV_C), F32),
               pltpu.VMEM((R, CONV_C), F32)]
    args = [qkv, sm, z, *params]
    if decode is None:
        return pl.pallas_call(
            functools.partial(_gdn_kernel, R=R), grid=(b, nj), in_specs=in_specs, out_specs=out_specs,
            out_shape=out_shape, scratch_shapes=scratch, compiler_params=_cparams("arbitrary", "arbitrary"),
        )(*args)
    layer, page_table, lam_init, G, *decode_args = decode
    steps_table, d_in, d_args, d_out, d_shape, d_scratch, spb = _decode_operands(
        layer, G, lambda i, j: i * nj + j, page_table, *decode_args)
    return pl.pallas_call(
        functools.partial(_gdn_kernel, R=R, rider=(G, spb, layer, lam_init)),
        grid_spec=pltpu.PrefetchScalarGridSpec(
            num_scalar_prefetch=1, grid=(b, nj), in_specs=in_specs + d_in, out_specs=out_specs + [d_out],
            scratch_shapes=scratch + d_scratch),
        out_shape=out_shape + [d_shape],
        compiler_params=_cparams("arbitrary", "arbitrary"),
    )(steps_table, *args, *d_args)


def _expand_heads(vals, n_heads, width):
    lane = _iota((1, n_heads * width), 1)
    out = jnp.zeros((1, n_heads * width), F32)
    for h in range(n_heads):
        out = jnp.where((lane >= h * width) & (lane < (h + 1) * width), vals[:, h:h + 1], out)
    return out


def _pad_rows(rows):
    n = rows[0].shape[-1]
    r = _iota((SUBLANES, n), 0)
    out = jnp.zeros((SUBLANES, n), F32)
    for i, v in enumerate(rows):
        out = jnp.where(r == i, v, out)
    return out


def _sample_step_kernel(*refs, n_data, n_params, block_b):
    data, params, outs = refs[:n_data], refs[n_data:n_data + n_params], refs[n_data + n_params:]
    for bi in range(block_b):
        _sample_step_one(*[r.at[bi] for r in data], *params, *[r.at[bi] for r in outs])


def _sample_step_one(xbc_ref, qkv_ref, sm_ref, z_ref, cst_b_ref, cst_c_ref, h0_ref, s0_ref,
                     cwb_ref, cbb_ref, cwc_ref, dtb_s_ref, alog_s_ref, dskip_ref, ngb_ref, ngc_ref,
                     ob_ref, oc_ref, cnew_b_ref, cnew_c_ref, h1_ref, s1_ref):
    def conv(x, st_ref, w_ref, new_ref):
        y = w_ref[CONV_W - 1:CONV_W, :] * x
        for i in range(CONV_W - 1):
            y = y + w_ref[i:i + 1, :] * st_ref[i:i + 1, :]
        for i in range(CONV_W - 2):
            new_ref[i:i + 1, :] = st_ref[i + 1:i + 2, :]
        new_ref[CONV_W - 2:CONV_W - 1, :] = x
        return y

    sm = sm_ref[...]
    dt_or_sp = _softplus(sm + dtb_s_ref[...])
    a_s = dt_or_sp * (-jnp.exp(alog_s_ref[...]))
    z = z_ref[...]

    xa = _silu(conv(xbc_ref[...], cst_b_ref, cwb_ref, cnew_b_ref) + cbb_ref[...])
    xs = xa[:, 0:D_B]
    dt_e = _expand_heads(dt_or_sp[:, DT_LANE:DT_LANE + H_B], H_B, P_B)
    a_e = _expand_heads(a_s[:, DT_LANE:DT_LANE + H_B], H_B, P_B)
    xc = xs * dt_e
    half = D_B // G_B
    lane_b = _iota((1, D_B), 1)
    bm = [xa[:, D_B + g * N_B:D_B + (g + 1) * N_B] for g in range(G_B)]
    cm = [xa[:, D_B + G_B * N_B + g * N_B:D_B + G_B * N_B + (g + 1) * N_B] for g in range(G_B)]
    h0 = h0_ref[...]
    coff = _dot_nt(_pad_rows(cm).astype(BF16), h0.astype(BF16))
    y_off = jnp.where(lane_b < half, coff[0:1, :], coff[1:2, :]) * jnp.exp(a_e)
    cb = [jnp.sum(cm[g] * bm[g], axis=-1, keepdims=True) for g in range(G_B)]
    y_diag = jnp.where(lane_b < half, cb[0], cb[1]) * xc
    y = (y_diag + y_off + xs * dskip_ref[...]) * _silu(z[:, D_A:D_A + D_B])
    y = jnp.concatenate([_rms(y[:, g * half:(g + 1) * half]) for g in range(G_B)], axis=-1)
    ob_ref[...] = y * ngb_ref[...]
    xc_rows = _pad_rows([jnp.where(lane_b < half, xc, 0.0), jnp.where(lane_b >= half, xc, 0.0)])
    outer = _dot_tn(xc_rows.astype(BF16), _pad_rows(bm).astype(BF16))
    state_row = _iota((D_B, N_B), 0)
    dec = jnp.exp(a_s)
    dec_col = jnp.zeros((D_B, N_B), F32)
    for h in range(H_B):
        in_head = (state_row >= h * P_B) & (state_row < (h + 1) * P_B)
        dec_col = jnp.where(in_head, dec[:, DT_LANE + h:DT_LANE + h + 1], dec_col)
    h1_ref[...] = h0 * dec_col + outer

    qkv = _silu(conv(qkv_ref[...], cst_c_ref, cwc_ref, cnew_c_ref))
    beta_s = jax.nn.sigmoid(sm)
    outs = []
    for h in range(H_C):
        q = qkv[:, h * DK_C:(h + 1) * DK_C]
        k = qkv[:, D_C + h * DK_C:D_C + (h + 1) * DK_C]
        v = qkv[:, 2 * D_C + h * DV_C:2 * D_C + (h + 1) * DV_C]
        q = q * lax.rsqrt(jnp.sum(q * q, axis=-1, keepdims=True) + EPS) * (DK_C ** -0.5)
        k = k * lax.rsqrt(jnp.sum(k * k, axis=-1, keepdims=True) + EPS)
        beta = beta_s[:, BETA_LANE + h:BETA_LANE + h + 1]
        eg = jnp.exp(a_s[:, AG_LANE + h:AG_LANE + h + 1])
        s_old = s0_ref[h * DK_C:(h + 1) * DK_C, :]
        ws = _dot(_pad_rows([k * (beta * eg), q * eg]).astype(BF16), s_old.astype(BF16))
        v_new = v * beta - ws[0:1, :]
        o = ws[1:2, :] + jnp.sum(q * k, axis=-1, keepdims=True) * v_new
        upd = _dot_tn(_pad_rows([k]).astype(BF16), _pad_rows([v_new]).astype(BF16))
        s1_ref[h * DK_C:(h + 1) * DK_C, :] = s_old * eg + upd
        outs.append(_rms(o) * ngc_ref[...])
    oc_ref[...] = jnp.concatenate(outs, axis=-1) * _silu(z[:, D_A + D_B:D_MIX])


def _sample_step(layer, xbc, qkv, sm, z, cst_b, cst_c, h0, s0, p):
    bs = xbc.shape[0]
    block_b = _pick(bs, (4, 2, 1))
    per_b = lambda a: pl.BlockSpec((block_b,) + tuple(a.shape[1:]), lambda i: (i,) + (0,) * (len(a.shape) - 1))
    of_layer = lambda a: pl.BlockSpec((None, block_b) + tuple(a.shape[2:]),
                                      lambda i: (layer, i) + (0,) * (len(a.shape) - 2))
    full = lambda a: pl.BlockSpec(a.shape, lambda i: (0,) * a.ndim)
    rows, states = [xbc, qkv, sm, z], [cst_b, cst_c, h0, s0]
    data = rows + states
    params = [p["ssm_conv_w"], p["ssm_conv_b"], p["gdn_conv_w"], p["dtb_small"], p["alog_small"],
              p["ssm_d_e"], p["ssm_norm_g"], p["gdn_norm_g"]]
    outs = [jax.ShapeDtypeStruct((bs, 1, D_B), F32), jax.ShapeDtypeStruct((bs, 1, D_C), F32)] + [
        jax.ShapeDtypeStruct(a.shape[1:], F32) for a in states]
    return pl.pallas_call(
        functools.partial(_sample_step_kernel, n_data=len(data), n_params=len(params), block_b=block_b),
        grid=(bs // block_b,),
        in_specs=[per_b(a) for a in rows] + [of_layer(a) for a in states] + [full(a) for a in params],
        out_specs=[per_b(a) for a in outs],
        out_shape=outs,
        compiler_params=_cparams("arbitrary"),
    )(*data, *params)


def _split_w_in(w):
    sizes = (2 * H_A * DH_A, 2 * H_A * DH_A, D_A, CONV_B, H_B, CONV_C, H_C, H_C, D_MIX)
    offs = [0]
    for s in sizes:
        offs.append(offs[-1] + s)
    col = lambda i, j=None: w[:, offs[i]:offs[(i if j is None else j) + 1]].astype(BF16)
    small = jnp.concatenate([col(4), col(6), col(7),
                             jnp.zeros((w.shape[0], SMALL_W - H_B - 2 * H_C), BF16)], axis=1)
    return [col(0, 3), col(5), col(8), small]


def _layer_params(l, w):
    pad_small = lambda a_ssm, a_gdn: jnp.concatenate(
        [a_ssm, jnp.zeros((H_C,), F32), a_gdn, jnp.zeros((SMALL_W - H_B - 2 * H_C,), F32)]).reshape(1, SMALL_W)
    rep = lambda a: jnp.repeat(a, P_B).reshape(1, D_B)
    expand = (jnp.arange(SMALL_W)[:, None] == (jnp.arange(D_B)[None, :] // P_B)).astype(BF16)
    return {
        "ssm_conv_w": w["ssm_conv_w"][l], "ssm_conv_b": w["ssm_conv_b"][l].reshape(1, CONV_B),
        "gdn_conv_w": w["gdn_conv_w"][l],
        "dtb_small": pad_small(w["ssm_dt_bias"][l], w["gdn_dt_bias"][l]),
        "alog_small": pad_small(w["ssm_A_log"][l], w["gdn_A_log"][l]),
        "ssm_d_e": rep(w["ssm_D"][l]),
        "ssm_norm_g": w["ssm_norm_g"][l].reshape(1, D_B), "gdn_norm_g": w["gdn_norm_g"][l].reshape(1, DV_C),
        "expand": expand,
    }


def _pick(n, prefs):
    for t in prefs:
        if n % t == 0:
            return t
    return n


def kernel(x_prompt, x_sample, c_prompt, c_sample, cache_k, cache_v, state_ssm, state_ssm_conv, state_gdn, state_gdn_conv, page_table, w_ada, b_ada, w_in, w_out, rel_bias, attn_lambda, attn_subln_g, ssm_conv_w, ssm_conv_b, ssm_dt_bias, ssm_A_log, ssm_D, ssm_norm_g, gdn_conv_w, gdn_dt_bias, gdn_A_log, gdn_norm_g, final_norm_g):
    depth = w_in.shape[0]
    bp, L, _ = x_prompt.shape
    bs = x_sample.shape[0]
    n_pages = page_table.shape[1]
    past = n_pages * PAGE_SIZE
    n_pool = cache_k.shape[1]
    wts = dict(ssm_conv_w=ssm_conv_w, ssm_conv_b=ssm_conv_b, ssm_dt_bias=ssm_dt_bias, ssm_A_log=ssm_A_log,
               ssm_D=ssm_D, ssm_norm_g=ssm_norm_g, gdn_conv_w=gdn_conv_w, gdn_dt_bias=gdn_dt_bias,
               gdn_A_log=gdn_A_log, gdn_norm_g=gdn_norm_g)

    tm = _pick(L, (512, 256, 128, 64))
    tm_out = _pick(L, (512, 256, 128, 64))
    T = _pick(L, (256, 128))
    Q = _pick(L, (256, 128, 64))
    R = _pick(L, (256, 128, 64))
    G = _pick(n_pages, (32, 16, 8, 4, 2, 1))
    gdn_steps = bp * (L // R)
    G_ride = (bs * n_pages) // gdn_steps if (bs * n_pages) % gdn_steps == 0 else 0
    if not (1 <= G_ride <= MAX_PAGES_PER_STEP and n_pages % G_ride == 0):
        G_ride = 0

    mod = _modulation(jnp.concatenate([c_prompt, c_sample], axis=0), w_ada, b_ada)
    bias_near = _near_bias_tiles(rel_bias, T)
    bias_pages = _decode_bias(rel_bias, past)
    bias_new = jnp.broadcast_to(jnp.tile(rel_bias[0].astype(F32), 2)[:, None], (2 * H_A, DV_A))
    ck = cache_k.reshape(depth, n_pool, PAGE_SIZE * H_A, 2 * DH_A)
    cv = cache_v.reshape(depth, n_pool, PAGE_SIZE * H_A, DV_A)
    lane_half = jnp.arange(DV_A) < DH_A

    xp = x_prompt
    xs = x_sample.reshape(1, bs, D_MODEL)
    new_p, new_s = [], []
    kv_rows = None
    for l in range(depth):
        lam_init = 0.8 - 0.6 * math.exp(-0.3 * l)
        p = _layer_params(l, wts)
        w_perm = _split_w_in(w_in[l])
        w_o = w_out[l].astype(BF16)
        shift, scale, gate = jnp.split(mod[l], 3, axis=-1)
        mp = lambda a: a[:bp].reshape(bp, 1, D_MODEL)
        ms = lambda a: a[bp:].reshape(1, bs, D_MODEL)

        final = final_norm_g if l == depth - 1 else None
        q, k, v, xbc, qkv, z, sm, *kv_rows = _in_projection(xp, mp(scale), mp(shift), w_perm, tm,
                                                            (l, depth, kv_rows))
        q_s, k_s, v_s, xbc_s, qkv_s, z_s, sm_s = _in_projection(xs, ms(scale), ms(shift), w_perm, bs)
        q4 = q_s.reshape(bs, H_A, DV_A) * (DH_A ** -0.5)
        qpat = jnp.concatenate([jnp.where(lane_half, q4, 0.0), jnp.where(lane_half, 0.0, q4)], axis=1)
        k4 = k_s.reshape(bs, H_A, DV_A)
        v4 = v_s.reshape(bs, H_A, DV_A)
        decode_args = (ck, cv, qpat, jnp.concatenate([k4, k4], axis=1), jnp.concatenate([v4, v4], axis=1),
                       bias_pages, bias_new, z_s[..., :D_A].reshape(bs, H_A, DV_A), attn_lambda[l],
                       attn_subln_g[l])

        o_a = _prompt_attention(q, k, v, z, bias_near, attn_lambda[l], attn_subln_g[l], lam_init, T)
        o_b, ssm_h = _prompt_ssd(xbc, sm, z, p, Q)
        if G_ride:
            o_c, gdn_s, o_a_s = _prompt_gdn(qkv, sm, z, p, R, (l, page_table, lam_init, G_ride, *decode_args))
        else:
            o_c, gdn_s = _prompt_gdn(qkv, sm, z, p, R)
            o_a_s = _decode_attention(l, page_table, lam_init, G, *decode_args)
        xp = _out_projection(xp, mp(gate), o_a, o_b, o_c, w_o, final, tm_out)
        new_p.append((ssm_h.reshape(bp, H_B, P_B, N_B), xbc[:, L - (CONV_W - 1):, :],
                      gdn_s.reshape(bp, H_C, DK_C, DV_C), qkv[:, L - (CONV_W - 1):, :]))

        row = lambda a: a.reshape(bs, 1, a.shape[-1])
        o_b_s, o_c_s, conv_b1, conv_c1, ssm_h_s, gdn_s_s = _sample_step(
            l, row(xbc_s), row(qkv_s), row(sm_s), row(z_s), state_ssm_conv, state_gdn_conv,
            state_ssm.reshape(depth, bs, D_B, N_B), state_gdn.reshape(depth, bs, H_C * DK_C, DV_C), p)
        xs = _out_projection(xs, ms(gate), o_a_s.reshape(1, bs, D_A).astype(BF16),
                             o_b_s.reshape(1, bs, D_B).astype(BF16), o_c_s.reshape(1, bs, D_C).astype(BF16),
                             w_o, final, bs)
        new_s.append((k_s.reshape(bs, 1, H_A, 2 * DH_A), v_s.reshape(bs, 1, H_A, DV_A),
                      ssm_h_s.reshape(bs, H_B, P_B, N_B), conv_b1,
                      gdn_s_s.reshape(bs, H_C, DK_C, DV_C), conv_c1))

    k_p, v_p = kv_rows
    ssm_p, ssm_conv_p, gdn_p, gdn_conv_p = [jnp.stack(a) for a in zip(*new_p)]
    k_s, v_s, ssm_s, ssm_conv_s, gdn_s, gdn_conv_s = [jnp.stack(a) for a in zip(*new_s)]
    return (xp, xs.reshape(bs, 1, D_MODEL), k_p, v_p, ssm_p, ssm_conv_p, gdn_p, gdn_conv_p,
            k_s, v_s, ssm_s, ssm_conv_s, gdn_s, gdn_conv_s)
```

```python
import functools
import math

import jax
import jax.numpy as jnp
from jax import lax
from jax.experimental import pallas as pl
from jax.experimental.pallas import tpu as pltpu

F32 = jnp.float32
BF16 = jnp.bfloat16

D_MODEL = 1024
H_A, DH_A, DV_A = 4, 64, 128
D_A = H_A * DV_A
H_B, P_B, G_B, N_B = 8, 64, 2, 128
D_B = H_B * P_B
CONV_B = D_B + 2 * G_B * N_B
H_C, DK_C, DV_C = 4, 128, 128
D_C = H_C * DV_C
CONV_C = 2 * H_C * DK_C + D_C
CONV_W = 4
D_MIX = D_A + D_B + D_C
REL_BUCKETS, REL_MAX_DIST = 32, 128
PAGE_SIZE = 128
EPS = 1e-6
GDN_CHUNK = 64

LANES = 128
SUBLANES = 8
SMALL_W = LANES
DT_LANE, BETA_LANE, AG_LANE = 0, H_B, H_B + H_C
NEG_BIG = -1e30
LOG2E = math.log2(math.e)
ATTN_Q_SCALE = DH_A ** -0.5 * LOG2E
ATTN_STRIP = 64
MAX_PAGES_PER_STEP = 32
V7X_VMEM_BYTES = 64 * 1024 * 1024
VMEM_LIMIT = V7X_VMEM_BYTES * 7 // 8

SEGMENTS = (("q", 2 * H_A * DH_A), ("k", 2 * H_A * DH_A), ("v", D_A), ("xbc", CONV_B),
            ("qkv", CONV_C), ("z", D_MIX), ("small", SMALL_W))
SEGMENT_GROUPS = (("q", "k", "v", "xbc"), ("qkv",), ("z",), ("small",))


def _cparams(*sem):
    return pltpu.CompilerParams(dimension_semantics=sem, vmem_limit_bytes=VMEM_LIMIT)


def _silu(x):
    return x * jax.nn.sigmoid(x)


def _softplus(x):
    return jnp.maximum(x, 0.0) + jnp.log1p(jnp.exp(-jnp.abs(x)))


def _rms(x):
    return x * lax.rsqrt(jnp.mean(x * x, axis=-1, keepdims=True) + EPS)


def _dot_nt(a, b):
    return lax.dot_general(a, b, (((1,), (1,)), ((), ())), preferred_element_type=F32)


def _dot_tn(a, b):
    return lax.dot_general(a, b, (((0,), (0,)), ((), ())), preferred_element_type=F32)


def _dot(a, b):
    return jnp.dot(a, b, preferred_element_type=F32)


def _iota(shape, dim):
    return lax.broadcasted_iota(jnp.int32, shape, dim)


def _mod_kernel(c_ref, w_ref, b_ref, o_ref):
    sc = _silu(c_ref[...]).astype(BF16)
    o_ref[...] = _dot(sc, w_ref[...].astype(BF16)) + b_ref[...]


def _modulation(c_all, w_ada, b_ada):
    depth = w_ada.shape[0]
    n = c_all.shape[0]
    return pl.pallas_call(
        _mod_kernel,
        grid=(depth, 3),
        in_specs=[pl.BlockSpec((n, D_MODEL), lambda l, j: (0, 0)),
                  pl.BlockSpec((None, D_MODEL, D_MODEL), lambda l, j: (l, 0, j)),
                  pl.BlockSpec((None, 1, D_MODEL), lambda l, j: (l, 0, j))],
        out_specs=pl.BlockSpec((None, n, D_MODEL), lambda l, j: (l, 0, j)),
        out_shape=jax.ShapeDtypeStruct((depth, n, 3 * D_MODEL), F32),
        compiler_params=_cparams("arbitrary", "arbitrary"),
    )(c_all, w_ada, b_ada.reshape(depth, 1, 3 * D_MODEL))


def _inproj_kernel(x_ref, scale_ref, shift_ref, *rest, heads_out, n_carried):
    n_groups = len(SEGMENT_GROUPS)
    w_refs = rest[:n_groups]
    out_refs = rest[n_groups + n_carried:]
    h = _rms(x_ref[...]) * (1.0 + scale_ref[...]) + shift_ref[...]
    hb = h.astype(BF16)
    seg_refs = dict(zip((name for name, _ in SEGMENTS), out_refs))
    head_refs = dict(zip(("k", "v"), out_refs[len(SEGMENTS):])) if heads_out else {}
    widths = dict(SEGMENTS)
    for w_ref, names in zip(w_refs, SEGMENT_GROUPS):
        off = 0
        for name in names:
            n = widths[name]
            res = _dot(hb, w_ref[:, off:off + n])
            scaled = res * ATTN_Q_SCALE if heads_out and name == "q" else res
            seg_refs[name][...] = scaled.astype(seg_refs[name].dtype)
            if name in head_refs:
                ref = head_refs[name]
                for hd in range(H_A):
                    if len(ref.shape) == 3:
                        ref[:, hd, :] = res[:, hd * DV_A:(hd + 1) * DV_A]
                    else:
                        for d in range(ref.shape[0]):
                            ref[d, :, hd, :] = res[:, hd * DV_A:(hd + 1) * DV_A]
            off += n


def _in_projection(x, scale, shift, w_groups, tm, cache_rows=None):
    b, L, _ = x.shape
    ms = scale.shape[1]
    mod_rows = tm if ms == L else 1
    mod_map = (lambda i, j: (i, j, 0)) if ms == L else (lambda i, j: (i, 0, 0))
    heads_out = cache_rows is not None
    dt = lambda name: BF16 if heads_out and name in ("q", "k", "v") else F32
    out_specs = [pl.BlockSpec((None, tm, w), lambda i, j: (i, j, 0)) for _, w in SEGMENTS]
    out_shape = [jax.ShapeDtypeStruct((b, L, w), dt(name)) for name, w in SEGMENTS]
    carried, aliases = (), {}
    if heads_out:
        layer, depth, prev = cache_rows
        if prev is None:
            out_specs += [pl.BlockSpec((depth, None, tm, H_A, DV_A), lambda i, j: (0, i, j, 0, 0))] * 2
        else:
            out_specs += [pl.BlockSpec((None, None, tm, H_A, DV_A), lambda i, j: (layer, i, j, 0, 0))] * 2
        out_shape += [jax.ShapeDtypeStruct((depth, b, L, H_A, DV_A), F32)] * 2
        if prev is not None:
            carried = tuple(prev)
            aliases = {3 + len(w_groups) + n: len(SEGMENTS) + n for n in range(len(carried))}
    return pl.pallas_call(
        functools.partial(_inproj_kernel, heads_out=heads_out, n_carried=len(carried)),
        grid=(b, L // tm),
        in_specs=[pl.BlockSpec((None, tm, D_MODEL), lambda i, j: (i, j, 0)),
                  pl.BlockSpec((None, mod_rows, D_MODEL), mod_map),
                  pl.BlockSpec((None, mod_rows, D_MODEL), mod_map)]
                 + [pl.BlockSpec(w.shape, lambda i, j: (0, 0), pipeline_mode=pl.Buffered(1)) for w in w_groups]
                 + [pl.BlockSpec(memory_space=pl.ANY)] * len(carried),
        out_specs=out_specs,
        out_shape=out_shape,
        input_output_aliases=aliases,
        compiler_params=_cparams("arbitrary", "arbitrary"),
    )(x, scale, shift, *w_groups, *carried)


def _outproj_kernel(x_ref, gate_ref, oa_ref, ob_ref, oc_ref, w_ref, *rest, final):
    acc = _dot(oa_ref[...], w_ref[0:D_A, :])
    acc += _dot(ob_ref[...], w_ref[D_A:D_A + D_B, :])
    acc += _dot(oc_ref[...], w_ref[D_A + D_B:D_MIX, :])
    y = x_ref[...] + gate_ref[...] * acc
    if final:
        g_ref, o_ref = rest
        o_ref[...] = _rms(y) * g_ref[...]
    else:
        (o_ref,) = rest
        o_ref[...] = y


def _out_projection(x, gate, oa, ob, oc, w_out, final_g, tm):
    b, L, _ = x.shape
    ms = gate.shape[1]
    mod_rows = tm if ms == L else 1
    mod_map = (lambda i, j: (i, j, 0)) if ms == L else (lambda i, j: (i, 0, 0))
    row = lambda w: pl.BlockSpec((None, tm, w), lambda i, j: (i, j, 0))
    in_specs = [row(D_MODEL), pl.BlockSpec((None, mod_rows, D_MODEL), mod_map),
                row(D_A), row(D_B), row(D_C),
                pl.BlockSpec((D_MIX, D_MODEL), lambda i, j: (0, 0))]
    args = [x, gate, oa, ob, oc, w_out]
    final = final_g is not None
    if final:
        in_specs.append(pl.BlockSpec((1, D_MODEL), lambda i, j: (0, 0)))
        args.append(final_g.reshape(1, D_MODEL))
    return pl.pallas_call(
        functools.partial(_outproj_kernel, final=final),
        grid=(b, L // tm),
        in_specs=in_specs,
        out_specs=row(D_MODEL),
        out_shape=jax.ShapeDtypeStruct((b, L, D_MODEL), F32),
        compiler_params=_cparams("arbitrary", "arbitrary"),
    )(*args)


def _lambda_value(lp, lam_init):
    s01 = jnp.sum(lp[0:1, :] * lp[1:2, :], axis=-1, keepdims=True)
    s23 = jnp.sum(lp[2:3, :] * lp[3:4, :], axis=-1, keepdims=True)
    return jnp.exp(s01) - jnp.exp(s23) + lam_init


def _rel_bias_values(table, n):
    exact = REL_BUCKETS // 2
    large = exact + (jnp.log(jnp.maximum(n, 1).astype(F32) / exact)
                     / math.log(REL_MAX_DIST / exact) * (REL_BUCKETS - exact)).astype(jnp.int32)
    bucket = jnp.where(n < exact, n, jnp.minimum(large, REL_BUCKETS - 1))
    onehot = bucket[..., None, None] == jnp.arange(REL_BUCKETS)[:, None]
    return jnp.sum(jnp.where(onehot, table.astype(F32), 0.0), axis=-2)


def _attn_kernel(q_ref, kb_scr, vb_scr, z_ref, bias_ref, lamp_ref, subln_ref, o_ref, s_bufs, w_bufs,
                 vaug_scr, *, T, lam_init):
    L = q_ref.shape[0]
    nq = L // T
    lam = _lambda_value(lamp_ref[...], lam_init)
    lane = _iota((T, DV_A), 1)
    half = LANES

    def logit_tasks(i):
        q = q_ref[i * T:(i + 1) * T, :]
        s_scr = s_bufs.at[i % 2]
        tasks = []
        for sub in range(2):
            qm = jnp.where((lane < DH_A) if sub == 0 else (lane >= DH_A), q, jnp.zeros_like(q))
            for j in range(i + 1):
                def task(qm=qm, sub=sub, j=j):
                    s = _dot_nt(qm, kb_scr[j * T:(j + 1) * T, :])
                    if j == i:
                        s = s + bias_ref[:, T:2 * T]
                    elif j == i - 1:
                        s = s + bias_ref[:, 0:T]
                    s_scr[sub, j] = s
                tasks.append(task)
        return tasks

    def softmax_strip(i, r0):
        s_scr, w_scr = s_bufs.at[i % 2], w_bufs.at[i % 2]
        strip = slice(r0, r0 + ATTN_STRIP)
        for sub in range(2):
            mx = None
            for j in range(i + 1):
                s = s_scr[sub, j, strip, :]
                for c0 in range(0, T, half):
                    t = s[:, c0:c0 + half]
                    mx = t if mx is None else jnp.maximum(mx, t)
            m_row = jnp.max(mx, axis=-1, keepdims=True)
            for j in range(i + 1):
                w_scr[sub, strip, j * T:(j + 1) * T] = jnp.exp2(s_scr[sub, j, strip, :] - m_row).astype(BF16)

    vaug_scr[:, 0:DV_A] = vb_scr[...]
    vaug_scr[:, DV_A:2 * DV_A] = (_iota((L, DV_A), 1) == 0).astype(BF16)

    for task in logit_tasks(0):
        task()
    for i in range(nq):
        nxt = logit_tasks(i + 1) if i + 1 < nq else []
        strips = list(range(0, T, ATTN_STRIP))
        per = -(-len(nxt) // len(strips))
        for n, r0 in enumerate(strips):
            softmax_strip(i, r0)
            for task in nxt[n * per:(n + 1) * per]:
                task()
        rows = slice(i * T, (i + 1) * T)
        nk = (i + 1) * T
        outs = []
        for sub in range(2):
            acc = _dot(w_bufs[i % 2, sub, :, 0:nk], vaug_scr[0:nk, :])
            outs.append(acc[:, 0:DV_A] / acc[:, DV_A:DV_A + 1])
        o = outs[0] - lam * outs[1]
        o = _rms(o) * subln_ref[...] * (1.0 - lam_init) * _silu(z_ref[rows, :])
        o_ref[rows, :] = o.astype(o_ref.dtype)


def _prompt_attention(q, k, v, z, bias_near, lam_p, subln, lam_init, T):
    b, L, _ = q.shape
    seq = lambda: pl.BlockSpec((None, L, DV_A), lambda i, h: (i, 0, h))
    return pl.pallas_call(
        functools.partial(_attn_kernel, T=T, lam_init=lam_init),
        grid=(b, H_A),
        in_specs=[seq(), seq(), seq(), seq(),
                  pl.BlockSpec((None, T, 2 * T), lambda i, h: (h, 0, 0)),
                  pl.BlockSpec((4, DH_A), lambda i, h: (0, 0)),
                  pl.BlockSpec((1, DV_A), lambda i, h: (0, 0))],
        out_specs=seq(),
        out_shape=jax.ShapeDtypeStruct((b, L, D_A), BF16),
        scratch_shapes=[pltpu.VMEM((2, 2, L // T, T, T), F32), pltpu.VMEM((2, 2, T, L), BF16),
                        pltpu.VMEM((L, 2 * DV_A), BF16)],
        compiler_params=_cparams("arbitrary", "arbitrary"),
    )(q, k, v, z, bias_near, lam_p, subln.reshape(1, DV_A))


def _near_bias_tiles(table, T):
    period = 3 * T
    k = jnp.arange(period)
    n = jnp.where(k < 2 * T, T - k, T + period - k)
    vals = _rel_bias_values(table, jnp.maximum(n, 0)) - table[REL_BUCKETS - 1].astype(F32)
    w = jnp.where((n >= 0)[:, None], vals * LOG2E, NEG_BIG).T
    skew = jnp.tile(w, (1, T + 1))[:, :T * (period - 1)].reshape(H_A, T, period - 1)
    return skew[:, :, :2 * T]


N_DECODE_IN = 10


def _decode_stages(pt_ref, in_refs, o_ref, scratch, step, n_steps, first, G, layer, lam_init):
    (qpat_ref, knew_ref, vnew_ref, bias_ref, biasnew_ref, z_ref, lamp_ref, subln_ref,
     cache_k_ref, cache_v_ref) = in_refs
    m_scr, l_scr, acc_scr, kbuf, vbuf, sem = scratch
    slot = lax.rem(step, 2)

    def page_copy(cache_ref, buf, which, page, to_slot, g):
        return pltpu.make_async_copy(cache_ref.at[layer, page], buf.at[to_slot, g], sem.at[to_slot, g, which])

    def start_step(s, to_slot):
        for g in range(G):
            page = pt_ref[s, g]
            page_copy(cache_k_ref, kbuf, 0, page, to_slot, g).start()
            page_copy(cache_v_ref, vbuf, 1, page, to_slot, g).start()

    @pl.when(step == 0)
    def _():
        start_step(0, 0)

    @pl.when(step + 1 < n_steps)
    def _():
        start_step(step + 1, 1 - slot)

    @pl.when(first)
    def _():
        m_scr[...] = jnp.full(m_scr.shape, NEG_BIG, F32)
        l_scr[...] = jnp.zeros(l_scr.shape, F32)
        acc_scr[...] = jnp.zeros(acc_scr.shape, F32)

    for g in range(G):
        page = pt_ref[step, g]
        page_copy(cache_k_ref, kbuf, 0, page, slot, g).wait()
        page_copy(cache_v_ref, vbuf, 1, page, slot, g).wait()

    qf = qpat_ref[...]
    qb = qf.astype(BF16)
    st = {"s": [None] * G, "pv": None}

    def logits(g):
        st["s"][g] = _dot_nt(qb, kbuf[slot, g].astype(BF16)) + bias_ref[g]

    def softmax_stats():
        s = st["s"]
        m_old, l_old = m_scr[...], l_scr[...]
        m_tile = s[0]
        for g in range(1, G):
            m_tile = jnp.maximum(m_tile, s[g])
        m = jnp.maximum(m_old, jnp.max(m_tile, axis=-1, keepdims=True))
        st["alpha"] = jnp.exp(m_old - m)
        p = [jnp.exp(s[g] - m) for g in range(G)]
        p_sum = p[0]
        for g in range(1, G):
            p_sum = p_sum + p[g]
        st["p"] = [x.astype(BF16) for x in p]
        st["m"] = m
        st["l"] = st["alpha"] * l_old + jnp.sum(p_sum, axis=-1, keepdims=True)

    def values(g):
        d =_dot(st["p"][g], vbuf[slot, g].astype(BF16))
        st["pv"] = d if st["pv"] is None else st["pv"] + d

    def finish():
        m, l = st["m"], st["l"]
        acc = st["alpha"] * acc_scr[...] + st["pv"]
        m_scr[...] = m
        l_scr[...] = l
        acc_scr[...] = acc
        s_new = jnp.sum(qf * knew_ref[...], axis=-1, keepdims=True) + biasnew_ref[:, 0:1]
        m_new = jnp.maximum(m, s_new)
        alpha = jnp.exp(m - m_new)
        p_new = jnp.exp(s_new - m_new)
        l_fin = alpha * l + p_new
        out = (alpha * acc + p_new * vnew_ref[...]) / l_fin
        lam = _lambda_value(lamp_ref[...], lam_init)
        o = out[0:H_A, :] - lam * out[H_A:2 * H_A, :]
        o_ref[...] = _rms(o) * subln_ref[...] * (1.0 - lam_init) * _silu(z_ref[...])

    return ([functools.partial(logits, g) for g in range(G)] + [softmax_stats]
            + [functools.partial(values, g) for g in range(G)] + [finish])


class _Interleaver:
    def __init__(self, thunks, n_ticks):
        self.thunks, self.n_ticks, self.ticks, self.done = thunks, n_ticks, 0, 0

    def tick(self):
        self.ticks += 1
        target = min(len(self.thunks), -(-len(self.thunks) * self.ticks // self.n_ticks))
        while self.done < target:
            self.thunks[self.done]()
            self.done += 1

    def finish(self):
        self.ticks = self.n_ticks - 1
        self.tick()


def _decode_operands(layer, G, flat_step, page_table, cache_k, cache_v, qpat, knew, vnew, bias_pages,
                     bias_new, z3, lam_p, subln):
    bs, n_pages = page_table.shape
    spb = n_pages // G
    rows = PAGE_SIZE * H_A
    if spb & (spb - 1) == 0:
        seq = lambda ids: lax.shift_right_logical(flat_step(*ids), spb.bit_length() - 1)
        grp = lambda ids: flat_step(*ids) & (spb - 1)
    else:
        seq = lambda ids: flat_step(*ids) // spb
        grp = lambda ids: flat_step(*ids) % spb

    per_b = lambda r: pl.BlockSpec((None, r, DV_A), lambda *a: (seq(a[:-1]), 0, 0))
    const = lambda shape: pl.BlockSpec(shape, lambda *a: (0,) * len(shape))
    in_specs = [per_b(2 * H_A), per_b(2 * H_A), per_b(2 * H_A),
                pl.BlockSpec((G, 2 * H_A, rows), lambda *a: (grp(a[:-1]), 0, 0)),
                const((2 * H_A, DV_A)), per_b(H_A), const((4, DH_A)), const((1, DV_A)),
                pl.BlockSpec(memory_space=pl.ANY), pl.BlockSpec(memory_space=pl.ANY)]
    args = [qpat, knew, vnew, bias_pages, bias_new, z3, lam_p, subln.reshape(1, DV_A), cache_k, cache_v]
    scratch = [pltpu.VMEM((2 * H_A, 1), F32), pltpu.VMEM((2 * H_A, 1), F32), pltpu.VMEM((2 * H_A, DV_A), F32),
               pltpu.VMEM((2, G, rows, DV_A), F32), pltpu.VMEM((2, G, rows, DV_A), F32),
               pltpu.SemaphoreType.DMA((2, G, 2))]
    steps_table = page_table.reshape(bs * spb, G)
    return steps_table, in_specs, args, per_b(H_A), jax.ShapeDtypeStruct((bs, H_A, DV_A), F32), scratch, spb


def _decode_kernel(pt_ref, *refs, G, layer, lam_init):
    step = pl.program_id(0) * pl.num_programs(1) + pl.program_id(1)
    n_steps = pl.num_programs(0) * pl.num_programs(1)
    for stage in _decode_stages(pt_ref, refs[:N_DECODE_IN], refs[N_DECODE_IN], refs[N_DECODE_IN + 1:], step,
                                n_steps, pl.program_id(1) == 0, G, layer, lam_init):
        stage()


def _decode_attention(layer, page_table, lam_init, G, *decode_args):
    bs, n_pages = page_table.shape
    steps_table, in_specs, args, out_spec, out_shape, scratch, spb = _decode_operands(
        layer, G, lambda i, j: i * (n_pages // G) + j, page_table, *decode_args)
    return pl.pallas_call(
        functools.partial(_decode_kernel, G=G, layer=layer, lam_init=lam_init),
        grid_spec=pltpu.PrefetchScalarGridSpec(num_scalar_prefetch=1, grid=(bs, spb), in_specs=in_specs,
                                               out_specs=out_spec, scratch_shapes=scratch),
        out_shape=out_shape,
        compiler_params=_cparams("arbitrary", "arbitrary"),
    )(steps_table, *args)


def _decode_bias(table, past):
    n = past - jnp.arange(past)
    vals = _rel_bias_values(table, n).reshape(past // PAGE_SIZE, PAGE_SIZE, H_A)
    vals = jnp.moveaxis(vals, -1, 1)[..., None]
    same = (jnp.arange(H_A)[:, None, None] == jnp.arange(H_A)[None, None, :])
    full = jnp.where(same[None], vals, NEG_BIG).reshape(past // PAGE_SIZE, H_A, PAGE_SIZE * H_A)
    return jnp.concatenate([full, full], axis=1)


def _conv_taps(xp_ref, x, w_ref, rows):
    xp_ref[SUBLANES:SUBLANES + rows, :] = x
    xe = xp_ref[...]
    t = w_ref[0:1, :] * xe
    for i in range(1, CONV_W):
        t = w_ref[i:i + 1, :] * xe + pltpu.roll(t, 1, axis=0)
    xp_ref[0:SUBLANES, :] = x[rows - SUBLANES:rows, :]
    return t[SUBLANES:SUBLANES + rows, :]


def _ssd_kernel(xbc_ref, sm_ref, z_ref, cw_ref, cb_ref, dtb_s_ref, alog_s_ref,
                dskip_ref, ng_ref, exp_ref, y_ref, hout_ref, xp_scr, h_scr, *, Q):
    c = pl.program_id(1)

    @pl.when(c == 0)
    def _():
        xp_scr[0:SUBLANES, :] = jnp.zeros((SUBLANES, CONV_B), F32)
        h_scr[...] = jnp.zeros(h_scr.shape, F32)

    xa = _silu(_conv_taps(xp_scr, xbc_ref[...], cw_ref, Q) + cb_ref[...])
    xs = xa[:, 0:D_B]
    sm = sm_ref[...]
    row = _iota((Q, Q), 0)
    col = _iota((Q, Q), 1)
    tril = row >= col
    expand = exp_ref[...]

    dt_s = _softplus(sm + dtb_s_ref[...])
    a_s = dt_s * (-jnp.exp(alog_s_ref[...]))
    cum_s = _dot_exact_rhs(tril.astype(BF16), a_s)
    cum_t = cum_s.T
    dt_e = _dot_exact_lhs(dt_s, expand)
    cum_e = _dot_exact_lhs(cum_s, expand)
    xc = xs * dt_e
    dec_end = jnp.exp(cum_e[Q - 1:Q, :] - cum_e)
    dec_in = jnp.exp(cum_e)
    xc_b = xc.astype(BF16)
    xe_b = (xc * dec_end).astype(BF16)
    lane = _iota((Q, LANES), 1)
    prow = _iota((LANES, LANES), 0)
    heads_per_group = H_B // G_B
    pairs = []
    for g in range(G_B):
        bm = xa[:, D_B + g * N_B:D_B + (g + 1) * N_B].astype(BF16)
        cm = xa[:, D_B + G_B * N_B + g * N_B:D_B + G_B * N_B + (g + 1) * N_B].astype(BF16)
        cb = _dot_nt(cm, bm)
        for pr in range(g * heads_per_group // 2, (g + 1) * heads_per_group // 2):
            lo, hi = pr * LANES, (pr + 1) * LANES
            yd = []
            for hh in (2 * pr, 2 * pr + 1):
                seg = cum_s[:, hh:hh + 1] - cum_t[hh:hh + 1, :]
                mat = cb * jnp.exp(jnp.where(tril, seg, -jnp.inf))
                yd.append(_dot(mat.astype(BF16), xc_b[:, lo:hi]))
            y_diag = jnp.where(lane < P_B, yd[0], yd[1])
            hp = h_scr[lo:hi, :]
            y_off = _dot_nt(cm, hp.astype(BF16)) * dec_in[:, lo:hi]
            st = _dot_tn(xe_b[:, lo:hi], bm)
            a_last = jnp.where(prow < P_B, cum_t[2 * pr:2 * pr + 1, Q - 1:Q],
                               cum_t[2 * pr + 1:2 * pr + 2, Q - 1:Q])
            h_scr[lo:hi, :] = hp * jnp.exp(a_last) + st
            pairs.append(y_diag + y_off + xs[:, lo:hi] * dskip_ref[:, lo:hi])
    y = jnp.concatenate(pairs, axis=-1) * _silu(z_ref[...])
    gw = D_B // G_B
    y = jnp.concatenate([_rms(y[:, g * gw:(g + 1) * gw]) for g in range(G_B)], axis=-1) * ng_ref[...]
    y_ref[...] = y.astype(y_ref.dtype)

    @pl.when(c == pl.num_programs(1) - 1)
    def _():
        hout_ref[...] = h_scr[...]


def _prompt_ssd(xbc, sm, z, p, Q):
    b, L, _ = xbc.shape
    full = lambda a: pl.BlockSpec(a.shape, lambda i, j: (0,) * a.ndim)
    params = [p["ssm_conv_w"], p["ssm_conv_b"], p["dtb_small"], p["alog_small"],
              p["ssm_d_e"], p["ssm_norm_g"], p["expand"]]
    return pl.pallas_call(
        functools.partial(_ssd_kernel, Q=Q),
        grid=(b, L // Q),
        in_specs=[pl.BlockSpec((None, Q, CONV_B), lambda i, j: (i, j, 0)),
                  pl.BlockSpec((None, Q, SMALL_W), lambda i, j: (i, j, 0)),
                  pl.BlockSpec((None, Q, D_B), lambda i, j: (i, j, D_A // D_B))]
                 + [full(a) for a in params],
        out_specs=[pl.BlockSpec((None, Q, D_B), lambda i, j: (i, j, 0)),
                   pl.BlockSpec((None, D_B, N_B), lambda i, j: (i, 0, 0))],
        out_shape=[jax.ShapeDtypeStruct((b, L, D_B), BF16),
                   jax.ShapeDtypeStruct((b, D_B, N_B), F32)],
        scratch_shapes=[pltpu.VMEM((Q + SUBLANES, CONV_B), F32), pltpu.VMEM((D_B, N_B), F32)],
        compiler_params=_cparams("arbitrary", "arbitrary"),
    )(xbc, sm, z, *params)


def _split2(a):
    hi = a.astype(BF16)
    return hi, (a - hi.astype(F32)).astype(BF16)


def _split3(x):
    x1 = x.astype(BF16)
    r1 = x - x1.astype(F32)
    x2 = r1.astype(BF16)
    return x1, x2, (r1 - x2.astype(F32)).astype(BF16)


def _dot_exact_rhs(a_bf16, x):
    x1, x2, x3 = _split3(x)
    return _dot(a_bf16, x1) + (_dot(a_bf16, x2) + _dot(a_bf16, x3))


def _dot_exact_lhs(x, a_bf16):
    x1, x2, x3 = _split3(x)
    return _dot(x1, a_bf16) + (_dot(x2, a_bf16) + _dot(x3, a_bf16))


def _gdn_head_prepare(act, gcum, gcum_t, beta_s, r0, h):
    cs = GDN_CHUNK
    rows = slice(r0, r0 + cs)
    q = act[rows, h * DK_C:(h + 1) * DK_C]
    k = act[rows, D_C + h * DK_C:D_C + (h + 1) * DK_C]
    v = act[rows, 2 * D_C + h * DV_C:2 * D_C + (h + 1) * DV_C]
    q = q * lax.rsqrt(jnp.sum(q * q, axis=-1, keepdims=True) + EPS) * (DK_C ** -0.5)
    k = k * lax.rsqrt(jnp.sum(k * k, axis=-1, keepdims=True) + EPS)
    beta = beta_s[rows, BETA_LANE + h:BETA_LANE + h + 1]
    gc = gcum[rows, AG_LANE + h:AG_LANE + h + 1]
    gr = gcum_t[AG_LANE + h:AG_LANE + h + 1, rows]
    g_last = gcum[r0 + cs - 1:r0 + cs, AG_LANE + h:AG_LANE + h + 1]
    ri = _iota((cs, 2 * cs), 0)
    ci = _iota((cs, 2 * cs), 1) & (cs - 1)
    decay = jnp.exp(jnp.where(ri >= ci, gc - jnp.concatenate([gr, gr], axis=1), -jnp.inf))
    kb = k.astype(BF16)
    kk = _dot_nt(kb, jnp.concatenate([kb, kb], axis=0))
    p = jnp.where(ri > ci, -(beta * kk * decay), 0.0)
    rhs = jnp.concatenate([v * beta, k * (beta * jnp.exp(gc))], axis=-1)
    attn = (_dot_nt(q.astype(BF16), kb) * decay[:, 0:cs]).astype(BF16)
    qg = q * jnp.exp(gc)
    kd = (k * jnp.exp(g_last - gc)).astype(BF16)
    return p, rhs, attn, qg, kd, jnp.exp(g_last)


def _gdn_head_finish(sol, attn, qg, kd, e_last):
    x_b = sol.astype(BF16)
    ax = _dot(attn, x_b)
    kx = _dot_tn(kd, x_b)
    lhs = jnp.concatenate([kx[:, DV_C:2 * DV_C], qg - ax[:, DV_C:2 * DV_C]], axis=0).astype(BF16)
    return lhs, kx[:, 0:DV_C], ax[:, 0:DV_C], e_last


def _split_lhs(p):
    hi, lo = _split2(p)
    half = jnp.where(_iota(p.shape, 1) < GDN_CHUNK, hi, lo)
    return jnp.concatenate([half, half], axis=1)


def _split_rhs(x):
    hi, lo = _split2(x)
    return jnp.concatenate([hi, hi, lo, lo], axis=0)


def _neumann_solve(ps, xs, tick):
    cs = GDN_CHUNK
    n_levels = cs.bit_length() - 1
    eye = (_iota((cs, 2 * cs), 0) == (_iota((cs, 2 * cs), 1) & (cs - 1))).astype(F32)
    ts = [eye + p for p in ps]
    tick()
    for _ in range(1, n_levels):
        ps = [_dot(_split_lhs(p), _split_rhs(p)) for p in ps]
        ts = [t + _dot(_split_lhs(p), _split_rhs(t)) for p, t in zip(ps, ts)]
        tick()
    return [_dot(_split_lhs(t), _split_rhs(x)) for t, x in zip(ts, xs)]


N_GDN_IN, N_GDN_OUT, N_GDN_SCRATCH = 7, 2, 3


def _gdn_kernel(*refs, R, rider=None):
    if rider is None:
        ins, refs = refs[:N_GDN_IN], refs[N_GDN_IN:]
        outs, scratch = refs[:N_GDN_OUT], refs[N_GDN_OUT:]
    else:
        G, spb, layer, lam_init = rider
        n_dec = N_DECODE_IN
        pt_ref, refs = refs[0], refs[1:]
        ins, dec_ins, refs = refs[:N_GDN_IN], refs[N_GDN_IN:N_GDN_IN + n_dec], refs[N_GDN_IN + n_dec:]
        outs, dec_out, refs = refs[:N_GDN_OUT], refs[N_GDN_OUT], refs[N_GDN_OUT + 1:]
        scratch, dec_scratch = refs[:N_GDN_SCRATCH], refs[N_GDN_SCRATCH:]
    qkv_ref, sm_ref, z_ref, cw_ref, dtb_s_ref, alog_s_ref, ng_ref = ins
    o_ref, sout_ref = outs
    xp_scr, s_scr, act_scr = scratch
    c = pl.program_id(1)

    @pl.when(c == 0)
    def _():
        xp_scr[0:SUBLANES, :] = jnp.zeros((SUBLANES, CONV_C), F32)
        s_scr[...] = jnp.zeros(s_scr.shape, F32)

    cs = GDN_CHUNK
    n_chunks = R // cs
    if rider is not None:
        step = pl.program_id(0) * pl.num_programs(1) + c
        n_steps = pl.num_programs(0) * pl.num_programs(1)
        stages = _decode_stages(pt_ref, dec_ins, dec_out, dec_scratch, step, n_steps,
                                lax.rem(step, spb) == 0, G, layer, lam_init)
        n_ticks = n_chunks * (H_C + 1) + 2 * (GDN_CHUNK.bit_length() - 1)
        rider_stages = _Interleaver(stages, n_ticks)
    else:
        rider_stages = _Interleaver([], 1)
    tick = rider_stages.tick

    act_scr[...] = _silu(_conv_taps(xp_scr, qkv_ref[...], cw_ref, R))
    sm = sm_ref[...]
    g_s = _softplus(sm + dtb_s_ref[...]) * (-jnp.exp(alog_s_ref[...]))
    beta_s = jax.nn.sigmoid(sm)
    ri = _iota((R, R), 0)
    ci = _iota((R, R), 1)
    chunk_tril = (lax.shift_right_logical(ri, 6) == lax.shift_right_logical(ci, 6)) & (ri >= ci)
    gcum = _dot_exact_rhs(chunk_tril.astype(BF16), g_s)
    gcum_t = gcum.T

    def state_free_part(chunks):
        prep = []
        for ch in chunks:
            for h in range(H_C):
                prep.append(_gdn_head_prepare(act_scr, gcum, gcum_t, beta_s, ch * cs, h))
                tick()
        sols = _neumann_solve([pr[0] for pr in prep], [pr[1] for pr in prep], tick)
        return [_gdn_head_finish(sol, *pr[2:]) for sol, pr in zip(sols, prep)]

    def state_step(ch, heads):
        tick()
        outs = []
        for h, (lhs, c_add, o_add, e_last) in enumerate(heads):
            s_old = s_scr[h * DK_C:(h + 1) * DK_C, :]
            r = _dot(lhs, s_old.astype(BF16))
            s_scr[h * DK_C:(h + 1) * DK_C, :] = s_old * e_last + (c_add - r[0:DK_C, :])
            outs.append(_rms(r[DK_C:DK_C + cs, :] + o_add) * ng_ref[...])
        rows = slice(ch * cs, (ch + 1) * cs)
        o_ref[rows, :] = (jnp.concatenate(outs, axis=-1) * _silu(z_ref[rows, :])).astype(o_ref.dtype)

    half = max(n_chunks // 2, 1)
    for chunks in (range(0, half), range(half, n_chunks)):
        parts = state_free_part(chunks)
        for n, ch in enumerate(chunks):
            state_step(ch, parts[n * H_C:(n + 1) * H_C])
    rider_stages.finish()

    @pl.when(c == pl.num_programs(1) - 1)
    def _():
        sout_ref[...] = s_scr[...]


def _prompt_gdn(qkv, sm, z, p, R, decode=None):
    b, L, _ = qkv.shape
    nj = L // R
    full = lambda a: pl.BlockSpec(a.shape, lambda *_: (0,) * a.ndim)
    params = [p["gdn_conv_w"], p["dtb_small"], p["alog_small"], p["gdn_norm_g"]]
    in_specs = [pl.BlockSpec((None, R, CONV_C), lambda i, j, *_: (i, j, 0)),
                pl.BlockSpec((None, R, SMALL_W), lambda i, j, *_: (i, j, 0)),
                pl.BlockSpec((None, R, D_C), lambda i, j, *_: (i, j, (D_A + D_B) // D_C))] + [full(a) for a in params]
    out_specs = [pl.BlockSpec((None, R, D_C), lambda i, j, *_: (i, j, 0)),
                 pl.BlockSpec((None, H_C * DK_C, DV_C), lambda i, j, *_: (i, 0, 0))]
    out_shape = [jax.ShapeDtypeStruct((b, L, D_C), BF16), jax.ShapeDtypeStruct((b, H_C * DK_C, DV_C), F32)]
    scratch = [pltpu.VMEM((R + SUBLANES, CONV_C), F32), pltpu.VMEM((H_C * DK_C, DV_C), F32),
               pltpu.VMEM((R, CONV_C), F32)]
    args = [qkv, sm, z, *params]
    if decode is None:
        return pl.pallas_call(
            functools.partial(_gdn_kernel, R=R), grid=(b, nj), in_specs=in_specs, out_specs=out_specs,
            out_shape=out_shape, scratch_shapes=scratch, compiler_params=_cparams("arbitrary", "arbitrary"),
        )(*args)
    layer, page_table, lam_init, G, *decode_args = decode
    steps_table, d_in, d_args, d_out, d_shape, d_scratch, spb = _decode_operands(
        layer, G, lambda i, j: i * nj + j, page_table, *decode_args)
    return pl.pallas_call(
        functools.partial(_gdn_kernel, R=R, rider=(G, spb, layer, lam_init)),
        grid_spec=pltpu.PrefetchScalarGridSpec(
            num_scalar_prefetch=1, grid=(b, nj), in_specs=in_specs + d_in, out_specs=out_specs + [d_out],
            scratch_shapes=scratch + d_scratch),
        out_shape=out_shape + [d_shape],
        compiler_params=_cparams("arbitrary", "arbitrary"),
    )(steps_table, *args, *d_args)


def _expand_heads(vals, n_heads, width):
    lane = _iota((1, n_heads * width), 1)
    out = jnp.zeros((1, n_heads * width), F32)
    for h in range(n_heads):
        out = jnp.where((lane >= h * width) & (lane < (h + 1) * width), vals[:, h:h + 1], out)
    return out


def _pad_rows(rows):
    n = rows[0].shape[-1]
    r = _iota((SUBLANES, n), 0)
    out = jnp.zeros((SUBLANES, n), F32)
    for i, v in enumerate(rows):
        out = jnp.where(r == i, v, out)
    return out


def _sample_step_kernel(*refs, n_data, n_params, block_b):
    data, params, outs = refs[:n_data], refs[n_data:n_data + n_params], refs[n_data + n_params:]
    for bi in range(block_b):
        _sample_step_one(*[r.at[bi] for r in data], *params, *[r.at[bi] for r in outs])


def _sample_step_one(xbc_ref, qkv_ref, sm_ref, z_ref, cst_b_ref, cst_c_ref, h0_ref, s0_ref,
                     cwb_ref, cbb_ref, cwc_ref, dtb_s_ref, alog_s_ref, dskip_ref, ngb_ref, ngc_ref,
                     ob_ref, oc_ref, cnew_b_ref, cnew_c_ref, h1_ref, s1_ref):
    def conv(x, st_ref, w_ref, new_ref):
        y = w_ref[CONV_W - 1:CONV_W, :] * x
        for i in range(CONV_W - 1):
            y = y + w_ref[i:i + 1, :] * st_ref[i:i + 1, :]
        for i in range(CONV_W - 2):
            new_ref[i:i + 1, :] = st_ref[i + 1:i + 2, :]
        new_ref[CONV_W - 2:CONV_W - 1, :] = x
        return y

    sm = sm_ref[...]
    dt_or_sp = _softplus(sm + dtb_s_ref[...])
    a_s = dt_or_sp * (-jnp.exp(alog_s_ref[...]))
    z = z_ref[...]

    xa = _silu(conv(xbc_ref[...], cst_b_ref, cwb_ref, cnew_b_ref) + cbb_ref[...])
    xs = xa[:, 0:D_B]
    dt_e = _expand_heads(dt_or_sp[:, DT_LANE:DT_LANE + H_B], H_B, P_B)
    a_e = _expand_heads(a_s[:, DT_LANE:DT_LANE + H_B], H_B, P_B)
    xc = xs * dt_e
    half = D_B // G_B
    lane_b = _iota((1, D_B), 1)
    bm = [xa[:, D_B + g * N_B:D_B + (g + 1) * N_B] for g in range(G_B)]
    cm = [xa[:, D_B + G_B * N_B + g * N_B:D_B + G_B * N_B + (g + 1) * N_B] for g in range(G_B)]
    h0 = h0_ref[...]
    coff = _dot_nt(_pad_rows(cm).astype(BF16), h0.astype(BF16))
    y_off = jnp.where(lane_b < half, coff[0:1, :], coff[1:2, :]) * jnp.exp(a_e)
    cb = [jnp.sum(cm[g] * bm[g], axis=-1, keepdims=True) for g in range(G_B)]
    y_diag = jnp.where(lane_b < half, cb[0], cb[1]) * xc
    y = (y_diag + y_off + xs * dskip_ref[...]) * _silu(z[:, D_A:D_A + D_B])
    y = jnp.concatenate([_rms(y[:, g * half:(g + 1) * half]) for g in range(G_B)], axis=-1)
    ob_ref[...] = y * ngb_ref[...]
    xc_rows = _pad_rows([jnp.where(lane_b < half, xc, 0.0), jnp.where(lane_b >= half, xc, 0.0)])
    outer = _dot_tn(xc_rows.astype(BF16), _pad_rows(bm).astype(BF16))
    state_row = _iota((D_B, N_B), 0)
    dec = jnp.exp(a_s)
    dec_col = jnp.zeros((D_B, N_B), F32)
    for h in range(H_B):
        in_head = (state_row >= h * P_B) & (state_row < (h + 1) * P_B)
        dec_col = jnp.where(in_head, dec[:, DT_LANE + h:DT_LANE + h + 1], dec_col)
    h1_ref[...] = h0 * dec_col + outer

    qkv = _silu(conv(qkv_ref[...], cst_c_ref, cwc_ref, cnew_c_ref))
    beta_s = jax.nn.sigmoid(sm)
    outs = []
    for h in range(H_C):
        q = qkv[:, h * DK_C:(h + 1) * DK_C]
        k = qkv[:, D_C + h * DK_C:D_C + (h + 1) * DK_C]
        v = qkv[:, 2 * D_C + h * DV_C:2 * D_C + (h + 1) * DV_C]
        q = q * lax.rsqrt(jnp.sum(q * q, axis=-1, keepdims=True) + EPS) * (DK_C ** -0.5)
        k = k * lax.rsqrt(jnp.sum(k * k, axis=-1, keepdims=True) + EPS)
        beta = beta_s[:, BETA_LANE + h:BETA_LANE + h + 1]
        eg = jnp.exp(a_s[:, AG_LANE + h:AG_LANE + h + 1])
        s_old = s0_ref[h * DK_C:(h + 1) * DK_C, :]
        ws = _dot(_pad_rows([k * (beta * eg), q * eg]).astype(BF16), s_old.astype(BF16))
        v_new = v * beta - ws[0:1, :]
        o = ws[1:2, :] + jnp.sum(q * k, axis=-1, keepdims=True) * v_new
        upd = _dot_tn(_pad_rows([k]).astype(BF16), _pad_rows([v_new]).astype(BF16))
        s1_ref[h * DK_C:(h + 1) * DK_C, :] = s_old * eg + upd
        outs.append(_rms(o) * ngc_ref[...])
    oc_ref[...] = jnp.concatenate(outs, axis=-1) * _silu(z[:, D_A + D_B:D_MIX])


def _sample_step(layer, xbc, qkv, sm, z, cst_b, cst_c, h0, s0, p):
    bs = xbc.shape[0]
    block_b = _pick(bs, (4, 2, 1))
    per_b = lambda a: pl.BlockSpec((block_b,) + tuple(a.shape[1:]), lambda i: (i,) + (0,) * (len(a.shape) - 1))
    of_layer = lambda a: pl.BlockSpec((None, block_b) + tuple(a.shape[2:]),
                                      lambda i: (layer, i) + (0,) * (len(a.shape) - 2))
    full = lambda a: pl.BlockSpec(a.shape, lambda i: (0,) * a.ndim)
    rows, states = [xbc, qkv, sm, z], [cst_b, cst_c, h0, s0]
    data = rows + states
    params = [p["ssm_conv_w"], p["ssm_conv_b"], p["gdn_conv_w"], p["dtb_small"], p["alog_small"],
              p["ssm_d_e"], p["ssm_norm_g"], p["gdn_norm_g"]]
    outs = [jax.ShapeDtypeStruct((bs, 1, D_B), F32), jax.ShapeDtypeStruct((bs, 1, D_C), F32)] + [
        jax.ShapeDtypeStruct(a.shape[1:], F32) for a in states]
    return pl.pallas_call(
        functools.partial(_sample_step_kernel, n_data=len(data), n_params=len(params), block_b=block_b),
        grid=(bs // block_b,),
        in_specs=[per_b(a) for a in rows] + [of_layer(a) for a in states] + [full(a) for a in params],
        out_specs=[per_b(a) for a in outs],
        out_shape=outs,
        compiler_params=_cparams("arbitrary"),
    )(*data, *params)


def _split_w_in(w):
    sizes = (2 * H_A * DH_A, 2 * H_A * DH_A, D_A, CONV_B, H_B, CONV_C, H_C, H_C, D_MIX)
    offs = [0]
    for s in sizes:
        offs.append(offs[-1] + s)
    col = lambda i, j=None: w[:, offs[i]:offs[(i if j is None else j) + 1]].astype(BF16)
    small = jnp.concatenate([col(4), col(6), col(7),
                             jnp.zeros((w.shape[0], SMALL_W - H_B - 2 * H_C), BF16)], axis=1)
    return [col(0, 3), col(5), col(8), small]


def _layer_params(l, w):
    pad_small = lambda a_ssm, a_gdn: jnp.concatenate(
        [a_ssm, jnp.zeros((H_C,), F32), a_gdn, jnp.zeros((SMALL_W - H_B - 2 * H_C,), F32)]).reshape(1, SMALL_W)
    rep = lambda a: jnp.repeat(a, P_B).reshape(1, D_B)
    expand = (jnp.arange(SMALL_W)[:, None] == (jnp.arange(D_B)[None, :] // P_B)).astype(BF16)
    return {
        "ssm_conv_w": w["ssm_conv_w"][l], "ssm_conv_b": w["ssm_conv_b"][l].reshape(1, CONV_B),
        "gdn_conv_w": w["gdn_conv_w"][l],
        "dtb_small": pad_small(w["ssm_dt_bias"][l], w["gdn_dt_bias"][l]),
        "alog_small": pad_small(w["ssm_A_log"][l], w["gdn_A_log"][l]),
        "ssm_d_e": rep(w["ssm_D"][l]),
        "ssm_norm_g": w["ssm_norm_g"][l].reshape(1, D_B), "gdn_norm_g": w["gdn_norm_g"][l].reshape(1, DV_C),
        "expand": expand,
    }


def _pick(n, prefs):
    for t in prefs:
        if n % t == 0:
            return t
    return n


def kernel(x_prompt, x_sample, c_prompt, c_sample, cache_k, cache_v, state_ssm, state_ssm_conv, state_gdn, state_gdn_conv, page_table, w_ada, b_ada, w_in, w_out, rel_bias, attn_lambda, attn_subln_g, ssm_conv_w, ssm_conv_b, ssm_dt_bias, ssm_A_log, ssm_D, ssm_norm_g, gdn_conv_w, gdn_dt_bias, gdn_A_log, gdn_norm_g, final_norm_g):
    depth = w_in.shape[0]
    bp, L, _ = x_prompt.shape
    bs = x_sample.shape[0]
    n_pages = page_table.shape[1]
    past = n_pages * PAGE_SIZE
    n_pool = cache_k.shape[1]
    wts = dict(ssm_conv_w=ssm_conv_w, ssm_conv_b=ssm_conv_b, ssm_dt_bias=ssm_dt_bias, ssm_A_log=ssm_A_log,
               ssm_D=ssm_D, ssm_norm_g=ssm_norm_g, gdn_conv_w=gdn_conv_w, gdn_dt_bias=gdn_dt_bias,
               gdn_A_log=gdn_A_log, gdn_norm_g=gdn_norm_g)

    tm = _pick(L, (512, 256, 128, 64))
    tm_out = _pick(L, (512, 256, 128, 64))
    T = _pick(L, (256, 128))
    Q = _pick(L, (256, 128, 64))
    R = _pick(L, (256, 128, 64))
    G = _pick(n_pages, (32, 16, 8, 4, 2, 1))
    gdn_steps = bp * (L // R)
    G_ride = (bs * n_pages) // gdn_steps if (bs * n_pages) % gdn_steps == 0 else 0
    if not (1 <= G_ride <= MAX_PAGES_PER_STEP and n_pages % G_ride == 0):
        G_ride = 0

    mod = _modulation(jnp.concatenate([c_prompt, c_sample], axis=0), w_ada, b_ada)
    bias_near = _near_bias_tiles(rel_bias, T)
    bias_pages = _decode_bias(rel_bias, past)
    bias_new = jnp.broadcast_to(jnp.tile(rel_bias[0].astype(F32), 2)[:, None], (2 * H_A, DV_A))
    ck = cache_k.reshape(depth, n_pool, PAGE_SIZE * H_A, 2 * DH_A)
    cv = cache_v.reshape(depth, n_pool, PAGE_SIZE * H_A, DV_A)
    lane_half = jnp.arange(DV_A) < DH_A

    xp = x_prompt
    xs = x_sample.reshape(1, bs, D_MODEL)
    new_p, new_s = [], []
    kv_rows = None
    for l in range(depth):
        lam_init = 0.8 - 0.6 * math.exp(-0.3 * l)
        p = _layer_params(l, wts)
        w_perm = _split_w_in(w_in[l])
        w_o = w_out[l].astype(BF16)
        shift, scale, gate = jnp.split(mod[l], 3, axis=-1)
        mp = lambda a: a[:bp].reshape(bp, 1, D_MODEL)
        ms = lambda a: a[bp:].reshape(1, bs, D_MODEL)

        final = final_norm_g if l == depth - 1 else None
        q, k, v, xbc, qkv, z, sm, *kv_rows = _in_projection(xp, mp(scale), mp(shift), w_perm, tm,
                                                            (l, depth, kv_rows))
        q_s, k_s, v_s, xbc_s, qkv_s, z_s, sm_s = _in_projection(xs, ms(scale), ms(shift), w_perm, bs)
        q4 = q_s.reshape(bs, H_A, DV_A) * (DH_A ** -0.5)
        qpat = jnp.concatenate([jnp.where(lane_half, q4, 0.0), jnp.where(lane_half, 0.0, q4)], axis=1)
        k4 = k_s.reshape(bs, H_A, DV_A)
        v4 = v_s.reshape(bs, H_A, DV_A)
        decode_args = (ck, cv, qpat, jnp.concatenate([k4, k4], axis=1), jnp.concatenate([v4, v4], axis=1),
                       bias_pages, bias_new, z_s[..., :D_A].reshape(bs, H_A, DV_A), attn_lambda[l],
                       attn_subln_g[l])

        o_a = _prompt_attention(q, k, v, z, bias_near, attn_lambda[l], attn_subln_g[l], lam_init, T)
        o_b, ssm_h = _prompt_ssd(xbc, sm, z, p, Q)
        if G_ride:
            o_c, gdn_s, o_a_s = _prompt_gdn(qkv, sm, z, p, R, (l, page_table, lam_init, G_ride, *decode_args))
        else:
            o_c, gdn_s = _prompt_gdn(qkv, sm, z, p, R)
            o_a_s = _decode_attention(l, page_table, lam_init, G, *decode_args)
        xp = _out_projection(xp, mp(gate), o_a, o_b, o_c, w_o, final, tm_out)
        new_p.append((ssm_h.reshape(bp, H_B, P_B, N_B), xbc[:, L - (CONV_W - 1):, :],
                      gdn_s.reshape(bp, H_C, DK_C, DV_C), qkv[:, L - (CONV_W - 1):, :]))

        row = lambda a: a.reshape(bs, 1, a.shape[-1])
        o_b_s, o_c_s, conv_b1, conv_c1, ssm_h_s, gdn_s_s = _sample_step(
            l, row(xbc_s), row(qkv_s), row(sm_s), row(z_s), state_ssm_conv, state_gdn_conv,
            state_ssm.reshape(depth, bs, D_B, N_B), state_gdn.reshape(depth, bs, H_C * DK_C, DV_C), p)
        xs = _out_projection(xs, ms(gate), o_a_s.reshape(1, bs, D_A).astype(BF16),
                             o_b_s.reshape(1, bs, D_B).astype(BF16), o_c_s.reshape(1, bs, D_C).astype(BF16),
                             w_o, final, bs)
        new_s.append((k_s.reshape(bs, 1, H_A, 2 * DH_A), v_s.reshape(bs, 1, H_A, DV_A),
                      ssm_h_s.reshape(bs, H_B, P_B, N_B), conv_b1,
                      gdn_s_s.reshape(bs, H_C, DK_C, DV_C), conv_c1))

    k_p, v_p = kv_rows
    ssm_p, ssm_conv_p, gdn_p, gdn_conv_p = [jnp.stack(a) for a in zip(*new_p)]
    k_s, v_s, ssm_s, ssm_conv_s, gdn_s, gdn_conv_s = [jnp.stack(a) for a in zip(*new_s)]
    return (xp, xs.reshape(bs, 1, D_MODEL), k_p, v_p, ssm_p, ssm_conv_p, gdn_p, gdn_conv_p,
            k_s, v_s, ssm_s, ssm_conv_s, gdn_s, gdn_conv_s)
```

```python
import functools
import math

import jax
import jax.numpy as jnp
from jax import lax
from jax.experimental import pallas as pl
from jax.experimental.pallas import tpu as pltpu

F32 = jnp.float32
BF16 = jnp.bfloat16

D_MODEL = 1024
H_A, DH_A, DV_A = 4, 64, 128
D_A = H_A * DV_A
H_B, P_B, G_B, N_B = 8, 64, 2, 128
D_B = H_B * P_B
CONV_B = D_B + 2 * G_B * N_B
H_C, DK_C, DV_C = 4, 128, 128
D_C = H_C * DV_C
CONV_C = 2 * H_C * DK_C + D_C
CONV_W = 4
D_MIX = D_A + D_B + D_C
REL_BUCKETS, REL_MAX_DIST = 32, 128
PAGE_SIZE = 128
EPS = 1e-6
GDN_CHUNK = 64

LANES = 128
SUBLANES = 8
SMALL_W = LANES
DT_LANE, BETA_LANE, AG_LANE = 0, H_B, H_B + H_C
NEG_BIG = -1e30
LOG2E = math.log2(math.e)
ATTN_Q_SCALE = DH_A ** -0.5 * LOG2E
ATTN_STRIP = 64
MAX_PAGES_PER_STEP = 32
V7X_VMEM_BYTES = 64 * 1024 * 1024
VMEM_LIMIT = V7X_VMEM_BYTES * 7 // 8

SEGMENTS = (("q", 2 * H_A * DH_A), ("k", 2 * H_A * DH_A), ("v", D_A), ("xbc", CONV_B),
            ("qkv", CONV_C), ("z", D_MIX), ("small", SMALL_W))
SEGMENT_GROUPS = (("q", "k", "v", "xbc"), ("qkv",), ("z",), ("small",))


def _cparams(*sem):
    return pltpu.CompilerParams(dimension_semantics=sem, vmem_limit_bytes=VMEM_LIMIT)


def _silu(x):
    return x * jax.nn.sigmoid(x)


def _softplus(x):
    return jnp.maximum(x, 0.0) + jnp.log1p(jnp.exp(-jnp.abs(x)))


def _rms(x):
    return x * lax.rsqrt(jnp.mean(x * x, axis=-1, keepdims=True) + EPS)


def _dot_nt(a, b):
    return lax.dot_general(a, b, (((1,), (1,)), ((), ())), preferred_element_type=F32)


def _dot_tn(a, b):
    return lax.dot_general(a, b, (((0,), (0,)), ((), ())), preferred_element_type=F32)


def _dot(a, b):
    return jnp.dot(a, b, preferred_element_type=F32)


def _iota(shape, dim):
    return lax.broadcasted_iota(jnp.int32, shape, dim)


def _mod_kernel(c_ref, w_ref, b_ref, o_ref):
    sc = _silu(c_ref[...]).astype(BF16)
    o_ref[...] = _dot(sc, w_ref[...].astype(BF16)) + b_ref[...]


def _modulation(c_all, w_ada, b_ada):
    depth = w_ada.shape[0]
    n = c_all.shape[0]
    return pl.pallas_call(
        _mod_kernel,
        grid=(depth, 3),
        in_specs=[pl.BlockSpec((n, D_MODEL), lambda l, j: (0, 0)),
                  pl.BlockSpec((None, D_MODEL, D_MODEL), lambda l, j: (l, 0, j)),
                  pl.BlockSpec((None, 1, D_MODEL), lambda l, j: (l, 0, j))],
        out_specs=pl.BlockSpec((None, n, D_MODEL), lambda l, j: (l, 0, j)),
        out_shape=jax.ShapeDtypeStruct((depth, n, 3 * D_MODEL), F32),
        compiler_params=_cparams("arbitrary", "arbitrary"),
    )(c_all, w_ada, b_ada.reshape(depth, 1, 3 * D_MODEL))


def _inproj_kernel(x_ref, scale_ref, shift_ref, *rest, heads_out, n_carried):
    n_groups = len(SEGMENT_GROUPS)
    w_refs = rest[:n_groups]
    out_refs = rest[n_groups + n_carried:]
    h = _rms(x_ref[...]) * (1.0 + scale_ref[...]) + shift_ref[...]
    hb = h.astype(BF16)
    seg_refs = dict(zip((name for name, _ in SEGMENTS), out_refs))
    head_refs = dict(zip(("k", "v"), out_refs[len(SEGMENTS):])) if heads_out else {}
    widths = dict(SEGMENTS)
    for w_ref, names in zip(w_refs, SEGMENT_GROUPS):
        off = 0
        for name in names:
            n = widths[name]
            res = _dot(hb, w_ref[:, off:off + n])
            scaled = res * ATTN_Q_SCALE if heads_out and name == "q" else res
            seg_refs[name][...] = scaled.astype(seg_refs[name].dtype)
            if name in head_refs:
                ref = head_refs[name]
                if len(ref.shape) == 4:
                    ref[1:] = jnp.zeros((ref.shape[0] - 1,) + tuple(ref.shape[1:]), ref.dtype)
                    ref = ref.at[0]
                for hd in range(H_A):
                    ref[:, hd, :] = res[:, hd * DV_A:(hd + 1) * DV_A]
            off += n


def _in_projection(x, scale, shift, w_groups, tm, cache_rows=None):
    b, L, _ = x.shape
    ms = scale.shape[1]
    mod_rows = tm if ms == L else 1
    mod_map = (lambda i, j: (i, j, 0)) if ms == L else (lambda i, j: (i, 0, 0))
    heads_out = cache_rows is not None
    dt = lambda name: BF16 if heads_out and name in ("q", "k", "v") else F32
    out_specs = [pl.BlockSpec((None, tm, w), lambda i, j: (i, j, 0)) for _, w in SEGMENTS]
    out_shape = [jax.ShapeDtypeStruct((b, L, w), dt(name)) for name, w in SEGMENTS]
    carried, aliases = (), {}
    if heads_out:
        layer, depth, prev = cache_rows
        if prev is None:
            assert layer == 0
            out_specs += [pl.BlockSpec((depth, None, tm, H_A, DV_A), lambda i, j: (0, i, j, 0, 0))] * 2
        else:
            out_specs += [pl.BlockSpec((None, None, tm, H_A, DV_A), lambda i, j: (layer, i, j, 0, 0))] * 2
        out_shape += [jax.ShapeDtypeStruct((depth, b, L, H_A, DV_A), F32)] * 2
        if prev is not None:
            carried = tuple(prev)
            aliases = {3 + len(w_groups) + n: len(SEGMENTS) + n for n in range(len(carried))}
    return pl.pallas_call(
        functools.partial(_inproj_kernel, heads_out=heads_out, n_carried=len(carried)),
        grid=(b, L // tm),
        in_specs=[pl.BlockSpec((None, tm, D_MODEL), lambda i, j: (i, j, 0)),
                  pl.BlockSpec((None, mod_rows, D_MODEL), mod_map),
                  pl.BlockSpec((None, mod_rows, D_MODEL), mod_map)]
                 + [pl.BlockSpec(w.shape, lambda i, j: (0, 0), pipeline_mode=pl.Buffered(1)) for w in w_groups]
                 + [pl.BlockSpec(memory_space=pl.ANY)] * len(carried),
        out_specs=out_specs,
        out_shape=out_shape,
        input_output_aliases=aliases,
        compiler_params=_cparams("arbitrary", "arbitrary"),
    )(x, scale, shift, *w_groups, *carried)


def _outproj_kernel(x_ref, gate_ref, oa_ref, ob_ref, oc_ref, w_ref, *rest, final):
    acc = _dot(oa_ref[...], w_ref[0:D_A, :])
    acc += _dot(ob_ref[...], w_ref[D_A:D_A + D_B, :])
    acc += _dot(oc_ref[...], w_ref[D_A + D_B:D_MIX, :])
    y = x_ref[...] + gate_ref[...] * acc
    if final:
        g_ref, o_ref = rest
        o_ref[...] = _rms(y) * g_ref[...]
    else:
        (o_ref,) = rest
        o_ref[...] = y


def _out_projection(x, gate, oa, ob, oc, w_out, final_g, tm):
    b, L, _ = x.shape
    ms = gate.shape[1]
    mod_rows = tm if ms == L else 1
    mod_map = (lambda i, j: (i, j, 0)) if ms == L else (lambda i, j: (i, 0, 0))
    row = lambda w: pl.BlockSpec((None, tm, w), lambda i, j: (i, j, 0))
    in_specs = [row(D_MODEL), pl.BlockSpec((None, mod_rows, D_MODEL), mod_map),
                row(D_A), row(D_B), row(D_C),
                pl.BlockSpec((D_MIX, D_MODEL), lambda i, j: (0, 0))]
    args = [x, gate, oa, ob, oc, w_out]
    final = final_g is not None
    if final:
        in_specs.append(pl.BlockSpec((1, D_MODEL), lambda i, j: (0, 0)))
        args.append(final_g.reshape(1, D_MODEL))
    return pl.pallas_call(
        functools.partial(_outproj_kernel, final=final),
        grid=(b, L // tm),
        in_specs=in_specs,
        out_specs=row(D_MODEL),
        out_shape=jax.ShapeDtypeStruct((b, L, D_MODEL), F32),
        compiler_params=_cparams("arbitrary", "arbitrary"),
    )(*args)


def _lambda_value(lp, lam_init):
    s01 = jnp.sum(lp[0:1, :] * lp[1:2, :], axis=-1, keepdims=True)
    s23 = jnp.sum(lp[2:3, :] * lp[3:4, :], axis=-1, keepdims=True)
    return jnp.exp(s01) - jnp.exp(s23) + lam_init


def _rel_bias_values(table, n):
    exact = REL_BUCKETS // 2
    large = exact + (jnp.log(jnp.maximum(n, 1).astype(F32) / exact)
                     / math.log(REL_MAX_DIST / exact) * (REL_BUCKETS - exact)).astype(jnp.int32)
    bucket = jnp.where(n < exact, n, jnp.minimum(large, REL_BUCKETS - 1))
    onehot = bucket[..., None, None] == jnp.arange(REL_BUCKETS)[:, None]
    return jnp.sum(jnp.where(onehot, table.astype(F32), 0.0), axis=-2)


def _attn_kernel(q_ref, kb_scr, vb_scr, z_ref, bias_ref, lamp_ref, subln_ref, o_ref, s_bufs, w_bufs,
                 vaug_scr, *, T, lam_init):
    L = q_ref.shape[0]
    nq = L // T
    lam = _lambda_value(lamp_ref[...], lam_init)
    lane = _iota((T, DV_A), 1)
    half = LANES

    def logit_tasks(i):
        q = q_ref[i * T:(i + 1) * T, :]
        s_scr = s_bufs.at[i % 2]
        tasks = []
        for sub in range(2):
            qm = jnp.where((lane < DH_A) if sub == 0 else (lane >= DH_A), q, jnp.zeros_like(q))
            for j in range(i + 1):
                def task(qm=qm, sub=sub, j=j):
                    s = _dot_nt(qm, kb_scr[j * T:(j + 1) * T, :])
                    if j == i:
                        s = s + bias_ref[:, T:2 * T]
                    elif j == i - 1:
                        s = s + bias_ref[:, 0:T]
                    s_scr[sub, j] = s
                tasks.append(task)
        return tasks

    def softmax_strip(i, r0):
        s_scr, w_scr = s_bufs.at[i % 2], w_bufs.at[i % 2]
        strip = slice(r0, r0 + ATTN_STRIP)
        for sub in range(2):
            mx = None
            for j in range(i + 1):
                s = s_scr[sub, j, strip, :]
                for c0 in range(0, T, half):
                    t = s[:, c0:c0 + half]
                    mx = t if mx is None else jnp.maximum(mx, t)
            m_row = jnp.max(mx, axis=-1, keepdims=True)
            for j in range(i + 1):
                w_scr[sub, strip, j * T:(j + 1) * T] = jnp.exp2(s_scr[sub, j, strip, :] - m_row).astype(BF16)

    vaug_scr[:, 0:DV_A] = vb_scr[...]
    vaug_scr[:, DV_A:2 * DV_A] = (_iota((L, DV_A), 1) == 0).astype(BF16)

    for task in logit_tasks(0):
        task()
    for i in range(nq):
        nxt = logit_tasks(i + 1) if i + 1 < nq else []
        strips = list(range(0, T, ATTN_STRIP))
        per = -(-len(nxt) // len(strips))
        for n, r0 in enumerate(strips):
            softmax_strip(i, r0)
            for task in nxt[n * per:(n + 1) * per]:
                task()
        rows = slice(i * T, (i + 1) * T)
        nk = (i + 1) * T
        outs = []
        for sub in range(2):
            acc = _dot(w_bufs[i % 2, sub, :, 0:nk], vaug_scr[0:nk, :])
            outs.append(acc[:, 0:DV_A] / acc[:, DV_A:DV_A + 1])
        o = outs[0] - lam * outs[1]
        o = _rms(o) * subln_ref[...] * (1.0 - lam_init) * _silu(z_ref[rows, :])
        o_ref[rows, :] = o.astype(o_ref.dtype)


def _prompt_attention(q, k, v, z, bias_near, lam_p, subln, lam_init, T):
    b, L, _ = q.shape
    seq = lambda: pl.BlockSpec((None, L, DV_A), lambda i, h: (i, 0, h))
    return pl.pallas_call(
        functools.partial(_attn_kernel, T=T, lam_init=lam_init),
        grid=(b, H_A),
        in_specs=[seq(), seq(), seq(), seq(),
                  pl.BlockSpec((None, T, 2 * T), lambda i, h: (h, 0, 0)),
                  pl.BlockSpec((4, DH_A), lambda i, h: (0, 0)),
                  pl.BlockSpec((1, DV_A), lambda i, h: (0, 0))],
        out_specs=seq(),
        out_shape=jax.ShapeDtypeStruct((b, L, D_A), BF16),
        scratch_shapes=[pltpu.VMEM((2, 2, L // T, T, T), F32), pltpu.VMEM((2, 2, T, L), BF16),
                        pltpu.VMEM((L, 2 * DV_A), BF16)],
        compiler_params=_cparams("arbitrary", "arbitrary"),
    )(q, k, v, z, bias_near, lam_p, subln.reshape(1, DV_A))


def _near_bias_tiles(table, T):
    period = 3 * T
    k = jnp.arange(period)
    n = jnp.where(k < 2 * T, T - k, T + period - k)
    vals = _rel_bias_values(table, jnp.maximum(n, 0)) - table[REL_BUCKETS - 1].astype(F32)
    w = jnp.where((n >= 0)[:, None], vals * LOG2E, NEG_BIG).T
    skew = jnp.tile(w, (1, T + 1))[:, :T * (period - 1)].reshape(H_A, T, period - 1)
    return skew[:, :, :2 * T]


N_DECODE_IN = 10


def _decode_stages(pt_ref, in_refs, o_ref, scratch, step, n_steps, first, G, layer, lam_init):
    (qpat_ref, knew_ref, vnew_ref, bias_ref, biasnew_ref, z_ref, lamp_ref, subln_ref,
     cache_k_ref, cache_v_ref) = in_refs
    m_scr, l_scr, acc_scr, kbuf, vbuf, sem = scratch
    slot = lax.rem(step, 2)

    def page_copy(cache_ref, buf, which, page, to_slot, g):
        return pltpu.make_async_copy(cache_ref.at[layer, page], buf.at[to_slot, g], sem.at[to_slot, g, which])

    def start_step(s, to_slot):
        for g in range(G):
            page = pt_ref[s, g]
            page_copy(cache_k_ref, kbuf, 0, page, to_slot, g).start()
            page_copy(cache_v_ref, vbuf, 1, page, to_slot, g).start()

    @pl.when(step == 0)
    def _():
        start_step(0, 0)

    @pl.when(step + 1 < n_steps)
    def _():
        start_step(step + 1, 1 - slot)

    @pl.when(first)
    def _():
        m_scr[...] = jnp.full(m_scr.shape, NEG_BIG, F32)
        l_scr[...] = jnp.zeros(l_scr.shape, F32)
        acc_scr[...] = jnp.zeros(acc_scr.shape, F32)

    for g in range(G):
        page = pt_ref[step, g]
        page_copy(cache_k_ref, kbuf, 0, page, slot, g).wait()
        page_copy(cache_v_ref, vbuf, 1, page, slot, g).wait()

    qf = qpat_ref[...]
    qb = qf.astype(BF16)
    st = {"s": [None] * G, "pv": None}

    def logits(g):
        st["s"][g] = _dot_nt(qb, kbuf[slot, g].astype(BF16)) + bias_ref[g]

    def softmax_stats():
        s = st["s"]
        m_old, l_old = m_scr[...], l_scr[...]
        m_tile = s[0]
        for g in range(1, G):
            m_tile = jnp.maximum(m_tile, s[g])
        m = jnp.maximum(m_old, jnp.max(m_tile, axis=-1, keepdims=True))
        st["alpha"] = jnp.exp(m_old - m)
        p = [jnp.exp(s[g] - m) for g in range(G)]
        p_sum = p[0]
        for g in range(1, G):
            p_sum = p_sum + p[g]
        st["p"] = [x.astype(BF16) for x in p]
        st["m"] = m
        st["l"] = st["alpha"] * l_old + jnp.sum(p_sum, axis=-1, keepdims=True)

    def values(g):
        d =_dot(st["p"][g], vbuf[slot, g].astype(BF16))
        st["pv"] = d if st["pv"] is None else st["pv"] + d

    def finish():
        m, l = st["m"], st["l"]
        acc = st["alpha"] * acc_scr[...] + st["pv"]
        m_scr[...] = m
        l_scr[...] = l
        acc_scr[...] = acc
        s_new = jnp.sum(qf * knew_ref[...], axis=-1, keepdims=True) + biasnew_ref[:, 0:1]
        m_new = jnp.maximum(m, s_new)
        alpha = jnp.exp(m - m_new)
        p_new = jnp.exp(s_new - m_new)
        l_fin = alpha * l + p_new
        out = (alpha * acc + p_new * vnew_ref[...]) / l_fin
        lam = _lambda_value(lamp_ref[...], lam_init)
        o = out[0:H_A, :] - lam * out[H_A:2 * H_A, :]
        o_ref[...] = _rms(o) * subln_ref[...] * (1.0 - lam_init) * _silu(z_ref[...])

    return ([functools.partial(logits, g) for g in range(G)] + [softmax_stats]
            + [functools.partial(values, g) for g in range(G)] + [finish])


class _Interleaver:
    def __init__(self, thunks, n_ticks):
        self.thunks, self.n_ticks, self.ticks, self.done = thunks, n_ticks, 0, 0

    def tick(self):
        self.ticks += 1
        target = min(len(self.thunks), -(-len(self.thunks) * self.ticks // self.n_ticks))
        while self.done < target:
            self.thunks[self.done]()
            self.done += 1

    def finish(self):
        self.ticks = self.n_ticks - 1
        self.tick()


def _decode_operands(layer, G, flat_step, page_table, cache_k, cache_v, qpat, knew, vnew, bias_pages,
                     bias_new, z3, lam_p, subln):
    bs, n_pages = page_table.shape
    spb = n_pages // G
    rows = PAGE_SIZE * H_A
    if spb & (spb - 1) == 0:
        seq = lambda ids: lax.shift_right_logical(flat_step(*ids), spb.bit_length() - 1)
        grp = lambda ids: flat_step(*ids) & (spb - 1)
    else:
        seq = lambda ids: flat_step(*ids) // spb
        grp = lambda ids: flat_step(*ids) % spb

    per_b = lambda r: pl.BlockSpec((None, r, DV_A), lambda *a: (seq(a[:-1]), 0, 0))
    const = lambda shape: pl.BlockSpec(shape, lambda *a: (0,) * len(shape))
    in_specs = [per_b(2 * H_A), per_b(2 * H_A), per_b(2 * H_A),
                pl.BlockSpec((G, 2 * H_A, rows), lambda *a: (grp(a[:-1]), 0, 0)),
                const((2 * H_A, DV_A)), per_b(H_A), const((4, DH_A)), const((1, DV_A)),
                pl.BlockSpec(memory_space=pl.ANY), pl.BlockSpec(memory_space=pl.ANY)]
    args = [qpat, knew, vnew, bias_pages, bias_new, z3, lam_p, subln.reshape(1, DV_A), cache_k, cache_v]
    scratch = [pltpu.VMEM((2 * H_A, 1), F32), pltpu.VMEM((2 * H_A, 1), F32), pltpu.VMEM((2 * H_A, DV_A), F32),
               pltpu.VMEM((2, G, rows, DV_A), F32), pltpu.VMEM((2, G, rows, DV_A), F32),
               pltpu.SemaphoreType.DMA((2, G, 2))]
    steps_table = page_table.reshape(bs * spb, G)
    return steps_table, in_specs, args, per_b(H_A), jax.ShapeDtypeStruct((bs, H_A, DV_A), F32), scratch, spb


def _decode_kernel(pt_ref, *refs, G, layer, lam_init):
    step = pl.program_id(0) * pl.num_programs(1) + pl.program_id(1)
    n_steps = pl.num_programs(0) * pl.num_programs(1)
    for stage in _decode_stages(pt_ref, refs[:N_DECODE_IN], refs[N_DECODE_IN], refs[N_DECODE_IN + 1:], step,
                                n_steps, pl.program_id(1) == 0, G, layer, lam_init):
        stage()


def _decode_attention(layer, page_table, lam_init, G, *decode_args):
    bs, n_pages = page_table.shape
    steps_table, in_specs, args, out_spec, out_shape, scratch, spb = _decode_operands(
        layer, G, lambda i, j: i * (n_pages // G) + j, page_table, *decode_args)
    return pl.pallas_call(
        functools.partial(_decode_kernel, G=G, layer=layer, lam_init=lam_init),
        grid_spec=pltpu.PrefetchScalarGridSpec(num_scalar_prefetch=1, grid=(bs, spb), in_specs=in_specs,
                                               out_specs=out_spec, scratch_shapes=scratch),
        out_shape=out_shape,
        compiler_params=_cparams("arbitrary", "arbitrary"),
    )(steps_table, *args)


def _decode_bias(table, past):
    n = past - jnp.arange(past)
    vals = _rel_bias_values(table, n).reshape(past // PAGE_SIZE, PAGE_SIZE, H_A)
    vals = jnp.moveaxis(vals, -1, 1)[..., None]
    same = (jnp.arange(H_A)[:, None, None] == jnp.arange(H_A)[None, None, :])
    full = jnp.where(same[None], vals, NEG_BIG).reshape(past // PAGE_SIZE, H_A, PAGE_SIZE * H_A)
    return jnp.concatenate([full, full], axis=1)


def _conv_taps(xp_ref, x, w_ref, rows):
    xp_ref[SUBLANES:SUBLANES + rows, :] = x
    xe = xp_ref[...]
    t = w_ref[0:1, :] * xe
    for i in range(1, CONV_W):
        t = w_ref[i:i + 1, :] * xe + pltpu.roll(t, 1, axis=0)
    xp_ref[0:SUBLANES, :] = x[rows - SUBLANES:rows, :]
    return t[SUBLANES:SUBLANES + rows, :]


def _ssd_kernel(xbc_ref, sm_ref, z_ref, cw_ref, cb_ref, dtb_s_ref, alog_s_ref,
                dskip_ref, ng_ref, exp_ref, y_ref, hout_ref, xp_scr, h_scr, *, Q):
    c = pl.program_id(1)

    @pl.when(c == 0)
    def _():
        xp_scr[0:SUBLANES, :] = jnp.zeros((SUBLANES, CONV_B), F32)
        h_scr[...] = jnp.zeros(h_scr.shape, F32)

    xa = _silu(_conv_taps(xp_scr, xbc_ref[...], cw_ref, Q) + cb_ref[...])
    xs = xa[:, 0:D_B]
    sm = sm_ref[...]
    row = _iota((Q, Q), 0)
    col = _iota((Q, Q), 1)
    tril = row >= col
    expand = exp_ref[...]

    dt_s = _softplus(sm + dtb_s_ref[...])
    a_s = dt_s * (-jnp.exp(alog_s_ref[...]))
    cum_s = _dot_exact_rhs(tril.astype(BF16), a_s)
    cum_t = cum_s.T
    dt_e = _dot_exact_lhs(dt_s, expand)
    cum_e = _dot_exact_lhs(cum_s, expand)
    xc = xs * dt_e
    dec_end = jnp.exp(cum_e[Q - 1:Q, :] - cum_e)
    dec_in = jnp.exp(cum_e)
    xc_b = xc.astype(BF16)
    xe_b = (xc * dec_end).astype(BF16)
    lane = _iota((Q, LANES), 1)
    prow = _iota((LANES, LANES), 0)
    heads_per_group = H_B // G_B
    pairs = []
    for g in range(G_B):
        bm = xa[:, D_B + g * N_B:D_B + (g + 1) * N_B].astype(BF16)
        cm = xa[:, D_B + G_B * N_B + g * N_B:D_B + G_B * N_B + (g + 1) * N_B].astype(BF16)
        cb = _dot_nt(cm, bm)
        for pr in range(g * heads_per_group // 2, (g + 1) * heads_per_group // 2):
            lo, hi = pr * LANES, (pr + 1) * LANES
            yd = []
            for hh in (2 * pr, 2 * pr + 1):
                seg = cum_s[:, hh:hh + 1] - cum_t[hh:hh + 1, :]
                mat = cb * jnp.exp(jnp.where(tril, seg, -jnp.inf))
                yd.append(_dot(mat.astype(BF16), xc_b[:, lo:hi]))
            y_diag = jnp.where(lane < P_B, yd[0], yd[1])
            hp = h_scr[lo:hi, :]
            y_off = _dot_nt(cm, hp.astype(BF16)) * dec_in[:, lo:hi]
            st = _dot_tn(xe_b[:, lo:hi], bm)
            a_last = jnp.where(prow < P_B, cum_t[2 * pr:2 * pr + 1, Q - 1:Q],
                               cum_t[2 * pr + 1:2 * pr + 2, Q - 1:Q])
            h_scr[lo:hi, :] = hp * jnp.exp(a_last) + st
            pairs.append(y_diag + y_off + xs[:, lo:hi] * dskip_ref[:, lo:hi])
    y = jnp.concatenate(pairs, axis=-1) * _silu(z_ref[...])
    gw = D_B // G_B
    y = jnp.concatenate([_rms(y[:, g * gw:(g + 1) * gw]) for g in range(G_B)], axis=-1) * ng_ref[...]
    y_ref[...] = y.astype(y_ref.dtype)

    @pl.when(c == pl.num_programs(1) - 1)
    def _():
        hout_ref[...] = h_scr[...]


def _prompt_ssd(xbc, sm, z, p, Q):
    b, L, _ = xbc.shape
    full = lambda a: pl.BlockSpec(a.shape, lambda i, j: (0,) * a.ndim)
    params = [p["ssm_conv_w"], p["ssm_conv_b"], p["dtb_small"], p["alog_small"],
              p["ssm_d_e"], p["ssm_norm_g"], p["expand"]]
    return pl.pallas_call(
        functools.partial(_ssd_kernel, Q=Q),
        grid=(b, L // Q),
        in_specs=[pl.BlockSpec((None, Q, CONV_B), lambda i, j: (i, j, 0)),
                  pl.BlockSpec((None, Q, SMALL_W), lambda i, j: (i, j, 0)),
                  pl.BlockSpec((None, Q, D_B), lambda i, j: (i, j, D_A // D_B))]
                 + [full(a) for a in params],
        out_specs=[pl.BlockSpec((None, Q, D_B), lambda i, j: (i, j, 0)),
                   pl.BlockSpec((None, D_B, N_B), lambda i, j: (i, 0, 0))],
        out_shape=[jax.ShapeDtypeStruct((b, L, D_B), BF16),
                   jax.ShapeDtypeStruct((b, D_B, N_B), F32)],
        scratch_shapes=[pltpu.VMEM((Q + SUBLANES, CONV_B), F32), pltpu.VMEM((D_B, N_B), F32)],
        compiler_params=_cparams("arbitrary", "arbitrary"),
    )(xbc, sm, z, *params)


def _split2(a):
    hi = a.astype(BF16)
    return hi, (a - hi.astype(F32)).astype(BF16)


def _split3(x):
    x1 = x.astype(BF16)
    r1 = x - x1.astype(F32)
    x2 = r1.astype(BF16)
    return x1, x2, (r1 - x2.astype(F32)).astype(BF16)


def _dot_exact_rhs(a_bf16, x):
    x1, x2, x3 = _split3(x)
    return _dot(a_bf16, x1) + (_dot(a_bf16, x2) + _dot(a_bf16, x3))


def _dot_exact_lhs(x, a_bf16):
    x1, x2, x3 = _split3(x)
    return _dot(x1, a_bf16) + (_dot(x2, a_bf16) + _dot(x3, a_bf16))


def _gdn_head_prepare(act, gcum, gcum_t, beta_s, r0, h):
    cs = GDN_CHUNK
    rows = slice(r0, r0 + cs)
    q = act[rows, h * DK_C:(h + 1) * DK_C]
    k = act[rows, D_C + h * DK_C:D_C + (h + 1) * DK_C]
    v = act[rows, 2 * D_C + h * DV_C:2 * D_C + (h + 1) * DV_C]
    q = q * lax.rsqrt(jnp.sum(q * q, axis=-1, keepdims=True) + EPS) * (DK_C ** -0.5)
    k = k * lax.rsqrt(jnp.sum(k * k, axis=-1, keepdims=True) + EPS)
    beta = beta_s[rows, BETA_LANE + h:BETA_LANE + h + 1]
    gc = gcum[rows, AG_LANE + h:AG_LANE + h + 1]
    gr = gcum_t[AG_LANE + h:AG_LANE + h + 1, rows]
    g_last = gcum[r0 + cs - 1:r0 + cs, AG_LANE + h:AG_LANE + h + 1]
    ri = _iota((cs, 2 * cs), 0)
    ci = _iota((cs, 2 * cs), 1) & (cs - 1)
    decay = jnp.exp(jnp.where(ri >= ci, gc - jnp.concatenate([gr, gr], axis=1), -jnp.inf))
    kb = k.astype(BF16)
    kk = _dot_nt(kb, jnp.concatenate([kb, kb], axis=0))
    p = jnp.where(ri > ci, -(beta * kk * decay), 0.0)
    rhs = jnp.concatenate([v * beta, k * (beta * jnp.exp(gc))], axis=-1)
    attn = (_dot_nt(q.astype(BF16), kb) * decay[:, 0:cs]).astype(BF16)
    qg = q * jnp.exp(gc)
    kd = (k * jnp.exp(g_last - gc)).astype(BF16)
    return p, rhs, attn, qg, kd, jnp.exp(g_last)


def _gdn_head_finish(sol, attn, qg, kd, e_last):
    x_b = sol.astype(BF16)
    ax = _dot(attn, x_b)
    kx = _dot_tn(kd, x_b)
    lhs = jnp.concatenate([kx[:, DV_C:2 * DV_C], qg - ax[:, DV_C:2 * DV_C]], axis=0).astype(BF16)
    return lhs, kx[:, 0:DV_C], ax[:, 0:DV_C], e_last


def _split_lhs(p):
    hi, lo = _split2(p)
    half = jnp.where(_iota(p.shape, 1) < GDN_CHUNK, hi, lo)
    return jnp.concatenate([half, half], axis=1)


def _split_rhs(x):
    hi, lo = _split2(x)
    return jnp.concatenate([hi, hi, lo, lo], axis=0)


def _neumann_solve(ps, xs, tick):
    cs = GDN_CHUNK
    n_levels = cs.bit_length() - 1
    eye = (_iota((cs, 2 * cs), 0) == (_iota((cs, 2 * cs), 1) & (cs - 1))).astype(F32)
    ts = [eye + p for p in ps]
    tick()
    for _ in range(1, n_levels):
        ps = [_dot(_split_lhs(p), _split_rhs(p)) for p in ps]
        ts = [t + _dot(_split_lhs(p), _split_rhs(t)) for p, t in zip(ps, ts)]
        tick()
    return [_dot(_split_lhs(t), _split_rhs(x)) for t, x in zip(ts, xs)]


N_GDN_IN, N_GDN_OUT, N_GDN_SCRATCH = 7, 2, 3


def _gdn_kernel(*refs, R, rider=None):
    if rider is None:
        ins, refs = refs[:N_GDN_IN], refs[N_GDN_IN:]
        outs, scratch = refs[:N_GDN_OUT], refs[N_GDN_OUT:]
    else:
        G, spb, layer, lam_init = rider
        n_dec = N_DECODE_IN
        pt_ref, refs = refs[0], refs[1:]
        ins, dec_ins, refs = refs[:N_GDN_IN], refs[N_GDN_IN:N_GDN_IN + n_dec], refs[N_GDN_IN + n_dec:]
        outs, dec_out, refs = refs[:N_GDN_OUT], refs[N_GDN_OUT], refs[N_GDN_OUT + 1:]
        scratch, dec_scratch = refs[:N_GDN_SCRATCH], refs[N_GDN_SCRATCH:]
    qkv_ref, sm_ref, z_ref, cw_ref, dtb_s_ref, alog_s_ref, ng_ref = ins
    o_ref, sout_ref = outs
    xp_scr, s_scr, act_scr = scratch
    c = pl.program_id(1)

    @pl.when(c == 0)
    def _():
        xp_scr[0:SUBLANES, :] = jnp.zeros((SUBLANES, CONV_C), F32)
        s_scr[...] = jnp.zeros(s_scr.shape, F32)

    cs = GDN_CHUNK
    n_chunks = R // cs
    if rider is not None:
        step = pl.program_id(0) * pl.num_programs(1) + c
        n_steps = pl.num_programs(0) * pl.num_programs(1)
        stages = _decode_stages(pt_ref, dec_ins, dec_out, dec_scratch, step, n_steps,
                                lax.rem(step, spb) == 0, G, layer, lam_init)
        n_ticks = n_chunks * (H_C + 1) + 2 * (GDN_CHUNK.bit_length() - 1)
        rider_stages = _Interleaver(stages, n_ticks)
    else:
        rider_stages = _Interleaver([], 1)
    tick = rider_stages.tick

    act_scr[...] = _silu(_conv_taps(xp_scr, qkv_ref[...], cw_ref, R))
    sm = sm_ref[...]
    g_s = _softplus(sm + dtb_s_ref[...]) * (-jnp.exp(alog_s_ref[...]))
    beta_s = jax.nn.sigmoid(sm)
    ri = _iota((R, R), 0)
    ci = _iota((R, R), 1)
    chunk_tril = (lax.shift_right_logical(ri, 6) == lax.shift_right_logical(ci, 6)) & (ri >= ci)
    gcum = _dot_exact_rhs(chunk_tril.astype(BF16), g_s)
    gcum_t = gcum.T

    def state_free_part(chunks):
        prep = []
        for ch in chunks:
            for h in range(H_C):
                prep.append(_gdn_head_prepare(act_scr, gcum, gcum_t, beta_s, ch * cs, h))
                tick()
        sols = _neumann_solve([pr[0] for pr in prep], [pr[1] for pr in prep], tick)
        return [_gdn_head_finish(sol, *pr[2:]) for sol, pr in zip(sols, prep)]

    def state_step(ch, heads):
        tick()
        outs = []
        for h, (lhs, c_add, o_add, e_last) in enumerate(heads):
            s_old = s_scr[h * DK_C:(h + 1) * DK_C, :]
            r = _dot(lhs, s_old.astype(BF16))
            s_scr[h * DK_C:(h + 1) * DK_C, :] = s_old * e_last + (c_add - r[0:DK_C, :])
            outs.append(_rms(r[DK_C:DK_C + cs, :] + o_add) * ng_ref[...])
        rows = slice(ch * cs, (ch + 1) * cs)
        o_ref[rows, :] = (jnp.concatenate(outs, axis=-1) * _silu(z_ref[rows, :])).astype(o_ref.dtype)

    half = max(n_chunks // 2, 1)
    for chunks in (range(0, half), range(half, n_chunks)):
        parts = state_free_part(chunks)
        for n, ch in enumerate(chunks):
            state_step(ch, parts[n * H_C:(n + 1) * H_C])
    rider_stages.finish()

    @pl.when(c == pl.num_programs(1) - 1)
    def _():
        sout_ref[...] = s_scr[...]


def _prompt_gdn(qkv, sm, z, p, R, decode=None):
    b, L, _ = qkv.shape
    nj = L // R
    full = lambda a: pl.BlockSpec(a.shape, lambda *_: (0,) * a.ndim)
    params = [p["gdn_conv_w"], p["dtb_small"], p["alog_small"], p["gdn_norm_g"]]
    in_specs = [pl.BlockSpec((None, R, CONV_C), lambda i, j, *_: (i, j, 0)),
                pl.BlockSpec((None, R, SMALL_W), lambda i, j, *_: (i, j, 0)),
                pl.BlockSpec((None, R, D_C), lambda i, j, *_: (i, j, (D_A + D_B) // D_C))] + [full(a) for a in params]
    out_specs = [pl.BlockSpec((None, R, D_C), lambda i, j, *_: (i, j, 0)),
                 pl.BlockSpec((None, H_C * DK_C, DV_C), lambda i, j, *_: (i, 0, 0))]
    out_shape = [jax.ShapeDtypeStruct((b, L, D_C), BF16), jax.ShapeDtypeStruct((b, H_C * DK_C, DV_C), F32)]
    scratch = [pltpu.VMEM((R + SUBLANES, CONV_C), F32), pltpu.VMEM((H_C * DK_C, DV_C), F32),
               pltpu.VMEM((R, CONV_C), F32)]
    args = [qkv, sm, z, *params]
    if decode is None:
        return pl.pallas_call(
            functools.partial(_gdn_kernel, R=R), grid=(b, nj), in_specs=in_specs, out_specs=out_specs,
            out_shape=out_shape, scratch_shapes=scratch, compiler_params=_cparams("arbitrary", "arbitrary"),
        )(*args)
    layer, page_table, lam_init, G, *decode_args = decode
    steps_table, d_in, d_args, d_out, d_shape, d_scratch, spb = _decode_operands(
        layer, G, lambda i, j: i * nj + j, page_table, *decode_args)
    return pl.pallas_call(
        functools.partial(_gdn_kernel, R=R, rider=(G, spb, layer, lam_init)),
        grid_spec=pltpu.PrefetchScalarGridSpec(
            num_scalar_prefetch=1, grid=(b, nj), in_specs=in_specs + d_in, out_specs=out_specs + [d_out],
            scratch_shapes=scratch + d_scratch),
        out_shape=out_shape + [d_shape],
        compiler_params=_cparams("arbitrary", "arbitrary"),
    )(steps_table, *args, *d_args)


def _expand_heads(vals, n_heads, width):
    lane = _iota((1, n_heads * width), 1)
    out = jnp.zeros((1, n_heads * width), F32)
    for h in range(n_heads):
        out = jnp.where((lane >= h * width) & (lane < (h + 1) * width), vals[:, h:h + 1], out)
    return out


def _pad_rows(rows):
    n = rows[0].shape[-1]
    r = _iota((SUBLANES, n), 0)
    out = jnp.zeros((SUBLANES, n), F32)
    for i, v in enumerate(rows):
        out = jnp.where(r == i, v, out)
    return out


def _sample_step_kernel(*refs, n_data, n_params, block_b):
    data, params, outs = refs[:n_data], refs[n_data:n_data + n_params], refs[n_data + n_params:]
    for bi in range(block_b):
        _sample_step_one(*[r.at[bi] for r in data], *params, *[r.at[bi] for r in outs])


def _sample_step_one(xbc_ref, qkv_ref, sm_ref, z_ref, cst_b_ref, cst_c_ref, h0_ref, s0_ref,
                     cwb_ref, cbb_ref, cwc_ref, dtb_s_ref, alog_s_ref, dskip_ref, ngb_ref, ngc_ref,
                     ob_ref, oc_ref, cnew_b_ref, cnew_c_ref, h1_ref, s1_ref):
    def conv(x, st_ref, w_ref, new_ref):
        y = w_ref[CONV_W - 1:CONV_W, :] * x
        for i in range(CONV_W - 1):
            y = y + w_ref[i:i + 1, :] * st_ref[i:i + 1, :]
        for i in range(CONV_W - 2):
            new_ref[i:i + 1, :] = st_ref[i + 1:i + 2, :]
        new_ref[CONV_W - 2:CONV_W - 1, :] = x
        return y

    sm = sm_ref[...]
    dt_or_sp = _softplus(sm + dtb_s_ref[...])
    a_s = dt_or_sp * (-jnp.exp(alog_s_ref[...]))
    z = z_ref[...]

    xa = _silu(conv(xbc_ref[...], cst_b_ref, cwb_ref, cnew_b_ref) + cbb_ref[...])
    xs = xa[:, 0:D_B]
    dt_e = _expand_heads(dt_or_sp[:, DT_LANE:DT_LANE + H_B], H_B, P_B)
    a_e = _expand_heads(a_s[:, DT_LANE:DT_LANE + H_B], H_B, P_B)
    xc = xs * dt_e
    half = D_B // G_B
    lane_b = _iota((1, D_B), 1)
    bm = [xa[:, D_B + g * N_B:D_B + (g + 1) * N_B] for g in range(G_B)]
    cm = [xa[:, D_B + G_B * N_B + g * N_B:D_B + G_B * N_B + (g + 1) * N_B] for g in range(G_B)]
    h0 = h0_ref[...]
    coff = _dot_nt(_pad_rows(cm).astype(BF16), h0.astype(BF16))
    y_off = jnp.where(lane_b < half, coff[0:1, :], coff[1:2, :]) * jnp.exp(a_e)
    cb = [jnp.sum(cm[g] * bm[g], axis=-1, keepdims=True) for g in range(G_B)]
    y_diag = jnp.where(lane_b < half, cb[0], cb[1]) * xc
    y = (y_diag + y_off + xs * dskip_ref[...]) * _silu(z[:, D_A:D_A + D_B])
    y = jnp.concatenate([_rms(y[:, g * half:(g + 1) * half]) for g in range(G_B)], axis=-1)
    ob_ref[...] = y * ngb_ref[...]
    xc_rows = _pad_rows([jnp.where(lane_b < half, xc, 0.0), jnp.where(lane_b >= half, xc, 0.0)])
    outer = _dot_tn(xc_rows.astype(BF16), _pad_rows(bm).astype(BF16))
    state_row = _iota((D_B, N_B), 0)
    dec = jnp.exp(a_s)
    dec_col = jnp.zeros((D_B, N_B), F32)
    for h in range(H_B):
        in_head = (state_row >= h * P_B) & (state_row < (h + 1) * P_B)
        dec_col = jnp.where(in_head, dec[:, DT_LANE + h:DT_LANE + h + 1], dec_col)
    h1_ref[...] = h0 * dec_col + outer

    qkv = _silu(conv(qkv_ref[...], cst_c_ref, cwc_ref, cnew_c_ref))
    beta_s = jax.nn.sigmoid(sm)
    outs = []
    for h in range(H_C):
        q = qkv[:, h * DK_C:(h + 1) * DK_C]
        k = qkv[:, D_C + h * DK_C:D_C + (h + 1) * DK_C]
        v = qkv[:, 2 * D_C + h * DV_C:2 * D_C + (h + 1) * DV_C]
        q = q * lax.rsqrt(jnp.sum(q * q, axis=-1, keepdims=True) + EPS) * (DK_C ** -0.5)
        k = k * lax.rsqrt(jnp.sum(k * k, axis=-1, keepdims=True) + EPS)
        beta = beta_s[:, BETA_LANE + h:BETA_LANE + h + 1]
        eg = jnp.exp(a_s[:, AG_LANE + h:AG_LANE + h + 1])
        s_old = s0_ref[h * DK_C:(h + 1) * DK_C, :]
        ws = _dot(_pad_rows([k * (beta * eg), q * eg]).astype(BF16), s_old.astype(BF16))
        v_new = v * beta - ws[0:1, :]
        o = ws[1:2, :] + jnp.sum(q * k, axis=-1, keepdims=True) * v_new
        upd = _dot_tn(_pad_rows([k]).astype(BF16), _pad_rows([v_new]).astype(BF16))
        s1_ref[h * DK_C:(h + 1) * DK_C, :] = s_old * eg + upd
        outs.append(_rms(o) * ngc_ref[...])
    oc_ref[...] = jnp.concatenate(outs, axis=-1) * _silu(z[:, D_A + D_B:D_MIX])


def _sample_step(layer, xbc, qkv, sm, z, cst_b, cst_c, h0, s0, p):
    bs = xbc.shape[0]
    block_b = _pick(bs, (4, 2, 1))
    per_b = lambda a: pl.BlockSpec((block_b,) + tuple(a.shape[1:]), lambda i: (i,) + (0,) * (len(a.shape) - 1))
    of_layer = lambda a: pl.BlockSpec((None, block_b) + tuple(a.shape[2:]),
                                      lambda i: (layer, i) + (0,) * (len(a.shape) - 2))
    full = lambda a: pl.BlockSpec(a.shape, lambda i: (0,) * a.ndim)
    rows, states = [xbc, qkv, sm, z], [cst_b, cst_c, h0, s0]
    data = rows + states
    params = [p["ssm_conv_w"], p["ssm_conv_b"], p["gdn_conv_w"], p["dtb_small"], p["alog_small"],
              p["ssm_d_e"], p["ssm_norm_g"], p["gdn_norm_g"]]
    outs = [jax.ShapeDtypeStruct((bs, 1, D_B), F32), jax.ShapeDtypeStruct((bs, 1, D_C), F32)] + [
        jax.ShapeDtypeStruct(a.shape[1:], F32) for a in states]
    return pl.pallas_call(
        functools.partial(_sample_step_kernel, n_data=len(data), n_params=len(params), block_b=block_b),
        grid=(bs // block_b,),
        in_specs=[per_b(a) for a in rows] + [of_layer(a) for a in states] + [full(a) for a in params],
        out_specs=[per_b(a) for a in outs],
        out_shape=outs,
        compiler_params=_cparams("arbitrary"),
    )(*data, *params)


def _split_w_in(w):
    sizes = (2 * H_A * DH_A, 2 * H_A * DH_A, D_A, CONV_B, H_B, CONV_C, H_C, H_C, D_MIX)
    offs = [0]
    for s in sizes:
        offs.append(offs[-1] + s)
    col = lambda i, j=None: w[:, offs[i]:offs[(i if j is None else j) + 1]].astype(BF16)
    small = jnp.concatenate([col(4), col(6), col(7),
                             jnp.zeros((w.shape[0], SMALL_W - H_B - 2 * H_C), BF16)], axis=1)
    return [col(0, 3), col(5), col(8), small]


def _layer_params(l, w):
    pad_small = lambda a_ssm, a_gdn: jnp.concatenate(
        [a_ssm, jnp.zeros((H_C,), F32), a_gdn, jnp.zeros((SMALL_W - H_B - 2 * H_C,), F32)]).reshape(1, SMALL_W)
    rep = lambda a: jnp.repeat(a, P_B).reshape(1, D_B)
    expand = (jnp.arange(SMALL_W)[:, None] == (jnp.arange(D_B)[None, :] // P_B)).astype(BF16)
    return {
        "ssm_conv_w": w["ssm_conv_w"][l], "ssm_conv_b": w["ssm_conv_b"][l].reshape(1, CONV_B),
        "gdn_conv_w": w["gdn_conv_w"][l],
        "dtb_small": pad_small(w["ssm_dt_bias"][l], w["gdn_dt_bias"][l]),
        "alog_small": pad_small(w["ssm_A_log"][l], w["gdn_A_log"][l]),
        "ssm_d_e": rep(w["ssm_D"][l]),
        "ssm_norm_g": w["ssm_norm_g"][l].reshape(1, D_B), "gdn_norm_g": w["gdn_norm_g"][l].reshape(1, DV_C),
        "expand": expand,
    }


def _pick(n, prefs):
    for t in prefs:
        if n % t == 0:
            return t
    return n


def kernel(x_prompt, x_sample, c_prompt, c_sample, cache_k, cache_v, state_ssm, state_ssm_conv, state_gdn, state_gdn_conv, page_table, w_ada, b_ada, w_in, w_out, rel_bias, attn_lambda, attn_subln_g, ssm_conv_w, ssm_conv_b, ssm_dt_bias, ssm_A_log, ssm_D, ssm_norm_g, gdn_conv_w, gdn_dt_bias, gdn_A_log, gdn_norm_g, final_norm_g):
    depth = w_in.shape[0]
    bp, L, _ = x_prompt.shape
    bs = x_sample.shape[0]
    n_pages = page_table.shape[1]
    past = n_pages * PAGE_SIZE
    n_pool = cache_k.shape[1]
    wts = dict(ssm_conv_w=ssm_conv_w, ssm_conv_b=ssm_conv_b, ssm_dt_bias=ssm_dt_bias, ssm_A_log=ssm_A_log,
               ssm_D=ssm_D, ssm_norm_g=ssm_norm_g, gdn_conv_w=gdn_conv_w, gdn_dt_bias=gdn_dt_bias,
               gdn_A_log=gdn_A_log, gdn_norm_g=gdn_norm_g)

    tm = _pick(L, (512, 256, 128, 64))
    tm_out = _pick(L, (1024, 512, 256, 128, 64))
    T = _pick(L, (256, 128))
    Q = _pick(L, (256, 128, 64))
    R = _pick(L, (256, 128, 64))
    G = _pick(n_pages, (32, 16, 8, 4, 2, 1))
    gdn_steps = bp * (L // R)
    G_ride = (bs * n_pages) // gdn_steps if (bs * n_pages) % gdn_steps == 0 else 0
    if not (1 <= G_ride <= MAX_PAGES_PER_STEP and n_pages % G_ride == 0):
        G_ride = 0

    mod = _modulation(jnp.concatenate([c_prompt, c_sample], axis=0), w_ada, b_ada)
    bias_near = _near_bias_tiles(rel_bias, T)
    bias_pages = _decode_bias(rel_bias, past)
    bias_new = jnp.broadcast_to(jnp.tile(rel_bias[0].astype(F32), 2)[:, None], (2 * H_A, DV_A))
    ck = cache_k.reshape(depth, n_pool, PAGE_SIZE * H_A, 2 * DH_A)
    cv = cache_v.reshape(depth, n_pool, PAGE_SIZE * H_A, DV_A)
    lane_half = jnp.arange(DV_A) < DH_A

    xp = x_prompt
    xs = x_sample.reshape(1, bs, D_MODEL)
    new_p, new_s = [], []
    kv_rows = None
    for l in range(depth):
        lam_init = 0.8 - 0.6 * math.exp(-0.3 * l)
        p = _layer_params(l, wts)
        w_perm = _split_w_in(w_in[l])
        w_o = w_out[l].astype(BF16)
        shift, scale, gate = jnp.split(mod[l], 3, axis=-1)
        mp = lambda a: a[:bp].reshape(bp, 1, D_MODEL)
        ms = lambda a: a[bp:].reshape(1, bs, D_MODEL)

        final = final_norm_g if l == depth - 1 else None
        q, k, v, xbc, qkv, z, sm, *kv_rows = _in_projection(xp, mp(scale), mp(shift), w_perm, tm,
                                                            (l, depth, kv_rows))
        q_s, k_s, v_s, xbc_s, qkv_s, z_s, sm_s = _in_projection(xs, ms(scale), ms(shift), w_perm, bs)
        q4 = q_s.reshape(bs, H_A, DV_A) * (DH_A ** -0.5)
        qpat = jnp.concatenate([jnp.where(lane_half, q4, 0.0), jnp.where(lane_half, 0.0, q4)], axis=1)
        k4 = k_s.reshape(bs, H_A, DV_A)
        v4 = v_s.reshape(bs, H_A, DV_A)
        decode_args = (ck, cv, qpat, jnp.concatenate([k4, k4], axis=1), jnp.concatenate([v4, v4], axis=1),
                       bias_pages, bias_new, z_s[..., :D_A].reshape(bs, H_A, DV_A), attn_lambda[l],
                       attn_subln_g[l])

        o_a = _prompt_attention(q, k, v, z, bias_near, attn_lambda[l], attn_subln_g[l], lam_init, T)
        o_b, ssm_h = _prompt_ssd(xbc, sm, z, p, Q)
        if G_ride:
            o_c, gdn_s, o_a_s = _prompt_gdn(qkv, sm, z, p, R, (l, page_table, lam_init, G_ride, *decode_args))
        else:
            o_c, gdn_s = _prompt_gdn(qkv, sm, z, p, R)
            o_a_s = _decode_attention(l, page_table, lam_init, G, *decode_args)
        xp = _out_projection(xp, mp(gate), o_a, o_b, o_c, w_o, final, tm_out)
        new_p.append((ssm_h.reshape(bp, H_B, P_B, N_B), xbc[:, L - (CONV_W - 1):, :],
                      gdn_s.reshape(bp, H_C, DK_C, DV_C), qkv[:, L - (CONV_W - 1):, :]))

        row = lambda a: a.reshape(bs, 1, a.shape[-1])
        o_b_s, o_c_s, conv_b1, conv_c1, ssm_h_s, gdn_s_s = _sample_step(
            l, row(xbc_s), row(qkv_s), row(sm_s), row(z_s), state_ssm_conv, state_gdn_conv,
            state_ssm.reshape(depth, bs, D_B, N_B), state_gdn.reshape(depth, bs, H_C * DK_C, DV_C), p)
        xs = _out_projection(xs, ms(gate), o_a_s.reshape(1, bs, D_A).astype(BF16),
                             o_b_s.reshape(1, bs, D_B).astype(BF16), o_c_s.reshape(1, bs, D_C).astype(BF16),
                             w_o, final, bs)
        new_s.append((k_s.reshape(bs, 1, H_A, 2 * DH_A), v_s.reshape(bs, 1, H_A, DV_A),
                      ssm_h_s.reshape(bs, H_B, P_B, N_B), conv_b1,
                      gdn_s_s.reshape(bs, H_C, DK_C, DV_C), conv_c1))

    k_p, v_p = kv_rows
    ssm_p, ssm_conv_p, gdn_p, gdn_conv_p = [jnp.stack(a) for a in zip(*new_p)]
    k_s, v_s, ssm_s, ssm_conv_s, gdn_s, gdn_conv_s = [jnp.stack(a) for a in zip(*new_s)]
    return (xp, xs.reshape(bs, 1, D_MODEL), k_p, v_p, ssm_p, ssm_conv_p, gdn_p, gdn_conv_p,
            k_s, v_s, ssm_s, ssm_conv_s, gdn_s, gdn_conv_s)
```
